```python
import math
import jax, jax.numpy as jnp
from jax import lax
import numpy as np

D_MODEL = 1024
BATCH = 8
SEQ = 8192
DEPTH = 4

META_TOKENS = 16
POOL_WIDTH = D_MODEL
POOL_WINDOWS = (2, 4, 8, 16)
POOL_GROUPS = len(POOL_WINDOWS)
POOL_GROUP_DIM = POOL_WIDTH // POOL_GROUPS
N_HEADS = 16
HEAD_DIM = 64
ATTN_WIDTH = N_HEADS * HEAD_DIM
Q_BLOCK = 128
ATTN_PAD = (-META_TOKENS) % Q_BLOCK
D_FF = int(math.ceil(8 * D_MODEL / 3 / 256) * 256)
RMS_EPS = 1e-6
NEG_INF = -1e30
SPLIT_SIZES = (POOL_WIDTH, ATTN_WIDTH, ATTN_WIDTH, ATTN_WIDTH, N_HEADS, D_MODEL, D_MODEL)
SPLIT_POINTS = tuple(int(v) for v in np.cumsum(SPLIT_SIZES)[:-1])
N_IN = int(sum(SPLIT_SIZES))

kernel_name = "gated_pool_forgetting_attn_hybrid"


def rms_norm(x, g):
    xf = x.astype(jnp.float32)
    y = xf * lax.rsqrt(jnp.mean(xf * xf, axis=-1, keepdims=True) + RMS_EPS)
    return (y * g.astype(jnp.float32)).astype(x.dtype)


def multiscale_pool(u, w_pool, scale):
    B, T, _ = u.shape
    ug = u.reshape(B, T, POOL_GROUPS, POOL_GROUP_DIM)
    c0 = jnp.concatenate(
        [jnp.zeros((B, 1, POOL_GROUPS, POOL_GROUP_DIM), jnp.float32),
         jnp.cumsum(ug.astype(jnp.float32), axis=1)], axis=1)
    pos1 = jnp.arange(1, T + 1)
    outs = []
    for g, w in enumerate(POOL_WINDOWS):
        lag_idx = jnp.maximum(pos1 - w, 0)
        window_sum = c0[:, 1:, g] - jnp.take(c0[:, :, g], lag_idx, axis=1)
        count = jnp.minimum(pos1, w).astype(jnp.float32)[None, :, None]
        diff = (window_sum / count).astype(u.dtype) - ug[:, :, g]
        outs.append(jnp.einsum('btc,cd->btd', diff, w_pool[g]))
    return jnp.concatenate(outs, axis=-1) * scale


def forgetting_attention(q, k, v, log_f):
    B, T, H, Dh = q.shape
    pad4 = ((0, 0), (ATTN_PAD, 0), (0, 0), (0, 0))
    q = jnp.pad(q, pad4)
    k = jnp.pad(k, pad4)
    v = jnp.pad(v, pad4)
    F = jnp.cumsum(jnp.pad(log_f.astype(jnp.float32), ((0, 0), (ATTN_PAD, 0), (0, 0))), axis=1)
    L = T + ATTN_PAD
    n_blocks = L // Q_BLOCK
    key_pos = jnp.arange(L)
    key_valid = key_pos >= ATTN_PAD
    F_k = jnp.transpose(F, (0, 2, 1))[:, :, None, :]
    scale = 1.0 / math.sqrt(Dh)

    def block(i):
        start = i * Q_BLOCK
        qb = lax.dynamic_slice_in_dim(q, start, Q_BLOCK, axis=1)
        Fq = lax.dynamic_slice_in_dim(F, start, Q_BLOCK, axis=1)
        s = jnp.einsum('bqhd,bkhd->bhqk', qb, k, preferred_element_type=jnp.float32) * scale
        s = s + jnp.transpose(Fq, (0, 2, 1))[:, :, :, None] - F_k
        q_pos = start + jnp.arange(Q_BLOCK)
        mask = (key_pos[None, :] <= q_pos[:, None]) & key_valid[None, :]
        s = jnp.where(mask[None, None], s, NEG_INF)
        p = jax.nn.softmax(s, axis=-1)
        return jnp.einsum('bhqk,bkhd->bqhd', p.astype(v.dtype), v)

    out = lax.map(block, jnp.arange(n_blocks))
    out = jnp.transpose(out, (1, 0, 2, 3, 4)).reshape(B, L, H, Dh)
    return out[:, ATTN_PAD:]


def _fwd_setup_inputs(seed: int = 0) -> dict:
    key = jax.random.key(seed)
    ks = jax.random.split(key, 16)
    f32 = jnp.float32
    nrm = lambda k, shape, s: jax.random.normal(k, shape, f32) * s
    gain = lambda k: 1.0 + 0.05 * jax.random.normal(k, (DEPTH, D_MODEL), f32)
    return {
        "x": jax.random.normal(ks[0], (BATCH, SEQ, D_MODEL), f32),
        "meta_tokens": nrm(ks[1], (META_TOKENS, D_MODEL), 1.0),
        "norm_mix_pre": gain(ks[2]),
        "norm_mix_post": gain(ks[3]),
        "norm_ffn_pre": gain(ks[4]),
        "norm_ffn_post": gain(ks[5]),
        "w_in": nrm(ks[6], (DEPTH, D_MODEL, N_IN), D_MODEL ** -0.5),
        "b_forget": jax.random.uniform(ks[7], (DEPTH, N_HEADS), f32, 1.0, 4.0),
        "w_pool": nrm(ks[8], (DEPTH, POOL_GROUPS, POOL_GROUP_DIM, POOL_GROUP_DIM), POOL_GROUP_DIM ** -0.5),
        "pool_scale": 1.0 + 0.05 * jax.random.normal(ks[9], (DEPTH, D_MODEL), f32),
        "w_out": nrm(ks[10], (DEPTH, D_MODEL, D_MODEL), D_MODEL ** -0.5),
        "w_ffn_gate": nrm(ks[11], (DEPTH, D_MODEL, D_FF), D_MODEL ** -0.5),
        "w_ffn_up": nrm(ks[12], (DEPTH, D_MODEL, D_FF), D_MODEL ** -0.5),
        "w_ffn_down": nrm(ks[13], (DEPTH, D_FF, D_MODEL), D_FF ** -0.5),
    }


def _fwd_reference(x, meta_tokens, norm_mix_pre, norm_mix_post, norm_ffn_pre, norm_ffn_post,
              w_in, b_forget, w_pool, pool_scale, w_out, w_ffn_gate, w_ffn_up, w_ffn_down):
    B = x.shape[0]
    meta = jnp.broadcast_to(meta_tokens[None].astype(x.dtype), (B, META_TOKENS, D_MODEL))
    h_res = jnp.concatenate([meta, x], axis=1)
    T = h_res.shape[1]
    for l in range(DEPTH):
        h = rms_norm(h_res, norm_mix_pre[l])
        proj = jnp.einsum('btd,dn->btn', h, w_in[l])
        u_pool, q, k, v, f_logit, g_pool, g_attn = jnp.split(proj, SPLIT_POINTS, axis=-1)
        y_pool = multiscale_pool(u_pool, w_pool[l], pool_scale[l])
        log_f = jax.nn.log_sigmoid(f_logit.astype(jnp.float32) + b_forget[l].astype(jnp.float32))
        y_attn = forgetting_attention(
            q.reshape(B, T, N_HEADS, HEAD_DIM), k.reshape(B, T, N_HEADS, HEAD_DIM),
            v.reshape(B, T, N_HEADS, HEAD_DIM), log_f).reshape(B, T, ATTN_WIDTH)
        merged = jax.nn.sigmoid(g_pool) * y_pool + jax.nn.sigmoid(g_attn) * y_attn
        mix_out = jnp.einsum('btd,de->bte', merged, w_out[l])
        h_res = h_res + rms_norm(mix_out, norm_mix_post[l])
        h = rms_norm(h_res, norm_ffn_pre[l])
        ff = jax.nn.silu(jnp.einsum('btd,df->btf', h, w_ffn_gate[l])) * jnp.einsum('btd,df->btf', h, w_ffn_up[l])
        ff_out = jnp.einsum('btf,fd->btd', ff, w_ffn_down[l])
        h_res = h_res + rms_norm(ff_out, norm_ffn_post[l])
    return h_res[:, META_TOKENS:]


import jax as _jax
import jax.numpy as _jnp

TWIN_FORMAT = 'train_step'
FWD_PARAMS = ['x', 'meta_tokens', 'norm_mix_pre', 'norm_mix_post', 'norm_ffn_pre', 'norm_ffn_post', 'w_in', 'b_forget', 'w_pool', 'pool_scale', 'w_out', 'w_ffn_gate', 'w_ffn_up', 'w_ffn_down']
TWIN_WEIGHTS = ['meta_tokens', 'norm_mix_pre', 'norm_mix_post', 'norm_ffn_pre', 'norm_ffn_post', 'w_in', 'b_forget', 'w_pool', 'pool_scale', 'w_out', 'w_ffn_gate', 'w_ffn_up', 'w_ffn_down']
TWIN_DIFF_INPUT = 'x'
TWIN_INPUTS = ['x', 'meta_tokens', 'norm_mix_pre', 'norm_mix_post', 'norm_ffn_pre', 'norm_ffn_post', 'w_in', 'b_forget', 'w_pool', 'pool_scale', 'w_out', 'w_ffn_gate', 'w_ffn_up', 'w_ffn_down', 'loss_target', 'm_meta_tokens', 'm_norm_mix_pre', 'm_norm_mix_post', 'm_norm_ffn_pre', 'm_norm_ffn_post', 'm_w_in', 'm_b_forget', 'm_w_pool', 'm_pool_scale', 'm_w_out', 'm_w_ffn_gate', 'm_w_ffn_up', 'm_w_ffn_down', 'v_meta_tokens', 'v_norm_mix_pre', 'v_norm_mix_post', 'v_norm_ffn_pre', 'v_norm_ffn_post', 'v_w_in', 'v_b_forget', 'v_w_pool', 'v_pool_scale', 'v_w_out', 'v_w_ffn_gate', 'v_w_ffn_up', 'v_w_ffn_down']
TWIN_OUTPUTS = ['loss', 'grad_x', 'grad_meta_tokens', 'grad_norm_mix_pre', 'grad_norm_mix_post', 'grad_norm_ffn_pre', 'grad_norm_ffn_post', 'grad_w_in', 'grad_b_forget', 'grad_w_pool', 'grad_pool_scale', 'grad_w_out', 'grad_w_ffn_gate', 'grad_w_ffn_up', 'grad_w_ffn_down', 'delta_meta_tokens', 'delta_norm_mix_pre', 'delta_norm_mix_post', 'delta_norm_ffn_pre', 'delta_norm_ffn_post', 'delta_w_in', 'delta_b_forget', 'delta_w_pool', 'delta_pool_scale', 'delta_w_out', 'delta_w_ffn_gate', 'delta_w_ffn_up', 'delta_w_ffn_down', 'new_m_meta_tokens', 'new_m_norm_mix_pre', 'new_m_norm_mix_post', 'new_m_norm_ffn_pre', 'new_m_norm_ffn_post', 'new_m_w_in', 'new_m_b_forget', 'new_m_w_pool', 'new_m_pool_scale', 'new_m_w_out', 'new_m_w_ffn_gate', 'new_m_w_ffn_up', 'new_m_w_ffn_down', 'new_v_meta_tokens', 'new_v_norm_mix_pre', 'new_v_norm_mix_post', 'new_v_norm_ffn_pre', 'new_v_norm_ffn_post', 'new_v_w_in', 'new_v_b_forget', 'new_v_w_pool', 'new_v_pool_scale', 'new_v_w_out', 'new_v_w_ffn_gate', 'new_v_w_ffn_up', 'new_v_w_ffn_down']
TWIN_LEAF_KINDS = {'loss': 'loss', 'grad_x': 'grad_x', 'grad_meta_tokens': 'grad_w', 'grad_norm_mix_pre': 'grad_w', 'grad_norm_mix_post': 'grad_w', 'grad_norm_ffn_pre': 'grad_w', 'grad_norm_ffn_post': 'grad_w', 'grad_w_in': 'grad_w', 'grad_b_forget': 'grad_w', 'grad_w_pool': 'grad_w', 'grad_pool_scale': 'grad_w', 'grad_w_out': 'grad_w', 'grad_w_ffn_gate': 'grad_w', 'grad_w_ffn_up': 'grad_w', 'grad_w_ffn_down': 'grad_w', 'delta_meta_tokens': 'delta_w', 'delta_norm_mix_pre': 'delta_w', 'delta_norm_mix_post': 'delta_w', 'delta_norm_ffn_pre': 'delta_w', 'delta_norm_ffn_post': 'delta_w', 'delta_w_in': 'delta_w', 'delta_b_forget': 'delta_w', 'delta_w_pool': 'delta_w', 'delta_pool_scale': 'delta_w', 'delta_w_out': 'delta_w', 'delta_w_ffn_gate': 'delta_w', 'delta_w_ffn_up': 'delta_w', 'delta_w_ffn_down': 'delta_w', 'new_m_meta_tokens': 'new_m', 'new_m_norm_mix_pre': 'new_m', 'new_m_norm_mix_post': 'new_m', 'new_m_norm_ffn_pre': 'new_m', 'new_m_norm_ffn_post': 'new_m', 'new_m_w_in': 'new_m', 'new_m_b_forget': 'new_m', 'new_m_w_pool': 'new_m', 'new_m_pool_scale': 'new_m', 'new_m_w_out': 'new_m', 'new_m_w_ffn_gate': 'new_m', 'new_m_w_ffn_up': 'new_m', 'new_m_w_ffn_down': 'new_m', 'new_v_meta_tokens': 'new_v', 'new_v_norm_mix_pre': 'new_v', 'new_v_norm_mix_post': 'new_v', 'new_v_norm_ffn_pre': 'new_v', 'new_v_norm_ffn_post': 'new_v', 'new_v_w_in': 'new_v', 'new_v_b_forget': 'new_v', 'new_v_w_pool': 'new_v', 'new_v_pool_scale': 'new_v', 'new_v_w_out': 'new_v', 'new_v_w_ffn_gate': 'new_v', 'new_v_w_ffn_up': 'new_v', 'new_v_w_ffn_down': 'new_v'}


def _forward(args):
    return _fwd_reference(*[args[k] for k in FWD_PARAMS])


def _output_shape():
    def fwd():
        inp = _fwd_setup_inputs(0)
        return _fwd_reference(*[inp[k] for k in FWD_PARAMS])
    out = _jax.eval_shape(fwd)
    return out.shape, out.dtype

N_MICROBATCH = 1
ADAM_LR = 0.001
ADAM_B1 = 0.9
ADAM_B2 = 0.999
ADAM_EPS = 1e-08
ADAM_WD = 0.01
ADAM_STEP = 10
PER_EXAMPLE_BATCH_AXIS = {'x': 0, 'loss_target': 0}
SHARED_INPUTS = []
_WEIGHT_DTYPES = {'meta_tokens': _jnp.float32, 'norm_mix_pre': _jnp.float32, 'norm_mix_post': _jnp.float32, 'norm_ffn_pre': _jnp.float32, 'norm_ffn_post': _jnp.float32, 'w_in': _jnp.float32, 'b_forget': _jnp.float32, 'w_pool': _jnp.float32, 'pool_scale': _jnp.float32, 'w_out': _jnp.float32, 'w_ffn_gate': _jnp.float32, 'w_ffn_up': _jnp.float32, 'w_ffn_down': _jnp.float32}
MOMENT_SCALE = {'meta_tokens': 8.647458e-02, 'norm_mix_pre': 3.392323e+00, 'norm_mix_post': 6.402044e+01, 'norm_ffn_pre': 2.207770e+00, 'norm_ffn_post': 6.393090e+01, 'w_in': 1.382609e+00, 'b_forget': 4.703194e+00, 'w_pool': 3.411458e+00, 'pool_scale': 3.706132e+00, 'w_out': 3.669521e+00, 'w_ffn_gate': 7.865172e-01, 'w_ffn_up': 1.121498e+00, 'w_ffn_down': 1.944059e+00}


def _to_microbatches(a, axis):
    t = _jnp.moveaxis(a, axis, 0)
    t = t.reshape((N_MICROBATCH, t.shape[0] // N_MICROBATCH) + t.shape[1:])
    return _jnp.moveaxis(t, 1, axis + 1)


def setup_inputs(seed: int = 0) -> dict:
    inp = _fwd_setup_inputs(seed)
    key = _jax.random.fold_in(_jax.random.key(seed), 7919)
    shape, _ = _output_shape()
    out = dict(inp)
    out["loss_target"] = _jax.random.normal(_jax.random.fold_in(key, 0), shape, _jnp.float32)
    for i, name in enumerate(TWIN_WEIGHTS):
        w = inp[name].astype(_jnp.float32)
        if MOMENT_SCALE is None:
            s = _jnp.sqrt(_jnp.mean(_jnp.square(w)) + 1e-30)
        else:
            s = MOMENT_SCALE[name]
        km, kv = _jax.random.split(_jax.random.fold_in(key, i + 1))
        out[name] = w
        out["m_" + name] = s * _jax.random.normal(km, w.shape, _jnp.float32)
        out["v_" + name] = (s * s) * _jax.random.uniform(kv, w.shape, _jnp.float32, 0.5, 1.5)
    if N_MICROBATCH > 1:
        for name, axis in PER_EXAMPLE_BATCH_AXIS.items():
            out[name] = _to_microbatches(out[name], axis)
    return {'x': out['x'], 'meta_tokens': out['meta_tokens'], 'norm_mix_pre': out['norm_mix_pre'], 'norm_mix_post': out['norm_mix_post'], 'norm_ffn_pre': out['norm_ffn_pre'], 'norm_ffn_post': out['norm_ffn_post'], 'w_in': out['w_in'], 'b_forget': out['b_forget'], 'w_pool': out['w_pool'], 'pool_scale': out['pool_scale'], 'w_out': out['w_out'], 'w_ffn_gate': out['w_ffn_gate'], 'w_ffn_up': out['w_ffn_up'], 'w_ffn_down': out['w_ffn_down'], 'loss_target': out['loss_target'], 'm_meta_tokens': out['m_meta_tokens'], 'm_norm_mix_pre': out['m_norm_mix_pre'], 'm_norm_mix_post': out['m_norm_mix_post'], 'm_norm_ffn_pre': out['m_norm_ffn_pre'], 'm_norm_ffn_post': out['m_norm_ffn_post'], 'm_w_in': out['m_w_in'], 'm_b_forget': out['m_b_forget'], 'm_w_pool': out['m_w_pool'], 'm_pool_scale': out['m_pool_scale'], 'm_w_out': out['m_w_out'], 'm_w_ffn_gate': out['m_w_ffn_gate'], 'm_w_ffn_up': out['m_w_ffn_up'], 'm_w_ffn_down': out['m_w_ffn_down'], 'v_meta_tokens': out['v_meta_tokens'], 'v_norm_mix_pre': out['v_norm_mix_pre'], 'v_norm_mix_post': out['v_norm_mix_post'], 'v_norm_ffn_pre': out['v_norm_ffn_pre'], 'v_norm_ffn_post': out['v_norm_ffn_post'], 'v_w_in': out['v_w_in'], 'v_b_forget': out['v_b_forget'], 'v_w_pool': out['v_w_pool'], 'v_pool_scale': out['v_pool_scale'], 'v_w_out': out['v_w_out'], 'v_w_ffn_gate': out['v_w_ffn_gate'], 'v_w_ffn_up': out['v_w_ffn_up'], 'v_w_ffn_down': out['v_w_ffn_down']}


def _loss(weights, diff, rest, loss_target):
    with _jax.named_scope("forward"):
        args = {**rest, TWIN_DIFF_INPUT: diff, **{k: w.astype(_WEIGHT_DTYPES[k]) for k, w in weights.items()}}
        y = _forward(args)
    with _jax.named_scope("loss_head"):
        err = _jnp.square(y.astype(_jnp.float32) - loss_target)
        return 0.5 * _jnp.sum(_jnp.mean(err, axis=-1)) if err.ndim else 0.5 * err


def _adamw(w, g, m, v):
    m = ADAM_B1 * m + (1.0 - ADAM_B1) * g
    v = ADAM_B2 * v + (1.0 - ADAM_B2) * _jnp.square(g)
    m_hat = m / (1.0 - ADAM_B1 ** ADAM_STEP)
    v_hat = v / (1.0 - ADAM_B2 ** ADAM_STEP)
    delta = -ADAM_LR * (m_hat / (_jnp.sqrt(v_hat) + ADAM_EPS) + ADAM_WD * w)
    return delta, m, v


def reference(x, meta_tokens, norm_mix_pre, norm_mix_post, norm_ffn_pre, norm_ffn_post, w_in, b_forget, w_pool, pool_scale, w_out, w_ffn_gate, w_ffn_up, w_ffn_down, loss_target, m_meta_tokens, m_norm_mix_pre, m_norm_mix_post, m_norm_ffn_pre, m_norm_ffn_post, m_w_in, m_b_forget, m_w_pool, m_pool_scale, m_w_out, m_w_ffn_gate, m_w_ffn_up, m_w_ffn_down, v_meta_tokens, v_norm_mix_pre, v_norm_mix_post, v_norm_ffn_pre, v_norm_ffn_post, v_w_in, v_b_forget, v_w_pool, v_pool_scale, v_w_out, v_w_ffn_gate, v_w_ffn_up, v_w_ffn_down):
    given = dict(x=x, meta_tokens=meta_tokens, norm_mix_pre=norm_mix_pre, norm_mix_post=norm_mix_post, norm_ffn_pre=norm_ffn_pre, norm_ffn_post=norm_ffn_post, w_in=w_in, b_forget=b_forget, w_pool=w_pool, pool_scale=pool_scale, w_out=w_out, w_ffn_gate=w_ffn_gate, w_ffn_up=w_ffn_up, w_ffn_down=w_ffn_down, loss_target=loss_target, m_meta_tokens=m_meta_tokens, m_norm_mix_pre=m_norm_mix_pre, m_norm_mix_post=m_norm_mix_post, m_norm_ffn_pre=m_norm_ffn_pre, m_norm_ffn_post=m_norm_ffn_post, m_w_in=m_w_in, m_b_forget=m_b_forget, m_w_pool=m_w_pool, m_pool_scale=m_pool_scale, m_w_out=m_w_out, m_w_ffn_gate=m_w_ffn_gate, m_w_ffn_up=m_w_ffn_up, m_w_ffn_down=m_w_ffn_down, v_meta_tokens=v_meta_tokens, v_norm_mix_pre=v_norm_mix_pre, v_norm_mix_post=v_norm_mix_post, v_norm_ffn_pre=v_norm_ffn_pre, v_norm_ffn_post=v_norm_ffn_post, v_w_in=v_w_in, v_b_forget=v_b_forget, v_w_pool=v_w_pool, v_pool_scale=v_pool_scale, v_w_out=v_w_out, v_w_ffn_gate=v_w_ffn_gate, v_w_ffn_up=v_w_ffn_up, v_w_ffn_down=v_w_ffn_down)
    weights = {n: given[n] for n in TWIN_WEIGHTS}
    shared = {n: given[n] for n in SHARED_INPUTS}
    per_example = {n: given[n] for n in ['x']}
    grad_fn = _jax.value_and_grad(_loss, argnums=(0, 1))

    def one_microbatch(ex, loss_target):
        ex = dict(ex)
        diff = ex.pop(TWIN_DIFF_INPUT)
        return grad_fn(weights, diff, {**shared, **ex}, loss_target)

    if N_MICROBATCH == 1:
        loss, (grad_w, grad_x) = one_microbatch(per_example, given["loss_target"])
    else:
        def body(carry, xs):
            loss_sum, grad_sum = carry
            l_k, (gw_k, gx_k) = one_microbatch(xs[0], xs[1])
            with _jax.named_scope("update"):
                return (loss_sum + l_k, _jax.tree.map(_jnp.add, grad_sum, gw_k)), gx_k

        init = (_jnp.zeros((), _jnp.float32), _jax.tree.map(_jnp.zeros_like, weights))
        (loss, grad_w), grad_x = _jax.lax.scan(body, init, (per_example, given["loss_target"]))
    with _jax.named_scope("update"):
        delta_w, new_m, new_v = {}, {}, {}
        for n in TWIN_WEIGHTS:
            delta_w[n], new_m[n], new_v[n] = _adamw(weights[n], grad_w[n], given["m_" + n], given["v_" + n])
    return (loss, grad_x, *[grad_w[n] for n in TWIN_WEIGHTS], *[delta_w[n] for n in TWIN_WEIGHTS],
            *[new_m[n] for n in TWIN_WEIGHTS], *[new_v[n] for n in TWIN_WEIGHTS])
```

```python
import functools
import math

import jax
import jax.numpy as jnp
from jax import lax
from jax.experimental import pallas as pl
from jax.experimental.pallas import tpu as pltpu

F32 = jnp.float32
BF16 = jnp.bfloat16

N_DEV = 8
META_TOKENS = 16
PAD_ROWS = 112
CHUNK = 128
HEAD_DIM = 64
POOL_WINDOWS = (2, 4, 8, 16)
FORGET_PAD = 256
RMS_EPS = 1e-6
NEG_INF = -1e30
ADAM_LR, ADAM_B1, ADAM_B2, ADAM_EPS, ADAM_WD, ADAM_STEP = 0.001, 0.9, 0.999, 1e-08, 0.01, 10

VMEM_LIMIT = 56 * 1024 * 1024
VMEM_TILE_BUDGET = 36 * 1024 * 1024
MESH_ID = pl.DeviceIdType.MESH


def _params(sem, vmem=VMEM_LIMIT):
    return pltpu.CompilerParams(dimension_semantics=sem, vmem_limit_bytes=vmem)


def _divisors(n, mult):
    return [d for d in range(mult, n + 1, mult) if n % d == 0]


def _row_tile(rows, cap):
    return max(d for d in _divisors(rows, CHUNK) if d <= max(cap, CHUNK))


def _fold8(x):
    r, c = x.shape
    return x.reshape(r // 8, 8, c).sum(axis=0)


def _split_bf16(x, parts):
    out = []
    for _ in range(parts - 1):
        hi = x.astype(BF16)
        out.append(hi)
        x = x - hi.astype(F32)
    out.append(x.astype(BF16))
    return out


def _apply01(mat, x, parts, left=True):
    acc = None
    for p in _split_bf16(x, parts):
        t = jnp.dot(mat, p, preferred_element_type=F32) if left else jnp.dot(p, mat, preferred_element_type=F32)
        acc = t if acc is None else acc + t
    return acc


def _matmul_tiles(m, n, k, mode, out_bytes):
    if mode == "tn":
        tk = _row_tile(k, 640)
    else:
        tk = max(d for d in _divisors(k, CHUNK) if d <= 1536)
    nk = k // tk
    best = None
    m_opts = _divisors(m, CHUNK)
    n_opts = _divisors(n, CHUNK)
    for tm in m_opts:
        for tn in n_opts:
            need = 2 * 2 * (tm * tk + tk * tn) + 2 * tm * tn * out_bytes
            if nk > 1 or mode == "tn":
                need += tm * tn * 4
            need += tm * tn * 4
            if need > VMEM_TILE_BUDGET:
                continue
            key = (tm * tn, tn)
            if best is None or key > best[0]:
                best = (key, tm, tn)
    return best[1], best[2], tk


def _matmul(a, b, mode, out_dtype, name):
    if mode == "nn":
        (m, k), (k2, n) = a.shape, b.shape
    elif mode == "nt":
        (m, k), (n, k2) = a.shape, b.shape
    else:
        (k, m), (k2, n) = a.shape, b.shape
    assert k == k2 and a.dtype == BF16 and b.dtype == BF16
    tm, tn, tk = _matmul_tiles(m, n, k, mode, jnp.dtype(out_dtype).itemsize)
    nk = k // tk
    if mode == "nn":
        a_spec = pl.BlockSpec((tm, tk), lambda i, j, r: (i, r))
        b_spec = pl.BlockSpec((tk, tn), lambda i, j, r: (r, j))
        dims = (((1,), (0,)), ((), ()))
    elif mode == "nt":
        a_spec = pl.BlockSpec((tm, tk), lambda i, j, r: (i, r))
        b_spec = pl.BlockSpec((tn, tk), lambda i, j, r: (j, r))
        dims = (((1,), (1,)), ((), ()))
    else:
        a_spec = pl.BlockSpec((tk, tm), lambda i, j, r: (r, i))
        b_spec = pl.BlockSpec((tk, tn), lambda i, j, r: (r, j))
        dims = (((0,), (0,)), ((), ()))

    def body(a_ref, b_ref, o_ref, *acc):
        part = lax.dot_general(a_ref[...], b_ref[...], dims, preferred_element_type=F32)
        if nk == 1:
            o_ref[...] = part.astype(o_ref.dtype)
        else:
            r = pl.program_id(2)

            @pl.when(r == 0)
            def _():
                acc[0][...] = part

            @pl.when(r > 0)
            def _():
                acc[0][...] += part

            @pl.when(r == nk - 1)
            def _():
                o_ref[...] = acc[0][...].astype(o_ref.dtype)

    return pl.pallas_call(
        body, name=name,
        out_shape=jax.ShapeDtypeStruct((m, n), out_dtype),
        grid=(m // tm, n // tn, nk),
        in_specs=[a_spec, b_spec],
        out_specs=pl.BlockSpec((tm, tn), lambda i, j, r: (i, j)),
        scratch_shapes=[pltpu.VMEM((tm, tn), F32)] if nk > 1 else [],
        compiler_params=_params(("parallel", "parallel", "arbitrary")),
    )(a, b)


def _rms(x, g):
    rstd = lax.rsqrt(jnp.mean(x * x, axis=-1, keepdims=True) + RMS_EPS)
    return x * rstd * g


def _norm_fwd(x, g):
    rows, d = x.shape
    tm = _row_tile(rows, 640)

    def body(x_ref, g_ref, h_ref):
        h_ref[...] = _rms(x_ref[...], g_ref[...]).astype(BF16)

    return pl.pallas_call(
        body, name="norm_fwd",
        out_shape=jax.ShapeDtypeStruct((rows, d), BF16),
        grid=(rows // tm,),
        in_specs=[pl.BlockSpec((tm, d), lambda i: (i, 0)), pl.BlockSpec((1, d), lambda i: (0, 0))],
        out_specs=pl.BlockSpec((tm, d), lambda i: (i, 0)),
        compiler_params=_params(("parallel",)),
    )(x, g.reshape(1, d))


def _resid_norm_fwd(h_res, y, g_post, g_next):
    rows, d = h_res.shape
    tm = _row_tile(rows, 640)

    def body(r_ref, y_ref, gp_ref, gn_ref, hn_ref, hx_ref):
        h_new = r_ref[...] + _rms(y_ref[...], gp_ref[...])
        hn_ref[...] = h_new
        hx_ref[...] = _rms(h_new, gn_ref[...]).astype(BF16)

    row = pl.BlockSpec((tm, d), lambda i: (i, 0))
    vec = pl.BlockSpec((1, d), lambda i: (0, 0))
    return pl.pallas_call(
        body, name="resid_norm_fwd",
        out_shape=(jax.ShapeDtypeStruct((rows, d), F32), jax.ShapeDtypeStruct((rows, d), BF16)),
        grid=(rows // tm,),
        in_specs=[row, row, vec, vec],
        out_specs=(row, row),
        compiler_params=_params(("parallel",)),
    )(h_res, y, g_post.reshape(1, d), g_next.reshape(1, d))


def _norm_bwd(x, g, dy, resid, out_dtype):
    rows, d = x.shape
    tm = _row_tile(rows, 640)
    has_resid = resid is not None

    def body(*refs):
        if has_resid:
            x_ref, g_ref, dy_ref, r_ref, dx_ref, dg_ref = refs
        else:
            x_ref, g_ref, dy_ref, dx_ref, dg_ref = refs
        xv = x_ref[...]
        dyv = dy_ref[...].astype(F32)
        rstd = lax.rsqrt(jnp.mean(xv * xv, axis=-1, keepdims=True) + RMS_EPS)
        xhat = xv * rstd
        gdy = dyv * g_ref[...]
        dx = rstd * (gdy - xhat * jnp.mean(gdy * xhat, axis=-1, keepdims=True))
        if has_resid:
            dx = dx + r_ref[...]
        dx_ref[...] = dx.astype(dx_ref.dtype)

        @pl.when(pl.program_id(0) == 0)
        def _():
            dg_ref[...] = jnp.zeros_like(dg_ref)

        dg_ref[...] += _fold8(dyv * xhat)

    row = pl.BlockSpec((tm, d), lambda i: (i, 0))
    vec = pl.BlockSpec((1, d), lambda i: (0, 0))
    args = [x, g.reshape(1, d), dy] + ([resid] if has_resid else [])
    dx, dg = pl.pallas_call(
        body, name="norm_bwd_resid" if has_resid else "norm_bwd",
        out_shape=(jax.ShapeDtypeStruct((rows, d), out_dtype), jax.ShapeDtypeStruct((8, d), F32)),
        grid=(rows // tm,),
        in_specs=[row, vec, row] + ([row] if has_resid else []),
        out_specs=(row, pl.BlockSpec((8, d), lambda i: (0, 0))),
        compiler_params=_params(("arbitrary",)),
    )(*args)
    return dx, dg.sum(axis=0)


def _swiglu_fwd(ab):
    rows, two_f = ab.shape
    f = two_f // 2
    tm = _row_tile(rows, 640)

    def body(a_ref, b_ref, o_ref):
        a = a_ref[...]
        o_ref[...] = (a * jax.nn.sigmoid(a) * b_ref[...]).astype(BF16)

    return pl.pallas_call(
        body, name="swiglu_fwd",
        out_shape=jax.ShapeDtypeStruct((rows, f), BF16),
        grid=(rows // tm,),
        in_specs=[pl.BlockSpec((tm, f), lambda i: (i, 0)), pl.BlockSpec((tm, f), lambda i: (i, 1))],
        out_specs=pl.BlockSpec((tm, f), lambda i: (i, 0)),
        compiler_params=_params(("parallel",)),
    )(ab, ab)


def _swiglu_bwd(ab, dff):
    rows, two_f = ab.shape
    f = two_f // 2
    tm = _row_tile(rows, 256)

    def body(a_ref, b_ref, d_ref, da_ref, db_ref):
        a = a_ref[...]
        d = d_ref[...]
        s = jax.nn.sigmoid(a)
        da_ref[...] = (d * b_ref[...] * (s * (1.0 + a * (1.0 - s)))).astype(BF16)
        db_ref[...] = (d * (a * s)).astype(BF16)

    lo = pl.BlockSpec((tm, f), lambda i: (i, 0))
    hi = pl.BlockSpec((tm, f), lambda i: (i, 1))
    return pl.pallas_call(
        body, name="swiglu_bwd",
        out_shape=(jax.ShapeDtypeStruct((rows, f), BF16), jax.ShapeDtypeStruct((rows, f), BF16)),
        grid=(rows // tm,),
        in_specs=[lo, hi, lo],
        out_specs=(lo, lo),
        compiler_params=_params(("parallel",)),
    )(ab, ab, dff)


def _tri(lower):
    r = lax.broadcasted_iota(jnp.int32, (CHUNK, CHUNK), 0)
    c = lax.broadcasted_iota(jnp.int32, (CHUNK, CHUNK), 1)
    return jnp.where((r >= c) if lower else (r <= c), 1.0, 0.0).astype(BF16)


def _logf_fwd(z, b):
    h, rows = z.shape
    n = rows // CHUNK

    def body(z_ref, b_ref, f_ref, carry):
        i = pl.program_id(0)

        @pl.when(i == 0)
        def _():
            carry[...] = jnp.zeros_like(carry)

        x = z_ref[...] + b_ref[...]
        lf = jnp.minimum(x, 0.0) - jnp.log(1.0 + jnp.exp(-jnp.abs(x)))
        col = i * CHUNK + lax.broadcasted_iota(jnp.int32, (1, CHUNK), 1)
        lf = jnp.where(col >= PAD_ROWS, lf, 0.0)
        run = _apply01(_tri(False), lf, 3, left=False) + carry[...]
        f_ref[...] = run
        carry[...] = jnp.broadcast_to(run[:, CHUNK - 1:CHUNK], carry.shape)

    blk = pl.BlockSpec((h, CHUNK), lambda i: (0, i))
    return pl.pallas_call(
        body, name="logf_fwd",
        out_shape=jax.ShapeDtypeStruct((h, rows), F32),
        grid=(n,),
        in_specs=[blk, pl.BlockSpec((h, CHUNK), lambda i: (0, 0))],
        out_specs=blk,
        scratch_shapes=[pltpu.VMEM((h, CHUNK), F32)],
        compiler_params=_params(("arbitrary",)),
    )(z, b)


def _logf_bwd(df, z, b):
    h, rows = z.shape
    n = rows // CHUNK

    def body(df_ref, z_ref, b_ref, dz_ref, db_ref, carry):
        i = pl.program_id(0)

        @pl.when(i == 0)
        def _():
            carry[...] = jnp.zeros_like(carry)
            db_ref[...] = jnp.zeros_like(db_ref)

        run = _apply01(_tri(True), df_ref[...], 3, left=False) + carry[...]
        carry[...] = jnp.broadcast_to(run[:, 0:1], carry.shape)
        x = z_ref[...] + b_ref[...]
        col = (n - 1 - i) * CHUNK + lax.broadcasted_iota(jnp.int32, (1, CHUNK), 1)
        dz = jnp.where(col >= PAD_ROWS, run * (1.0 - jax.nn.sigmoid(x)), 0.0)
        dz_ref[...] = dz
        db_ref[...] += dz

    rev = pl.BlockSpec((h, CHUNK), lambda i: (0, n - 1 - i))
    fix = pl.BlockSpec((h, CHUNK), lambda i: (0, 0))
    dz, db = pl.pallas_call(
        body, name="logf_bwd",
        out_shape=(jax.ShapeDtypeStruct((h, rows), F32), jax.ShapeDtypeStruct((h, CHUNK), F32)),
        grid=(n,),
        in_specs=[rev, rev, fix],
        out_specs=(rev, fix),
        scratch_shapes=[pltpu.VMEM((h, CHUNK), F32)],
        compiler_params=_params(("arbitrary",)),
    )(df, z, b)
    return dz, db.sum(axis=1)


def _lane_lo():
    return lax.broadcasted_iota(jnp.int32, (1, CHUNK), 1) < HEAD_DIM


def _attn_block(rows):
    return _row_tile(rows, min(640, rows // 2))


def _masked_logits(s, i, j, blk):
    row = i * blk + lax.broadcasted_iota(jnp.int32, (blk, 1), 0)
    col = j * blk + lax.broadcasted_iota(jnp.int32, (1, blk), 1)
    return jnp.where(col <= row, jnp.where(col >= PAD_ROWS, s, NEG_INF), NEG_INF)


def _attn_fwd(qkv, f_pairs, d):
    rows = qkv.shape[0]
    hp = d // CHUNK
    blk = _attn_block(rows)
    nb = rows // blk
    scale = 1.0 / math.sqrt(HEAD_DIM)
    nt = (((1,), (1,)), ((), ()))

    def body(q_ref, k_ref, v_ref, f_ref, o_ref, lse_ref, m_s, l_s, acc_s):
        i = pl.program_id(1)
        j = pl.program_id(2)
        lo = _lane_lo()

        @pl.when(j == 0)
        def _():
            m_s[...] = jnp.full(m_s.shape, NEG_INF, F32)
            l_s[...] = jnp.zeros_like(l_s)
            acc_s[...] = jnp.zeros_like(acc_s)

        def step(masked):
            q = (q_ref[...].astype(F32) * scale).astype(BF16)
            k = k_ref[...]
            v = v_ref[...]
            zero = jnp.zeros_like(k)
            alphas, pvs = [], []
            for hh in range(2):
                kh = jnp.where(lo, k, zero) if hh == 0 else jnp.where(lo, zero, k)
                s = lax.dot_general(q, kh, nt, preferred_element_type=F32) - f_ref[hh:hh + 1, :]
                if masked:
                    s = _masked_logits(s, i, j, blk)
                m_prev = m_s[hh]
                m_new = jnp.maximum(m_prev, s.max(axis=-1, keepdims=True))
                alpha = jnp.exp(m_prev - m_new)
                p = jnp.exp(s - m_new)
                l_s[hh] = alpha * l_s[hh] + p.sum(axis=-1, keepdims=True)
                m_s[hh] = m_new
                alphas.append(alpha)
                pvs.append(jnp.dot(p.astype(BF16), v, preferred_element_type=F32))
            acc_s[...] = acc_s[...] * jnp.where(lo, alphas[0], alphas[1]) + jnp.where(lo, pvs[0], pvs[1])

        edge = (j == i) | (j == 0)

        @pl.when(edge & (j <= i))
        def _():
            step(True)

        @pl.when(jnp.logical_not(edge) & (j <= i))
        def _():
            step(False)

        @pl.when(j == i)
        def _():
            row = i * blk + lax.broadcasted_iota(jnp.int32, (blk, 1), 0)
            inv = jnp.where(lo, 1.0 / l_s[0], 1.0 / l_s[1])
            o_ref[...] = jnp.where(row >= PAD_ROWS, acc_s[...] * inv, 0.0)
            lse_ref[...] = jnp.where(lo, m_s[0] + jnp.log(l_s[0]), m_s[1] + jnp.log(l_s[1]))

    q_spec = pl.BlockSpec((blk, CHUNK), lambda h, i, j: (i, h))
    k_spec = pl.BlockSpec((blk, CHUNK), lambda h, i, j: (jnp.minimum(j, i), hp + h))
    v_spec = pl.BlockSpec((blk, CHUNK), lambda h, i, j: (jnp.minimum(j, i), 2 * hp + h))
    f_spec = pl.BlockSpec((None, 2, blk), lambda h, i, j: (h, 0, jnp.minimum(j, i)))
    o_spec = pl.BlockSpec((blk, CHUNK), lambda h, i, j: (i, h))
    return pl.pallas_call(
        body, name="attn_fwd",
        out_shape=(jax.ShapeDtypeStruct((rows, d), F32), jax.ShapeDtypeStruct((rows, d), F32)),
        grid=(hp, nb, nb),
        in_specs=[q_spec, k_spec, v_spec, f_spec],
        out_specs=(o_spec, o_spec),
        scratch_shapes=[pltpu.VMEM((2, blk, 1), F32), pltpu.VMEM((2, blk, 1), F32), pltpu.VMEM((blk, CHUNK), F32)],
        compiler_params=_params(("parallel", "parallel", "arbitrary")),
    )(qkv, qkv, qkv, f_pairs)


def _attn_bwd(qkv, do, lse, delta, f_pairs, d):
    rows = qkv.shape[0]
    hp = d // CHUNK
    blk = _attn_block(rows)
    nb = rows // blk
    scale = 1.0 / math.sqrt(HEAD_DIM)
    nt = (((1,), (1,)), ((), ()))
    tn = (((0,), (0,)), ((), ()))

    def body(q_ref, k_ref, v_ref, do_ref, lse_ref, dl_ref, f_ref, dq_ref, dk_ref, dv_ref, df_ref, rs_ref,
             dq_s, dk_s, dv_s, df_s, rs_s):
        j = pl.program_id(1)
        i = pl.program_id(2)
        lo = _lane_lo()

        @pl.when((j == 0) & (i == 0))
        def _():
            dq_s[...] = jnp.zeros_like(dq_s)
            rs_s[...] = jnp.zeros_like(rs_s)

        @pl.when(i == j)
        def _():
            dk_s[...] = jnp.zeros_like(dk_s)
            dv_s[...] = jnp.zeros_like(dv_s)
            df_s[...] = jnp.zeros_like(df_s)

        def step(masked):
            q = (q_ref[...].astype(F32) * scale).astype(BF16)
            k = k_ref[...]
            v = v_ref[...]
            dov = do_ref[...]
            zero = jnp.zeros_like(k)
            dq_acc = dk_acc = dv_acc = None
            row_sums = []
            for hh in range(2):
                sel = (lambda t: jnp.where(lo, t, zero)) if hh == 0 else (lambda t: jnp.where(lo, zero, t))
                kh, vh, qh, doh = sel(k), sel(v), sel(q), sel(dov)
                off = hh * HEAD_DIM
                s = lax.dot_general(q, kh, nt, preferred_element_type=F32) - f_ref[hh:hh + 1, :]
                if masked:
                    s = _masked_logits(s, i, j, blk)
                p = jnp.exp(s - lse_ref[:, off:off + 1])
                dp = lax.dot_general(dov, vh, nt, preferred_element_type=F32)
                ds = p * (dp - dl_ref[:, off:off + 1])
                df_s[hh:hh + 1, :] += ds.sum(axis=0, keepdims=True)
                row_sums.append(ds.sum(axis=1, keepdims=True))
                pb = p.astype(BF16)
                dsb = ds.astype(BF16)
                t_dv = lax.dot_general(pb, doh, tn, preferred_element_type=F32)
                t_dk = lax.dot_general(dsb, qh, tn, preferred_element_type=F32)
                t_dq = jnp.dot(dsb, kh, preferred_element_type=F32)
                dv_acc = t_dv if dv_acc is None else dv_acc + t_dv
                dk_acc = t_dk if dk_acc is None else dk_acc + t_dk
                dq_acc = t_dq if dq_acc is None else dq_acc + t_dq
            dv_s[...] += dv_acc
            dk_s[...] += dk_acc
            r0 = pl.multiple_of(i * blk, blk)
            dq_s[pl.ds(r0, blk), :] += dq_acc
            rs_s[pl.ds(r0, blk), :] += jnp.where(lo, row_sums[0], row_sums[1])

        edge = (j == i) | (j == 0)

        @pl.when(edge & (i >= j))
        def _():
            step(True)

        @pl.when(jnp.logical_not(edge) & (i >= j))
        def _():
            step(False)

        @pl.when(i == nb - 1)
        def _():
            dk_ref[...] = dk_s[...].astype(BF16)
            dv_ref[...] = dv_s[...].astype(BF16)
            df_ref[...] = -df_s[...]

        @pl.when((i == nb - 1) & (j == nb - 1))
        def _():
            dq_ref[...] = (dq_s[...] * scale).astype(BF16)
            rs_ref[...] = rs_s[...]

    qi = lambda h, j, i: (jnp.maximum(i, j), h)
    q_spec = pl.BlockSpec((blk, CHUNK), qi)
    k_spec = pl.BlockSpec((blk, CHUNK), lambda h, j, i: (j, hp + h))
    v_spec = pl.BlockSpec((blk, CHUNK), lambda h, j, i: (j, 2 * hp + h))
    f_spec = pl.BlockSpec((None, 2, blk), lambda h, j, i: (h, 0, j))
    kv_out = pl.BlockSpec((blk, CHUNK), lambda h, j, i: (j, h))
    dq_out = pl.BlockSpec((rows, CHUNK), lambda h, j, i: (0, h))
    return pl.pallas_call(
        body, name="attn_bwd",
        out_shape=(jax.ShapeDtypeStruct((rows, d), BF16), jax.ShapeDtypeStruct((rows, d), BF16),
                   jax.ShapeDtypeStruct((rows, d), BF16), jax.ShapeDtypeStruct((hp, 2, rows), F32),
                   jax.ShapeDtypeStruct((rows, d), F32)),
        grid=(hp, nb, nb),
        in_specs=[q_spec, k_spec, v_spec, q_spec, q_spec, q_spec, f_spec],
        out_specs=(dq_out, kv_out, kv_out, f_spec, dq_out),
        scratch_shapes=[pltpu.VMEM((rows, CHUNK), F32), pltpu.VMEM((blk, CHUNK), F32),
                        pltpu.VMEM((blk, CHUNK), F32), pltpu.VMEM((2, blk), F32), pltpu.VMEM((rows, CHUNK), F32)],
        compiler_params=_params(("parallel", "arbitrary", "arbitrary")),
    )(qkv, qkv, qkv, do, lse, delta, f_pairs)


def _band(w, transposed, other):
    r = lax.broadcasted_iota(jnp.int32, (CHUNK, CHUNK), 0)
    c = lax.broadcasted_iota(jnp.int32, (CHUNK, CHUNK), 1)
    dist = (c - r) if transposed else (r - c)
    if other:
        dist = dist + CHUNK
    return jnp.where(dist >= 0, jnp.where(dist < w, 1.0, 0.0), 0.0).astype(BF16)


def _inv_count(chunk_index, w):
    row = chunk_index * CHUNK + lax.broadcasted_iota(jnp.int32, (CHUNK, 1), 0)
    cnt = jnp.clip(row - PAD_ROWS + 1, 1, w).astype(F32)
    return 1.0 / cnt


def _pool_diff(u_cur, u_prev, i, w):
    ws = _apply01(_band(w, False, False), u_cur, 3)
    ws = ws + jnp.where(i > 0, _apply01(_band(w, False, True), u_prev, 3), 0.0)
    return ws * _inv_count(i, w) - u_cur


def _pool_merge_fwd(rest, o, w_pool, scale, d):
    rows = rest.shape[0]
    n = rows // CHUNK
    cg = d // len(POOL_WINDOWS)

    def body(up_ref, uc_ref, gp_ref, ga_ref, o_ref, wp_ref, sc_ref, mg_ref, yp_ref):
        i = pl.program_id(0)
        for g, w in enumerate(POOL_WINDOWS):
            sl = slice(g * cg, (g + 1) * cg)
            diff = _pool_diff(uc_ref[:, sl], up_ref[:, sl], i, w)
            ypre = jnp.dot(diff.astype(BF16), wp_ref[g], preferred_element_type=F32)
            yp_ref[:, sl] = ypre
            merged = (jax.nn.sigmoid(gp_ref[:, sl]) * (ypre * sc_ref[:, sl])
                      + jax.nn.sigmoid(ga_ref[:, sl]) * o_ref[:, sl])
            mg_ref[:, sl] = merged.astype(BF16)

    col = lambda c: pl.BlockSpec((CHUNK, d), lambda i: (i, c))
    return pl.pallas_call(
        body, name="pool_merge_fwd",
        out_shape=(jax.ShapeDtypeStruct((rows, d), BF16), jax.ShapeDtypeStruct((rows, d), F32)),
        grid=(n,),
        in_specs=[pl.BlockSpec((CHUNK, d), lambda i: (jnp.maximum(i - 1, 0), 0)), col(0), col(1), col(2), col(0),
                  pl.BlockSpec((len(POOL_WINDOWS), cg, cg), lambda i: (0, 0, 0)),
                  pl.BlockSpec((1, d), lambda i: (0, 0))],
        out_specs=(col(0), col(0)),
        compiler_params=_params(("parallel",)),
    )(rest, rest, rest, rest, o, w_pool, scale.reshape(1, d))


def _gate_bwd(dm, rest, o, ypre, scale, d):
    rows = rest.shape[0]
    n = rows // CHUNK

    def body(dm_ref, gp_ref, ga_ref, o_ref, yp_ref, sc_ref, dgp_ref, dga_ref, do_ref, dl_ref, dy_ref, ds_ref):
        @pl.when(pl.program_id(0) == 0)
        def _():
            ds_ref[...] = jnp.zeros_like(ds_ref)

        dmv = dm_ref[...]
        sp = jax.nn.sigmoid(gp_ref[...])
        sa = jax.nn.sigmoid(ga_ref[...])
        ov = o_ref[...]
        ypre_v = yp_ref[...]
        sc = sc_ref[...]
        dgp_ref[...] = (dmv * (ypre_v * sc) * (sp * (1.0 - sp))).astype(BF16)
        dga_ref[...] = (dmv * ov * (sa * (1.0 - sa))).astype(BF16)
        t = dmv * sp
        dy_ref[...] = (t * sc).astype(BF16)
        ds_ref[...] += _fold8(t * ypre_v)
        dob = (dmv * sa).astype(BF16)
        do_ref[...] = dob
        prod = dob.astype(F32) * ov
        lo = _lane_lo()
        for pr in range(d // CHUNK):
            sl = slice(pr * CHUNK, (pr + 1) * CHUNK)
            tp = prod[:, sl]
            s_lo = jnp.where(lo, tp, 0.0).sum(axis=-1, keepdims=True)
            s_hi = jnp.where(lo, 0.0, tp).sum(axis=-1, keepdims=True)
            dl_ref[:, sl] = jnp.where(lo, s_lo, s_hi)

    col = lambda c: pl.BlockSpec((CHUNK, d), lambda i: (i, c))
    row_bf = jax.ShapeDtypeStruct((rows, d), BF16)
    outs = pl.pallas_call(
        body, name="gate_bwd",
        out_shape=(row_bf, row_bf, row_bf, jax.ShapeDtypeStruct((rows, d), F32), row_bf,
                   jax.ShapeDtypeStruct((8, d), F32)),
        grid=(n,),
        in_specs=[col(0), col(1), col(2), col(0), col(0), pl.BlockSpec((1, d), lambda i: (0, 0))],
        out_specs=(col(0), col(0), col(0), col(0), col(0), pl.BlockSpec((8, d), lambda i: (0, 0))),
        compiler_params=_params(("arbitrary",)),
    )(dm, rest, rest, o, ypre, scale.reshape(1, d))
    return outs[:5] + (outs[5].sum(axis=0),)


def _pool_bwd(dypre, rest, w_pool, d):
    rows = rest.shape[0]
    n = rows // CHUNK
    ng = len(POOL_WINDOWS)
    cg = d // ng
    nt = (((1,), (1,)), ((), ()))
    tn = (((0,), (0,)), ((), ()))

    def body(dc_ref, dn_ref, up_ref, uc_ref, wp_ref, du_ref, dw_ref):
        i = pl.program_id(0)

        @pl.when(i == 0)
        def _():
            dw_ref[...] = jnp.zeros_like(dw_ref)

        row = i * CHUNK + lax.broadcasted_iota(jnp.int32, (CHUNK, 1), 0)
        for g, w in enumerate(POOL_WINDOWS):
            sl = slice(g * cg, (g + 1) * cg)
            diff = _pool_diff(uc_ref[:, sl], up_ref[:, sl], i, w)
            dyc = dc_ref[:, sl]
            dw_ref[g] += lax.dot_general(diff.astype(BF16), dyc, tn, preferred_element_type=F32)
            wg = wp_ref[g]
            dd_cur = lax.dot_general(dyc, wg, nt, preferred_element_type=F32)
            dd_next = lax.dot_general(dn_ref[:, sl], wg, nt, preferred_element_type=F32)
            du = _apply01(_band(w, True, False), dd_cur * _inv_count(i, w), 2)
            du = du + jnp.where(i < n - 1, _apply01(_band(w, True, True), dd_next * _inv_count(i + 1, w), 2), 0.0)
            du = du - dd_cur
            du_ref[:, sl] = jnp.where(row >= PAD_ROWS, du, 0.0).astype(BF16)

    cur = pl.BlockSpec((CHUNK, d), lambda i: (i, 0))
    return pl.pallas_call(
        body, name="pool_bwd",
        out_shape=(jax.ShapeDtypeStruct((rows, d), BF16), jax.ShapeDtypeStruct((ng, cg, cg), F32)),
        grid=(n,),
        in_specs=[cur, pl.BlockSpec((CHUNK, d), lambda i: (jnp.minimum(i + 1, n - 1), 0)),
                  pl.BlockSpec((CHUNK, d), lambda i: (jnp.maximum(i - 1, 0), 0)), cur,
                  pl.BlockSpec((ng, cg, cg), lambda i: (0, 0, 0))],
        out_specs=(cur, pl.BlockSpec((ng, cg, cg), lambda i: (0, 0, 0))),
        compiler_params=_params(("arbitrary",)),
    )(dypre, dypre, rest, rest, w_pool)


def _loss_grad(h_res, target):
    rows, d = h_res.shape
    n = rows // CHUNK

    def body(h_ref, t_ref, dh_ref, acc_ref):
        i = pl.program_id(0)

        @pl.when(i == 0)
        def _():
            acc_ref[...] = jnp.zeros_like(acc_ref)
            dh_ref[...] = jnp.zeros_like(dh_ref)

        @pl.when(i > 0)
        def _():
            err = h_ref[...] - t_ref[...]
            dh_ref[...] = err * (1.0 / d)
            e2 = _fold8(err * err)
            part = e2[:, 0:CHUNK]
            for c in range(1, d // CHUNK):
                part = part + e2[:, c * CHUNK:(c + 1) * CHUNK]
            acc_ref[...] += part

    dh, acc = pl.pallas_call(
        body, name="loss_grad",
        out_shape=(jax.ShapeDtypeStruct((rows, d), F32), jax.ShapeDtypeStruct((8, CHUNK), F32)),
        grid=(n,),
        in_specs=[pl.BlockSpec((CHUNK, d), lambda i: (i, 0)),
                  pl.BlockSpec((CHUNK, d), lambda i: (jnp.maximum(i - 1, 0), 0))],
        out_specs=(pl.BlockSpec((CHUNK, d), lambda i: (i, 0)), pl.BlockSpec((8, CHUNK), lambda i: (0, 0))),
        compiler_params=_params(("arbitrary",)),
    )(h_res, target)
    return dh, (0.5 / d) * acc.sum()


def _adamw(slots, w, m, v, name):
    rows = w.shape[0]
    tr = max(t for t in (8, 16, 32, 64, 128, 256, 512, 1024) if rows % t == 0)
    c1 = 1.0 - ADAM_B1 ** ADAM_STEP
    c2 = 1.0 - ADAM_B2 ** ADAM_STEP

    def body(s_ref, w_ref, m_ref, v_ref, g_ref, d_ref, mo_ref, vo_ref):
        g = s_ref[0]
        for k in range(1, N_DEV):
            g = g + s_ref[k]
        m_new = ADAM_B1 * m_ref[...] + (1.0 - ADAM_B1) * g
        v_new = ADAM_B2 * v_ref[...] + (1.0 - ADAM_B2) * (g * g)
        m_hat = m_new / c1
        v_hat = v_new / c2
        g_ref[...] = g
        d_ref[...] = -ADAM_LR * (m_hat / (jnp.sqrt(v_hat) + ADAM_EPS) + ADAM_WD * w_ref[...])
        mo_ref[...] = m_new
        vo_ref[...] = v_new

    flat = pl.BlockSpec((tr, CHUNK), lambda i: (i, 0))
    sds = jax.ShapeDtypeStruct((rows, CHUNK), F32)
    return pl.pallas_call(
        body, name=name,
        out_shape=(sds, sds, sds, sds),
        grid=(rows // tr,),
        in_specs=[pl.BlockSpec((N_DEV, tr, CHUNK), lambda i: (0, i, 0)), flat, flat, flat],
        out_specs=(flat, flat, flat, flat),
        compiler_params=_params(("parallel",)),
    )(slots, w, m, v)


def _exchange(src, gather, name):
    shape = ((N_DEV,) + src.shape) if gather else src.shape

    def body(src_ref, out_ref, send_sems, recv_sems, local_sem):
        x, y, c = lax.axis_index("x"), lax.axis_index("y"), lax.axis_index("c")
        me = 4 * x + 2 * y + c

        def payload(slot):
            return src_ref if gather else src_ref.at[slot]

        own = pltpu.make_async_copy(payload(me), out_ref.at[me], local_sem)
        own.start()
        sends, recvs = [], []
        for k in range(1, N_DEV):
            px = 1 - x if k & 4 else x
            py = 1 - y if k & 2 else y
            pc = 1 - c if k & 1 else c
            peer = 4 * px + 2 * py + pc
            sems = dict(send_sem=send_sems.at[k - 1], recv_sem=recv_sems.at[k - 1],
                        device_id=(px, py, pc), device_id_type=MESH_ID)
            sends.append(pltpu.make_async_remote_copy(src_ref=payload(peer), dst_ref=out_ref.at[me], **sems))
            recvs.append(pltpu.make_async_remote_copy(src_ref=payload(peer), dst_ref=out_ref.at[peer], **sems))
        for cp in sends:
            cp.start()
        for cp in recvs:
            cp.wait_recv()
        for cp in sends:
            cp.wait_send()
        own.wait()

    return pl.pallas_call(
        body, name=name,
        out_shape=jax.ShapeDtypeStruct(shape, src.dtype),
        in_specs=[pl.BlockSpec(memory_space=pl.ANY)],
        out_specs=pl.BlockSpec(memory_space=pl.ANY),
        scratch_shapes=[pltpu.SemaphoreType.DMA((N_DEV - 1,)), pltpu.SemaphoreType.DMA((N_DEV - 1,)),
                        pltpu.SemaphoreType.DMA],
    )(src)


def _to_flat(parts, row_mult, dtype):
    v = jnp.concatenate([p.astype(dtype).reshape(-1) for p in parts])
    quantum = row_mult * CHUNK
    padded = -(-v.shape[0] // quantum) * quantum
    return jnp.pad(v, (0, padded - v.shape[0])).reshape(-1, CHUNK)


def _slots_to_flat(parts, row_mult):
    v = jnp.concatenate([p.reshape(N_DEV, -1) for p in parts], axis=1)
    quantum = row_mult * CHUNK
    padded = -(-v.shape[1] // quantum) * quantum
    return jnp.pad(v, ((0, 0), (0, padded - v.shape[1]))).reshape(N_DEV, -1, CHUNK)


def _from_flat(flat, shapes, lead=()):
    v = flat.reshape(lead + (-1,))
    out, off = [], 0
    for s in shapes:
        size = math.prod(s)
        out.append(v[..., off:off + size].reshape(lead + tuple(s)))
        off += size
    return out


def _shard_axis(name):
    return {"w_in": 2, "w_pool": 2, "w_out": 1, "w_gate": 2, "w_up": 2, "w_down": 1}[name]


def _unshard(g, axis):
    moved = jnp.moveaxis(g, 0, axis)
    s = moved.shape
    return moved.reshape(s[:axis] + (s[axis] * s[axis + 1],) + s[axis + 2:])


def _reshard(full, axis):
    s = full.shape
    split = full.reshape(s[:axis] + (N_DEV, s[axis] // N_DEV) + s[axis + 1:])
    return jnp.moveaxis(split, axis, 0)


BIG = ("w_in", "w_pool", "w_out", "w_gate", "w_up", "w_down")
SMALL = ("norm_mix_pre", "norm_mix_post", "norm_ffn_pre", "norm_ffn_post", "pool_scale", "b_forget")


def kernel(x, meta_tokens, norm_mix_pre, norm_mix_post, norm_ffn_pre, norm_ffn_post, w_in, b_forget, w_pool, pool_scale, w_out, w_ffn_gate, w_ffn_up, w_ffn_down, loss_target, m_meta_tokens, m_norm_mix_pre, m_norm_mix_post, m_norm_ffn_pre, m_norm_ffn_post, m_w_in, m_b_forget, m_w_pool, m_pool_scale, m_w_out, m_w_ffn_gate, m_w_ffn_up, m_w_ffn_down, v_meta_tokens, v_norm_mix_pre, v_norm_mix_post, v_norm_ffn_pre, v_norm_ffn_post, v_w_in, v_b_forget, v_w_pool, v_pool_scale, v_w_out, v_w_ffn_gate, v_w_ffn_up, v_w_ffn_down):
    x2 = x[0]
    target = loss_target[0]
    seq, d = x2.shape
    depth = w_in.shape[0]
    heads = d // HEAD_DIM
    ff = w_ffn_gate.shape[2] * N_DEV
    rows = PAD_ROWS + META_TOKENS + seq
    assert seq % CHUNK == 0 and d % (2 * CHUNK) == 0 and heads <= FORGET_PAD
    me = 4 * lax.axis_index("x") + 2 * lax.axis_index("y") + lax.axis_index("c")

    big = dict(w_in=w_in, w_pool=w_pool, w_out=w_out, w_gate=w_ffn_gate, w_up=w_ffn_up, w_down=w_ffn_down)
    big_m = dict(w_in=m_w_in, w_pool=m_w_pool, w_out=m_w_out, w_gate=m_w_ffn_gate, w_up=m_w_ffn_up, w_down=m_w_ffn_down)
    big_v = dict(w_in=v_w_in, w_pool=v_w_pool, w_out=v_w_out, w_gate=v_w_ffn_gate, w_up=v_w_ffn_up, w_down=v_w_ffn_down)
    small = dict(norm_mix_pre=norm_mix_pre, norm_mix_post=norm_mix_post, norm_ffn_pre=norm_ffn_pre,
                 norm_ffn_post=norm_ffn_post, pool_scale=pool_scale, b_forget=b_forget)
    small_m = dict(norm_mix_pre=m_norm_mix_pre, norm_mix_post=m_norm_mix_post, norm_ffn_pre=m_norm_ffn_pre,
                   norm_ffn_post=m_norm_ffn_post, pool_scale=m_pool_scale, b_forget=m_b_forget)
    small_v = dict(norm_mix_pre=v_norm_mix_pre, norm_mix_post=v_norm_mix_post, norm_ffn_pre=v_norm_ffn_pre,
                   norm_ffn_post=v_norm_ffn_post, pool_scale=v_pool_scale, b_forget=v_b_forget)

    big_shapes = [big[n].shape for n in BIG]
    gathered = _exchange(_to_flat([big[n] for n in BIG], 16, BF16), True, "gather_weights")
    full = {n: _unshard(g, _shard_axis(n))
            for n, g in zip(BIG, _from_flat(gathered, big_shapes, lead=(N_DEV,)))}
    meta_full = _unshard(_exchange(meta_tokens, True, "gather_meta"), 1)

    win = full["w_in"]
    fcol = 4 * d
    w_qkv = win[:, :, d:4 * d]
    w_rest = jnp.concatenate([win[:, :, :d], win[:, :, fcol + heads:], win[:, :, fcol:fcol + heads],
                              jnp.zeros((depth, d, FORGET_PAD - heads), BF16)], axis=2)
    w_cat = jnp.concatenate([w_qkv, w_rest], axis=2)
    w_gu = jnp.concatenate([full["w_gate"], full["w_up"]], axis=2)
    w_o, w_dn, w_pl = full["w_out"], full["w_down"], full["w_pool"]

    h_res = jnp.concatenate([jnp.zeros((PAD_ROWS, d), F32), meta_full, x2], axis=0)
    h1 = _norm_fwd(h_res, norm_mix_pre[0])
    ones = jnp.ones((d,), F32)
    saved = []
    for l in range(depth):
        qkv = _matmul(h1, w_qkv[l], "nn", BF16, "proj_qkv")
        rest = _matmul(h1, w_rest[l], "nn", F32, "proj_rest")
        z = rest[:, 3 * d:3 * d + heads].T
        bias = jnp.broadcast_to(b_forget[l][:, None], (heads, CHUNK))
        f_run = _logf_fwd(z, bias).reshape(heads // 2, 2, rows)
        o, lse = _attn_fwd(qkv, f_run, d)
        merged, ypre = _pool_merge_fwd(rest, o, w_pl[l], pool_scale[l], d)
        mix = _matmul(merged, w_o[l], "nn", F32, "mix_out")
        h_mid, h2 = _resid_norm_fwd(h_res, mix, norm_mix_post[l], norm_ffn_pre[l])
        ab = _matmul(h2, w_gu[l], "nn", F32, "ffn_in")
        act = _swiglu_fwd(ab)
        ffo = _matmul(act, w_dn[l], "nn", F32, "ffn_out")
        g_next = norm_mix_pre[l + 1] if l + 1 < depth else ones
        h_next, h1_next = _resid_norm_fwd(h_mid, ffo, norm_ffn_post[l], g_next)
        saved.append(dict(h_in=h_res, h1=h1, qkv=qkv, rest=rest, z=z, bias=bias, f_run=f_run, o=o, lse=lse,
                          merged=merged, ypre=ypre, mix=mix, h_mid=h_mid, h2=h2, ab=ab, act=act, ffo=ffo))
        h_res, h1 = h_next, h1_next

    dh, loss_local = _loss_grad(h_res, target)
    loss = lax.psum(loss_local, ("x", "y", "c"))

    grads = {n: [None] * depth for n in BIG + SMALL}
    for l in reversed(range(depth)):
        s = saved[l]
        dffo, grads["norm_ffn_post"][l] = _norm_bwd(s["ffo"], norm_ffn_post[l], dh, None, BF16)
        grads["w_down"][l] = _matmul(s["act"], dffo, "tn", F32, "grad_w_down")
        dact = _matmul(dffo, w_dn[l], "nt", F32, "ffn_out_dx")
        da, db = _swiglu_bwd(s["ab"], dact)
        dab = jnp.concatenate([da, db], axis=1)
        dgu = _matmul(s["h2"], dab, "tn", F32, "grad_w_gu")
        grads["w_gate"][l], grads["w_up"][l] = dgu[:, :ff], dgu[:, ff:]
        dh2 = _matmul(dab, w_gu[l], "nt", F32, "ffn_in_dx")
        dh_mid, grads["norm_ffn_pre"][l] = _norm_bwd(s["h_mid"], norm_ffn_pre[l], dh2, dh, F32)

        dmix, grads["norm_mix_post"][l] = _norm_bwd(s["mix"], norm_mix_post[l], dh_mid, None, BF16)
        grads["w_out"][l] = _matmul(s["merged"], dmix, "tn", F32, "grad_w_out")
        dm = _matmul(dmix, w_o[l], "nt", F32, "mix_out_dx")
        dgp, dga, do, delta, dypre, grads["pool_scale"][l] = _gate_bwd(dm, s["rest"], s["o"], s["ypre"], pool_scale[l], d)
        du, grads["w_pool"][l] = _pool_bwd(dypre, s["rest"], w_pl[l], d)
        dq, dk, dv, df_key, df_query = _attn_bwd(s["qkv"], do, s["lse"], delta, s["f_run"], d)
        df = df_key.reshape(heads, rows) + df_query.reshape(rows, heads, HEAD_DIM)[:, :, 0].T
        dz, grads["b_forget"][l] = _logf_bwd(df, s["z"], s["bias"])
        dzt = jnp.pad(dz.T.astype(BF16), ((0, 0), (0, FORGET_PAD - heads)))
        dproj = jnp.concatenate([dq, dk, dv, du, dgp, dga, dzt], axis=1)
        dwc = _matmul(s["h1"], dproj, "tn", F32, "grad_w_in")
        grads["w_in"][l] = jnp.concatenate([dwc[:, 3 * d:4 * d], dwc[:, :3 * d], dwc[:, 6 * d:6 * d + heads],
                                            dwc[:, 4 * d:6 * d]], axis=1)
        dh1 = _matmul(dproj, w_cat[l], "nt", F32, "proj_dx")
        dh, grads["norm_mix_pre"][l] = _norm_bwd(s["h_in"], norm_mix_pre[l], dh1, dh_mid, F32)

    grad_x = dh[PAD_ROWS + META_TOKENS:][None]
    dmeta = dh[PAD_ROWS:PAD_ROWS + META_TOKENS]

    slot_parts = [_reshard(jnp.stack(grads[n]), _shard_axis(n)) for n in BIG]
    recv = _exchange(_slots_to_flat(slot_parts, 1024), False, "scatter_grads")
    outs = _adamw(recv, _to_flat([big[n] for n in BIG], 1024, F32), _to_flat([big_m[n] for n in BIG], 1024, F32),
                  _to_flat([big_v[n] for n in BIG], 1024, F32), "adamw_big")
    big_out = [dict(zip(BIG, _from_flat(o_, big_shapes))) for o_ in outs]

    small_shapes = [small[n].shape for n in SMALL]
    small_flat = _to_flat([jnp.stack(grads[n]) for n in SMALL], 8, F32)
    small_rows = small_flat.shape[0]
    meta_rows = _to_flat([dmeta], 8, F32)
    got = _exchange(jnp.concatenate([small_flat, meta_rows], axis=0), True, "gather_small_grads")
    dcols = d // N_DEV
    meta_parts = got[:, small_rows:].reshape(N_DEV, -1)[:, :META_TOKENS * d].reshape(N_DEV, META_TOKENS, d)
    meta_mine = lax.dynamic_slice_in_dim(meta_parts, me * dcols, dcols, axis=2)
    slots_small = jnp.concatenate([got[:, :small_rows], _slots_to_flat([meta_mine], 8)], axis=1)

    def pack_small(table, meta):
        return jnp.concatenate([_to_flat([table[n] for n in SMALL], 8, F32), _to_flat([meta], 8, F32)], axis=0)

    outs = _adamw(slots_small, pack_small(small, meta_tokens), pack_small(small_m, m_meta_tokens),
                  pack_small(small_v, v_meta_tokens), "adamw_small")
    small_out, meta_out = [], []
    for o_ in outs:
        small_out.append(dict(zip(SMALL, _from_flat(o_[:small_rows], small_shapes))))
        meta_out.append(o_[small_rows:].reshape(-1)[:META_TOKENS * dcols].reshape(META_TOKENS, dcols))

    def ordered(k):
        so, bo = small_out[k], big_out[k]
        return (meta_out[k], so["norm_mix_pre"], so["norm_mix_post"], so["norm_ffn_pre"], so["norm_ffn_post"],
                bo["w_in"], so["b_forget"], bo["w_pool"], so["pool_scale"], bo["w_out"], bo["w_gate"], bo["w_up"],
                bo["w_down"])

    return (loss, grad_x) + ordered(0) + ordered(1) + ordered(2) + ordered(3)
```

```python
import math

import jax
import jax.numpy as jnp
from jax import lax
from jax.experimental import pallas as pl
from jax.experimental.pallas import tpu as pltpu

F32 = jnp.float32
BF16 = jnp.bfloat16

N_DEV = 8
META_TOKENS = 16
PAD_ROWS = 112
CHUNK = 128
HEAD_DIM = 64
POOL_WINDOWS = (2, 4, 8, 16)
FORGET_PAD = 256
RMS_EPS = 1e-6
NEG_INF = -1e30
ADAM_LR, ADAM_B1, ADAM_B2, ADAM_EPS, ADAM_WD, ADAM_STEP = 0.001, 0.9, 0.999, 1e-08, 0.01, 10

VMEM_LIMIT = 56 * 1024 * 1024
VMEM_TILE_BUDGET = 36 * 1024 * 1024
ADAM_TILE_BYTES = 8 * 1024 * 1024
MESH_ID = pl.DeviceIdType.MESH


def _params(sem, vmem=VMEM_LIMIT):
    return pltpu.CompilerParams(dimension_semantics=sem, vmem_limit_bytes=vmem)


def _divisors(n, mult):
    return [d for d in range(mult, n + 1, mult) if n % d == 0]


def _row_tile(rows, cap):
    return max(d for d in _divisors(rows, CHUNK) if d <= max(cap, CHUNK))


def _fold8(x):
    r, c = x.shape
    return x.reshape(r // 8, 8, c).sum(axis=0)


def _split_bf16(x, parts):
    out = []
    for _ in range(parts - 1):
        hi = x.astype(BF16)
        out.append(hi)
        x = x - hi.astype(F32)
    out.append(x.astype(BF16))
    return out


def _apply01(mat, x, parts, left=True):
    acc = None
    for p in _split_bf16(x, parts):
        t = jnp.dot(mat, p, preferred_element_type=F32) if left else jnp.dot(p, mat, preferred_element_type=F32)
        acc = t if acc is None else acc + t
    return acc


def _matmul_tiles(m, n, k, mode, out_bytes):
    if mode == "tn":
        tk = _row_tile(k, 640)
    else:
        tk = max(d for d in _divisors(k, CHUNK) if d <= 1536)
    nk = k // tk
    best = None
    m_opts = _divisors(m, CHUNK)
    n_opts = _divisors(n, CHUNK)
    for tm in m_opts:
        for tn in n_opts:
            need = 2 * 2 * (tm * tk + tk * tn) + 2 * tm * tn * out_bytes
            if nk > 1 or mode == "tn":
                need += tm * tn * 4
            need += tm * tn * 4
            if need > VMEM_TILE_BUDGET:
                continue
            key = (tm * tn, tn)
            if best is None or key > best[0]:
                best = (key, tm, tn)
    return best[1], best[2], tk


def _matmul(a, b, mode, out_dtype, name):
    if mode == "nn":
        (m, k), (k2, n) = a.shape, b.shape
    elif mode == "nt":
        (m, k), (n, k2) = a.shape, b.shape
    else:
        (k, m), (k2, n) = a.shape, b.shape
    assert k == k2 and a.dtype == BF16 and b.dtype == BF16
    tm, tn, tk = _matmul_tiles(m, n, k, mode, jnp.dtype(out_dtype).itemsize)
    nk = k // tk
    if mode == "nn":
        a_spec = pl.BlockSpec((tm, tk), lambda i, j, r: (i, r))
        b_spec = pl.BlockSpec((tk, tn), lambda i, j, r: (r, j))
        dims = (((1,), (0,)), ((), ()))
    elif mode == "nt":
        a_spec = pl.BlockSpec((tm, tk), lambda i, j, r: (i, r))
        b_spec = pl.BlockSpec((tn, tk), lambda i, j, r: (j, r))
        dims = (((1,), (1,)), ((), ()))
    else:
        a_spec = pl.BlockSpec((tk, tm), lambda i, j, r: (r, i))
        b_spec = pl.BlockSpec((tk, tn), lambda i, j, r: (r, j))
        dims = (((0,), (0,)), ((), ()))

    def body(a_ref, b_ref, o_ref, *acc):
        part = lax.dot_general(a_ref[...], b_ref[...], dims, preferred_element_type=F32)
        if nk == 1:
            o_ref[...] = part.astype(o_ref.dtype)
        else:
            r = pl.program_id(2)

            @pl.when(r == 0)
            def _():
                acc[0][...] = part

            @pl.when(r > 0)
            def _():
                acc[0][...] += part

            @pl.when(r == nk - 1)
            def _():
                o_ref[...] = acc[0][...].astype(o_ref.dtype)

    return pl.pallas_call(
        body, name=name,
        out_shape=jax.ShapeDtypeStruct((m, n), out_dtype),
        grid=(m // tm, n // tn, nk),
        in_specs=[a_spec, b_spec],
        out_specs=pl.BlockSpec((tm, tn), lambda i, j, r: (i, j)),
        scratch_shapes=[pltpu.VMEM((tm, tn), F32)] if nk > 1 else [],
        compiler_params=_params(("parallel", "parallel", "arbitrary")),
    )(a, b)


def _rms(x, g):
    rstd = lax.rsqrt(jnp.mean(x * x, axis=-1, keepdims=True) + RMS_EPS)
    return x * rstd * g


def _norm_fwd(x, g):
    rows, d = x.shape
    tm = _row_tile(rows, 640)

    def body(x_ref, g_ref, h_ref):
        h_ref[...] = _rms(x_ref[...], g_ref[...]).astype(BF16)

    return pl.pallas_call(
        body, name="norm_fwd",
        out_shape=jax.ShapeDtypeStruct((rows, d), BF16),
        grid=(rows // tm,),
        in_specs=[pl.BlockSpec((tm, d), lambda i: (i, 0)), pl.BlockSpec((1, d), lambda i: (0, 0))],
        out_specs=pl.BlockSpec((tm, d), lambda i: (i, 0)),
        compiler_params=_params(("parallel",)),
    )(x, g.reshape(1, d))


def _resid_norm_fwd(h_res, y, g_post, g_next):
    rows, d = h_res.shape
    tm = _row_tile(rows, 640)

    def body(r_ref, y_ref, gp_ref, gn_ref, hn_ref, hx_ref):
        h_new = r_ref[...] + _rms(y_ref[...], gp_ref[...])
        hn_ref[...] = h_new
        hx_ref[...] = _rms(h_new, gn_ref[...]).astype(BF16)

    row = pl.BlockSpec((tm, d), lambda i: (i, 0))
    vec = pl.BlockSpec((1, d), lambda i: (0, 0))
    return pl.pallas_call(
        body, name="resid_norm_fwd",
        out_shape=(jax.ShapeDtypeStruct((rows, d), F32), jax.ShapeDtypeStruct((rows, d), BF16)),
        grid=(rows // tm,),
        in_specs=[row, row, vec, vec],
        out_specs=(row, row),
        compiler_params=_params(("parallel",)),
    )(h_res, y, g_post.reshape(1, d), g_next.reshape(1, d))


def _norm_bwd(x, g, dy, resid, out_dtype):
    rows, d = x.shape
    tm = _row_tile(rows, 640)
    has_resid = resid is not None

    def body(*refs):
        if has_resid:
            x_ref, g_ref, dy_ref, r_ref, dx_ref, dg_ref = refs
        else:
            x_ref, g_ref, dy_ref, dx_ref, dg_ref = refs
        xv = x_ref[...]
        dyv = dy_ref[...].astype(F32)
        rstd = lax.rsqrt(jnp.mean(xv * xv, axis=-1, keepdims=True) + RMS_EPS)
        xhat = xv * rstd
        gdy = dyv * g_ref[...]
        dx = rstd * (gdy - xhat * jnp.mean(gdy * xhat, axis=-1, keepdims=True))
        if has_resid:
            dx = dx + r_ref[...]
        dx_ref[...] = dx.astype(dx_ref.dtype)

        @pl.when(pl.program_id(0) == 0)
        def _():
            dg_ref[...] = jnp.zeros_like(dg_ref)

        dg_ref[...] += _fold8(dyv * xhat)

    row = pl.BlockSpec((tm, d), lambda i: (i, 0))
    vec = pl.BlockSpec((1, d), lambda i: (0, 0))
    args = [x, g.reshape(1, d), dy] + ([resid] if has_resid else [])
    dx, dg = pl.pallas_call(
        body, name="norm_bwd_resid" if has_resid else "norm_bwd",
        out_shape=(jax.ShapeDtypeStruct((rows, d), out_dtype), jax.ShapeDtypeStruct((8, d), F32)),
        grid=(rows // tm,),
        in_specs=[row, vec, row] + ([row] if has_resid else []),
        out_specs=(row, pl.BlockSpec((8, d), lambda i: (0, 0))),
        compiler_params=_params(("arbitrary",)),
    )(*args)
    return dx, dg.sum(axis=0)


def _swiglu_fwd(ab):
    rows, two_f = ab.shape
    f = two_f // 2
    tm = _row_tile(rows, 640)

    def body(a_ref, b_ref, o_ref):
        a = a_ref[...]
        o_ref[...] = (a * jax.nn.sigmoid(a) * b_ref[...]).astype(BF16)

    return pl.pallas_call(
        body, name="swiglu_fwd",
        out_shape=jax.ShapeDtypeStruct((rows, f), BF16),
        grid=(rows // tm,),
        in_specs=[pl.BlockSpec((tm, f), lambda i: (i, 0)), pl.BlockSpec((tm, f), lambda i: (i, 1))],
        out_specs=pl.BlockSpec((tm, f), lambda i: (i, 0)),
        compiler_params=_params(("parallel",)),
    )(ab, ab)


def _swiglu_bwd(ab, dff):
    rows, two_f = ab.shape
    f = two_f // 2
    tm = _row_tile(rows, 256)

    def body(a_ref, b_ref, d_ref, da_ref, db_ref):
        a = a_ref[...]
        d = d_ref[...]
        s = jax.nn.sigmoid(a)
        da_ref[...] = (d * b_ref[...] * (s * (1.0 + a * (1.0 - s)))).astype(BF16)
        db_ref[...] = (d * (a * s)).astype(BF16)

    lo = pl.BlockSpec((tm, f), lambda i: (i, 0))
    hi = pl.BlockSpec((tm, f), lambda i: (i, 1))
    return pl.pallas_call(
        body, name="swiglu_bwd",
        out_shape=(jax.ShapeDtypeStruct((rows, f), BF16), jax.ShapeDtypeStruct((rows, f), BF16)),
        grid=(rows // tm,),
        in_specs=[lo, hi, lo],
        out_specs=(lo, lo),
        compiler_params=_params(("parallel",)),
    )(ab, ab, dff)


def _tri(lower):
    r = lax.broadcasted_iota(jnp.int32, (CHUNK, CHUNK), 0)
    c = lax.broadcasted_iota(jnp.int32, (CHUNK, CHUNK), 1)
    return jnp.where((r >= c) if lower else (r <= c), 1.0, 0.0).astype(BF16)


def _logf_fwd(z, b):
    h, rows = z.shape
    n = rows // CHUNK

    def body(z_ref, b_ref, f_ref, carry):
        i = pl.program_id(0)

        @pl.when(i == 0)
        def _():
            carry[...] = jnp.zeros_like(carry)

        x = z_ref[...] + b_ref[...]
        lf = jnp.minimum(x, 0.0) - jnp.log(1.0 + jnp.exp(-jnp.abs(x)))
        col = i * CHUNK + lax.broadcasted_iota(jnp.int32, (1, CHUNK), 1)
        lf = jnp.where(col >= PAD_ROWS, lf, 0.0)
        run = _apply01(_tri(False), lf, 3, left=False) + carry[...]
        f_ref[...] = run
        carry[...] = jnp.broadcast_to(run[:, CHUNK - 1:CHUNK], carry.shape)

    blk = pl.BlockSpec((h, CHUNK), lambda i: (0, i))
    return pl.pallas_call(
        body, name="logf_fwd",
        out_shape=jax.ShapeDtypeStruct((h, rows), F32),
        grid=(n,),
        in_specs=[blk, pl.BlockSpec((h, CHUNK), lambda i: (0, 0))],
        out_specs=blk,
        scratch_shapes=[pltpu.VMEM((h, CHUNK), F32)],
        compiler_params=_params(("arbitrary",)),
    )(z, b)


def _logf_bwd(df, z, b):
    h, rows = z.shape
    n = rows // CHUNK

    def body(df_ref, z_ref, b_ref, dz_ref, db_ref, carry):
        i = pl.program_id(0)

        @pl.when(i == 0)
        def _():
            carry[...] = jnp.zeros_like(carry)
            db_ref[...] = jnp.zeros_like(db_ref)

        run = _apply01(_tri(True), df_ref[...], 3, left=False) + carry[...]
        carry[...] = jnp.broadcast_to(run[:, 0:1], carry.shape)
        x = z_ref[...] + b_ref[...]
        col = (n - 1 - i) * CHUNK + lax.broadcasted_iota(jnp.int32, (1, CHUNK), 1)
        dz = jnp.where(col >= PAD_ROWS, run * (1.0 - jax.nn.sigmoid(x)), 0.0)
        dz_ref[...] = dz
        db_ref[...] += dz

    rev = pl.BlockSpec((h, CHUNK), lambda i: (0, n - 1 - i))
    fix = pl.BlockSpec((h, CHUNK), lambda i: (0, 0))
    dz, db = pl.pallas_call(
        body, name="logf_bwd",
        out_shape=(jax.ShapeDtypeStruct((h, rows), F32), jax.ShapeDtypeStruct((h, CHUNK), F32)),
        grid=(n,),
        in_specs=[rev, rev, fix],
        out_specs=(rev, fix),
        scratch_shapes=[pltpu.VMEM((h, CHUNK), F32)],
        compiler_params=_params(("arbitrary",)),
    )(df, z, b)
    return dz, db.sum(axis=1)


def _lane_lo():
    return lax.broadcasted_iota(jnp.int32, (1, CHUNK), 1) < HEAD_DIM


def _attn_block(rows):
    return _row_tile(rows, min(640, rows // 2))


def _masked_logits(s, i, j, blk):
    row = i * blk + lax.broadcasted_iota(jnp.int32, (blk, 1), 0)
    col = j * blk + lax.broadcasted_iota(jnp.int32, (1, blk), 1)
    return jnp.where(col <= row, jnp.where(col >= PAD_ROWS, s, NEG_INF), NEG_INF)


def _attn_fwd(qkv, f_pairs, d):
    rows = qkv.shape[0]
    hp = d // CHUNK
    blk = _attn_block(rows)
    nb = rows // blk
    scale = 1.0 / math.sqrt(HEAD_DIM)
    nt = (((1,), (1,)), ((), ()))

    def body(q_ref, k_ref, v_ref, f_ref, o_ref, lse_ref, m_s, l_s, acc_s):
        i = pl.program_id(1)
        j = pl.program_id(2)
        lo = _lane_lo()

        @pl.when(j == 0)
        def _():
            m_s[...] = jnp.full(m_s.shape, NEG_INF, F32)
            l_s[...] = jnp.zeros_like(l_s)
            acc_s[...] = jnp.zeros_like(acc_s)

        def step(masked):
            q = (q_ref[...].astype(F32) * scale).astype(BF16)
            k = k_ref[...]
            v = v_ref[...]
            zero = jnp.zeros_like(k)
            alphas, pvs = [], []
            for hh in range(2):
                kh = jnp.where(lo, k, zero) if hh == 0 else jnp.where(lo, zero, k)
                s = lax.dot_general(q, kh, nt, preferred_element_type=F32) - f_ref[hh:hh + 1, :]
                if masked:
                    s = _masked_logits(s, i, j, blk)
                m_prev = m_s[hh]
                m_new = jnp.maximum(m_prev, s.max(axis=-1, keepdims=True))
                alpha = jnp.exp(m_prev - m_new)
                p = jnp.exp(s - m_new)
                l_s[hh] = alpha * l_s[hh] + p.sum(axis=-1, keepdims=True)
                m_s[hh] = m_new
                alphas.append(alpha)
                pvs.append(jnp.dot(p.astype(BF16), v, preferred_element_type=F32))
            acc_s[...] = acc_s[...] * jnp.where(lo, alphas[0], alphas[1]) + jnp.where(lo, pvs[0], pvs[1])

        edge = (j == i) | (j == 0)

        @pl.when(edge & (j <= i))
        def _():
            step(True)

        @pl.when(jnp.logical_not(edge) & (j <= i))
        def _():
            step(False)

        @pl.when(j == i)
        def _():
            row = i * blk + lax.broadcasted_iota(jnp.int32, (blk, 1), 0)
            inv = jnp.where(lo, 1.0 / l_s[0], 1.0 / l_s[1])
            o_ref[...] = jnp.where(row >= PAD_ROWS, acc_s[...] * inv, 0.0)
            lse_ref[...] = jnp.where(lo, m_s[0] + jnp.log(l_s[0]), m_s[1] + jnp.log(l_s[1]))

    q_spec = pl.BlockSpec((blk, CHUNK), lambda h, i, j: (i, h))
    k_spec = pl.BlockSpec((blk, CHUNK), lambda h, i, j: (jnp.minimum(j, i), hp + h))
    v_spec = pl.BlockSpec((blk, CHUNK), lambda h, i, j: (jnp.minimum(j, i), 2 * hp + h))
    f_spec = pl.BlockSpec((None, 2, blk), lambda h, i, j: (h, 0, jnp.minimum(j, i)))
    o_spec = pl.BlockSpec((blk, CHUNK), lambda h, i, j: (i, h))
    return pl.pallas_call(
        body, name="attn_fwd",
        out_shape=(jax.ShapeDtypeStruct((rows, d), F32), jax.ShapeDtypeStruct((rows, d), F32)),
        grid=(hp, nb, nb),
        in_specs=[q_spec, k_spec, v_spec, f_spec],
        out_specs=(o_spec, o_spec),
        scratch_shapes=[pltpu.VMEM((2, blk, 1), F32), pltpu.VMEM((2, blk, 1), F32), pltpu.VMEM((blk, CHUNK), F32)],
        compiler_params=_params(("parallel", "parallel", "arbitrary")),
    )(qkv, qkv, qkv, f_pairs)


def _attn_bwd(qkv, do, lse, delta, f_pairs, d):
    rows = qkv.shape[0]
    hp = d // CHUNK
    blk = _attn_block(rows)
    nb = rows // blk
    scale = 1.0 / math.sqrt(HEAD_DIM)
    nt = (((1,), (1,)), ((), ()))
    tn = (((0,), (0,)), ((), ()))

    def body(q_ref, k_ref, v_ref, do_ref, lse_ref, dl_ref, f_ref, dq_ref, dk_ref, dv_ref, df_ref, rs_ref,
             dq_s, dk_s, dv_s, df_s, rs_s):
        j = pl.program_id(1)
        i = pl.program_id(2)
        lo = _lane_lo()

        @pl.when((j == 0) & (i == 0))
        def _():
            dq_s[...] = jnp.zeros_like(dq_s)
            rs_s[...] = jnp.zeros_like(rs_s)

        @pl.when(i == j)
        def _():
            dk_s[...] = jnp.zeros_like(dk_s)
            dv_s[...] = jnp.zeros_like(dv_s)
            df_s[...] = jnp.zeros_like(df_s)

        def step(masked):
            q = (q_ref[...].astype(F32) * scale).astype(BF16)
            k = k_ref[...]
            v = v_ref[...]
            dov = do_ref[...]
            zero = jnp.zeros_like(k)
            dq_acc = dk_acc = dv_acc = None
            row_sums = []
            for hh in range(2):
                sel = (lambda t: jnp.where(lo, t, zero)) if hh == 0 else (lambda t: jnp.where(lo, zero, t))
                kh, vh, qh, doh = sel(k), sel(v), sel(q), sel(dov)
                off = hh * HEAD_DIM
                s = lax.dot_general(q, kh, nt, preferred_element_type=F32) - f_ref[hh:hh + 1, :]
                if masked:
                    s = _masked_logits(s, i, j, blk)
                p = jnp.exp(s - lse_ref[:, off:off + 1])
                dp = lax.dot_general(dov, vh, nt, preferred_element_type=F32)
                ds = p * (dp - dl_ref[:, off:off + 1])
                df_s[hh:hh + 1, :] += ds.sum(axis=0, keepdims=True)
                row_sums.append(ds.sum(axis=1, keepdims=True))
                pb = p.astype(BF16)
                dsb = ds.astype(BF16)
                t_dv = lax.dot_general(pb, doh, tn, preferred_element_type=F32)
                t_dk = lax.dot_general(dsb, qh, tn, preferred_element_type=F32)
                t_dq = jnp.dot(dsb, kh, preferred_element_type=F32)
                dv_acc = t_dv if dv_acc is None else dv_acc + t_dv
                dk_acc = t_dk if dk_acc is None else dk_acc + t_dk
                dq_acc = t_dq if dq_acc is None else dq_acc + t_dq
            dv_s[...] += dv_acc
            dk_s[...] += dk_acc
            r0 = pl.multiple_of(i * blk, blk)
            dq_s[pl.ds(r0, blk), :] += dq_acc
            rs_s[pl.ds(r0, blk), :] += jnp.where(lo, row_sums[0], row_sums[1])

        edge = (j == i) | (j == 0)

        @pl.when(edge & (i >= j))
        def _():
            step(True)

        @pl.when(jnp.logical_not(edge) & (i >= j))
        def _():
            step(False)

        @pl.when(i == nb - 1)
        def _():
            dk_ref[...] = dk_s[...].astype(BF16)
            dv_ref[...] = dv_s[...].astype(BF16)
            df_ref[...] = -df_s[...]

        @pl.when((i == nb - 1) & (j == nb - 1))
        def _():
            dq_ref[...] = (dq_s[...] * scale).astype(BF16)
            rs_ref[...] = rs_s[...]

    qi = lambda h, j, i: (jnp.maximum(i, j), h)
    q_spec = pl.BlockSpec((blk, CHUNK), qi)
    k_spec = pl.BlockSpec((blk, CHUNK), lambda h, j, i: (j, hp + h))
    v_spec = pl.BlockSpec((blk, CHUNK), lambda h, j, i: (j, 2 * hp + h))
    f_spec = pl.BlockSpec((None, 2, blk), lambda h, j, i: (h, 0, j))
    kv_out = pl.BlockSpec((blk, CHUNK), lambda h, j, i: (j, h))
    dq_out = pl.BlockSpec((rows, CHUNK), lambda h, j, i: (0, h))
    return pl.pallas_call(
        body, name="attn_bwd",
        out_shape=(jax.ShapeDtypeStruct((rows, d), BF16), jax.ShapeDtypeStruct((rows, d), BF16),
                   jax.ShapeDtypeStruct((rows, d), BF16), jax.ShapeDtypeStruct((hp, 2, rows), F32),
                   jax.ShapeDtypeStruct((rows, d), F32)),
        grid=(hp, nb, nb),
        in_specs=[q_spec, k_spec, v_spec, q_spec, q_spec, q_spec, f_spec],
        out_specs=(dq_out, kv_out, kv_out, f_spec, dq_out),
        scratch_shapes=[pltpu.VMEM((rows, CHUNK), F32), pltpu.VMEM((blk, CHUNK), F32),
                        pltpu.VMEM((blk, CHUNK), F32), pltpu.VMEM((2, blk), F32), pltpu.VMEM((rows, CHUNK), F32)],
        compiler_params=_params(("parallel", "arbitrary", "arbitrary")),
    )(qkv, qkv, qkv, do, lse, delta, f_pairs)


def _band(w, transposed, other):
    r = lax.broadcasted_iota(jnp.int32, (CHUNK, CHUNK), 0)
    c = lax.broadcasted_iota(jnp.int32, (CHUNK, CHUNK), 1)
    dist = (c - r) if transposed else (r - c)
    if other:
        dist = dist + CHUNK
    return jnp.where(dist >= 0, jnp.where(dist < w, 1.0, 0.0), 0.0).astype(BF16)


def _inv_count(chunk_index, w):
    row = chunk_index * CHUNK + lax.broadcasted_iota(jnp.int32, (CHUNK, 1), 0)
    cnt = jnp.clip(row - PAD_ROWS + 1, 1, w).astype(F32)
    return 1.0 / cnt


def _pool_diff(u_cur, u_prev, i, w):
    ws = _apply01(_band(w, False, False), u_cur, 3)
    ws = ws + jnp.where(i > 0, _apply01(_band(w, False, True), u_prev, 3), 0.0)
    return ws * _inv_count(i, w) - u_cur


def _pool_merge_fwd(rest, o, w_pool, scale, d):
    rows = rest.shape[0]
    n = rows // CHUNK
    cg = d // len(POOL_WINDOWS)

    def body(up_ref, uc_ref, gp_ref, ga_ref, o_ref, wp_ref, sc_ref, mg_ref, yp_ref):
        i = pl.program_id(0)
        for g, w in enumerate(POOL_WINDOWS):
            sl = slice(g * cg, (g + 1) * cg)
            diff = _pool_diff(uc_ref[:, sl], up_ref[:, sl], i, w)
            ypre = jnp.dot(diff.astype(BF16), wp_ref[g], preferred_element_type=F32)
            yp_ref[:, sl] = ypre
            merged = (jax.nn.sigmoid(gp_ref[:, sl]) * (ypre * sc_ref[:, sl])
                      + jax.nn.sigmoid(ga_ref[:, sl]) * o_ref[:, sl])
            mg_ref[:, sl] = merged.astype(BF16)

    col = lambda c: pl.BlockSpec((CHUNK, d), lambda i: (i, c))
    return pl.pallas_call(
        body, name="pool_merge_fwd",
        out_shape=(jax.ShapeDtypeStruct((rows, d), BF16), jax.ShapeDtypeStruct((rows, d), F32)),
        grid=(n,),
        in_specs=[pl.BlockSpec((CHUNK, d), lambda i: (jnp.maximum(i - 1, 0), 0)), col(0), col(1), col(2), col(0),
                  pl.BlockSpec((len(POOL_WINDOWS), cg, cg), lambda i: (0, 0, 0)),
                  pl.BlockSpec((1, d), lambda i: (0, 0))],
        out_specs=(col(0), col(0)),
        compiler_params=_params(("parallel",)),
    )(rest, rest, rest, rest, o, w_pool, scale.reshape(1, d))


def _gate_bwd(dm, rest, o, ypre, scale, d):
    rows = rest.shape[0]
    n = rows // CHUNK

    def body(dm_ref, gp_ref, ga_ref, o_ref, yp_ref, sc_ref, dgp_ref, dga_ref, do_ref, dl_ref, dy_ref, ds_ref):
        @pl.when(pl.program_id(0) == 0)
        def _():
            ds_ref[...] = jnp.zeros_like(ds_ref)

        dmv = dm_ref[...]
        sp = jax.nn.sigmoid(gp_ref[...])
        sa = jax.nn.sigmoid(ga_ref[...])
        ov = o_ref[...]
        ypre_v = yp_ref[...]
        sc = sc_ref[...]
        dgp_ref[...] = (dmv * (ypre_v * sc) * (sp * (1.0 - sp))).astype(BF16)
        dga_ref[...] = (dmv * ov * (sa * (1.0 - sa))).astype(BF16)
        t = dmv * sp
        dy_ref[...] = (t * sc).astype(BF16)
        ds_ref[...] += _fold8(t * ypre_v)
        dob = (dmv * sa).astype(BF16)
        do_ref[...] = dob
        prod = dob.astype(F32) * ov
        lo = _lane_lo()
        for pr in range(d // CHUNK):
            sl = slice(pr * CHUNK, (pr + 1) * CHUNK)
            tp = prod[:, sl]
            s_lo = jnp.where(lo, tp, 0.0).sum(axis=-1, keepdims=True)
            s_hi = jnp.where(lo, 0.0, tp).sum(axis=-1, keepdims=True)
            dl_ref[:, sl] = jnp.where(lo, s_lo, s_hi)

    col = lambda c: pl.BlockSpec((CHUNK, d), lambda i: (i, c))
    row_bf = jax.ShapeDtypeStruct((rows, d), BF16)
    outs = pl.pallas_call(
        body, name="gate_bwd",
        out_shape=(row_bf, row_bf, row_bf, jax.ShapeDtypeStruct((rows, d), F32), row_bf,
                   jax.ShapeDtypeStruct((8, d), F32)),
        grid=(n,),
        in_specs=[col(0), col(1), col(2), col(0), col(0), pl.BlockSpec((1, d), lambda i: (0, 0))],
        out_specs=(col(0), col(0), col(0), col(0), col(0), pl.BlockSpec((8, d), lambda i: (0, 0))),
        compiler_params=_params(("arbitrary",)),
    )(dm, rest, rest, o, ypre, scale.reshape(1, d))
    return outs[:5] + (outs[5].sum(axis=0),)


def _pool_bwd(dypre, rest, w_pool, d):
    rows = rest.shape[0]
    n = rows // CHUNK
    ng = len(POOL_WINDOWS)
    cg = d // ng
    nt = (((1,), (1,)), ((), ()))
    tn = (((0,), (0,)), ((), ()))

    def body(dc_ref, dn_ref, up_ref, uc_ref, wp_ref, du_ref, dw_ref):
        i = pl.program_id(0)

        @pl.when(i == 0)
        def _():
            dw_ref[...] = jnp.zeros_like(dw_ref)

        row = i * CHUNK + lax.broadcasted_iota(jnp.int32, (CHUNK, 1), 0)
        for g, w in enumerate(POOL_WINDOWS):
            sl = slice(g * cg, (g + 1) * cg)
            diff = _pool_diff(uc_ref[:, sl], up_ref[:, sl], i, w)
            dyc = dc_ref[:, sl]
            dw_ref[g] += lax.dot_general(diff.astype(BF16), dyc, tn, preferred_element_type=F32)
            wg = wp_ref[g]
            dd_cur = lax.dot_general(dyc, wg, nt, preferred_element_type=F32)
            dd_next = lax.dot_general(dn_ref[:, sl], wg, nt, preferred_element_type=F32)
            du = _apply01(_band(w, True, False), dd_cur * _inv_count(i, w), 2)
            du = du + jnp.where(i < n - 1, _apply01(_band(w, True, True), dd_next * _inv_count(i + 1, w), 2), 0.0)
            du = du - dd_cur
            du_ref[:, sl] = jnp.where(row >= PAD_ROWS, du, 0.0).astype(BF16)

    cur = pl.BlockSpec((CHUNK, d), lambda i: (i, 0))
    return pl.pallas_call(
        body, name="pool_bwd",
        out_shape=(jax.ShapeDtypeStruct((rows, d), BF16), jax.ShapeDtypeStruct((ng, cg, cg), F32)),
        grid=(n,),
        in_specs=[cur, pl.BlockSpec((CHUNK, d), lambda i: (jnp.minimum(i + 1, n - 1), 0)),
                  pl.BlockSpec((CHUNK, d), lambda i: (jnp.maximum(i - 1, 0), 0)), cur,
                  pl.BlockSpec((ng, cg, cg), lambda i: (0, 0, 0))],
        out_specs=(cur, pl.BlockSpec((ng, cg, cg), lambda i: (0, 0, 0))),
        compiler_params=_params(("arbitrary",)),
    )(dypre, dypre, rest, rest, w_pool)


def _loss_grad(h_res, target):
    rows, d = h_res.shape
    n = rows // CHUNK

    def body(h_ref, t_ref, dh_ref, acc_ref):
        i = pl.program_id(0)

        @pl.when(i == 0)
        def _():
            acc_ref[...] = jnp.zeros_like(acc_ref)
            dh_ref[...] = jnp.zeros_like(dh_ref)

        @pl.when(i > 0)
        def _():
            err = h_ref[...] - t_ref[...]
            dh_ref[...] = err * (1.0 / d)
            e2 = _fold8(err * err)
            part = e2[:, 0:CHUNK]
            for c in range(1, d // CHUNK):
                part = part + e2[:, c * CHUNK:(c + 1) * CHUNK]
            acc_ref[...] += part

    dh, acc = pl.pallas_call(
        body, name="loss_grad",
        out_shape=(jax.ShapeDtypeStruct((rows, d), F32), jax.ShapeDtypeStruct((8, CHUNK), F32)),
        grid=(n,),
        in_specs=[pl.BlockSpec((CHUNK, d), lambda i: (i, 0)),
                  pl.BlockSpec((CHUNK, d), lambda i: (jnp.maximum(i - 1, 0), 0))],
        out_specs=(pl.BlockSpec((CHUNK, d), lambda i: (i, 0)), pl.BlockSpec((8, CHUNK), lambda i: (0, 0))),
        compiler_params=_params(("arbitrary",)),
    )(h_res, target)
    return dh, (0.5 / d) * acc.sum()


def _adamw(slots, w, m, v, name):
    rows, cols = w.shape
    lanes = -(-cols // CHUNK) * CHUNK
    row_bytes = lanes * (N_DEV * slots.dtype.itemsize + 7 * 4)
    tr = max(t for t in _divisors(rows, 8) if t <= max(8, ADAM_TILE_BYTES // row_bytes))
    c1 = 1.0 - ADAM_B1 ** ADAM_STEP
    c2 = 1.0 - ADAM_B2 ** ADAM_STEP

    def body(s_ref, w_ref, m_ref, v_ref, g_ref, d_ref, mo_ref, vo_ref):
        g = s_ref[0].astype(F32)
        for k in range(1, N_DEV):
            g = g + s_ref[k].astype(F32)
        m_new = ADAM_B1 * m_ref[...] + (1.0 - ADAM_B1) * g
        v_new = ADAM_B2 * v_ref[...] + (1.0 - ADAM_B2) * (g * g)
        m_hat = m_new / c1
        v_hat = v_new / c2
        g_ref[...] = g
        d_ref[...] = -ADAM_LR * (m_hat / (jnp.sqrt(v_hat) + ADAM_EPS) + ADAM_WD * w_ref[...])
        mo_ref[...] = m_new
        vo_ref[...] = v_new

    tile = pl.BlockSpec((tr, cols), lambda i: (i, 0))
    sds = jax.ShapeDtypeStruct((rows, cols), F32)
    return pl.pallas_call(
        body, name=name,
        out_shape=(sds, sds, sds, sds),
        grid=(rows // tr,),
        in_specs=[pl.BlockSpec((N_DEV, tr, cols), lambda i: (0, i, 0)), tile, tile, tile],
        out_specs=(tile, tile, tile, tile),
        compiler_params=_params(("parallel",)),
    )(slots, w, m, v)


def _exchange(srcs, gather, name):
    n = len(srcs)
    shapes = [((N_DEV,) + s.shape) if gather else s.shape for s in srcs]

    def body(*refs):
        src_refs, out_refs = refs[:n], refs[n:2 * n]
        send_sems, recv_sems, local_sems = refs[2 * n:]
        x, y, c = lax.axis_index("x"), lax.axis_index("y"), lax.axis_index("c")
        me = 4 * x + 2 * y + c

        def payload(a, slot):
            return src_refs[a] if gather else src_refs[a].at[slot]

        own = [pltpu.make_async_copy(payload(a, me), out_refs[a].at[me], local_sems.at[a]) for a in range(n)]
        for cp in own:
            cp.start()
        sends, recvs = [], []
        for k in range(1, N_DEV):
            px = 1 - x if k & 4 else x
            py = 1 - y if k & 2 else y
            pc = 1 - c if k & 1 else c
            peer = 4 * px + 2 * py + pc
            for a in range(n):
                sems = dict(send_sem=send_sems.at[(k - 1) * n + a], recv_sem=recv_sems.at[(k - 1) * n + a],
                            device_id=(px, py, pc), device_id_type=MESH_ID)
                sends.append(pltpu.make_async_remote_copy(src_ref=payload(a, peer), dst_ref=out_refs[a].at[me], **sems))
                recvs.append(pltpu.make_async_remote_copy(src_ref=payload(a, peer), dst_ref=out_refs[a].at[peer], **sems))
        for cp in sends:
            cp.start()
        for cp in recvs:
            cp.wait_recv()
        for cp in sends:
            cp.wait_send()
        for cp in own:
            cp.wait()

    return pl.pallas_call(
        body, name=name,
        out_shape=tuple(jax.ShapeDtypeStruct(sh, s.dtype) for sh, s in zip(shapes, srcs)),
        in_specs=[pl.BlockSpec(memory_space=pl.ANY)] * n,
        out_specs=tuple([pl.BlockSpec(memory_space=pl.ANY)] * n),
        scratch_shapes=[pltpu.SemaphoreType.DMA(((N_DEV - 1) * n,)), pltpu.SemaphoreType.DMA(((N_DEV - 1) * n,)),
                        pltpu.SemaphoreType.DMA((n,))],
    )(*srcs)


BIG = ("w_in", "w_pool", "w_out", "w_gate", "w_up", "w_down")
SHARD_AXIS = dict(w_in=-1, w_pool=-2, w_out=-2, w_gate=-1, w_up=-1, w_down=-2)
SMALL = ("norm_mix_pre", "norm_mix_post", "norm_ffn_pre", "norm_ffn_post", "pool_scale")


def _join(g, axis):
    return jnp.concatenate([g[j] for j in range(N_DEV)], axis=axis)


def _split(full, axis):
    return jnp.stack(jnp.split(full, N_DEV, axis=axis))


def _rows2d(a):
    return a.reshape(-1, a.shape[-1])


def kernel(x, meta_tokens, norm_mix_pre, norm_mix_post, norm_ffn_pre, norm_ffn_post, w_in, b_forget, w_pool, pool_scale, w_out, w_ffn_gate, w_ffn_up, w_ffn_down, loss_target, m_meta_tokens, m_norm_mix_pre, m_norm_mix_post, m_norm_ffn_pre, m_norm_ffn_post, m_w_in, m_b_forget, m_w_pool, m_pool_scale, m_w_out, m_w_ffn_gate, m_w_ffn_up, m_w_ffn_down, v_meta_tokens, v_norm_mix_pre, v_norm_mix_post, v_norm_ffn_pre, v_norm_ffn_post, v_w_in, v_b_forget, v_w_pool, v_pool_scale, v_w_out, v_w_ffn_gate, v_w_ffn_up, v_w_ffn_down):
    x2 = x[0]
    target = loss_target[0]
    seq, d = x2.shape
    depth = w_in.shape[0]
    heads = d // HEAD_DIM
    ff = w_ffn_gate.shape[2] * N_DEV
    rows = PAD_ROWS + META_TOKENS + seq
    assert seq % CHUNK == 0 and d % (2 * CHUNK) == 0 and heads <= FORGET_PAD
    me = 4 * lax.axis_index("x") + 2 * lax.axis_index("y") + lax.axis_index("c")

    big = dict(w_in=w_in, w_pool=w_pool, w_out=w_out, w_gate=w_ffn_gate, w_up=w_ffn_up, w_down=w_ffn_down)
    big_m = dict(w_in=m_w_in, w_pool=m_w_pool, w_out=m_w_out, w_gate=m_w_ffn_gate, w_up=m_w_ffn_up, w_down=m_w_ffn_down)
    big_v = dict(w_in=v_w_in, w_pool=v_w_pool, w_out=v_w_out, w_gate=v_w_ffn_gate, w_up=v_w_ffn_up, w_down=v_w_ffn_down)
    small = dict(norm_mix_pre=norm_mix_pre, norm_mix_post=norm_mix_post, norm_ffn_pre=norm_ffn_pre,
                 norm_ffn_post=norm_ffn_post, pool_scale=pool_scale)
    small_m = dict(norm_mix_pre=m_norm_mix_pre, norm_mix_post=m_norm_mix_post, norm_ffn_pre=m_norm_ffn_pre,
                   norm_ffn_post=m_norm_ffn_post, pool_scale=m_pool_scale)
    small_v = dict(norm_mix_pre=v_norm_mix_pre, norm_mix_post=v_norm_mix_post, norm_ffn_pre=v_norm_ffn_pre,
                   norm_ffn_post=v_norm_ffn_post, pool_scale=v_pool_scale)

    gathered = _exchange([big[n].astype(BF16) for n in BIG] + [meta_tokens], True, "gather_weights")
    full = {n: _join(g, SHARD_AXIS[n]) for n, g in zip(BIG, gathered)}
    meta_full = _join(gathered[-1], -1)

    win = full["w_in"]
    fcol = 4 * d
    w_qkv = win[:, :, d:4 * d]
    w_rest = jnp.concatenate([win[:, :, :d], win[:, :, fcol + heads:], win[:, :, fcol:fcol + heads],
                              jnp.zeros((depth, d, FORGET_PAD - heads), BF16)], axis=2)
    w_cat = jnp.concatenate([w_qkv, w_rest], axis=2)
    w_gu = jnp.concatenate([full["w_gate"], full["w_up"]], axis=2)
    w_o, w_dn, w_pl = full["w_out"], full["w_down"], full["w_pool"]

    h_res = jnp.concatenate([jnp.zeros((PAD_ROWS, d), F32), meta_full, x2], axis=0)
    h1 = _norm_fwd(h_res, norm_mix_pre[0])
    ones = jnp.ones((d,), F32)
    saved = []
    for l in range(depth):
        qkv = _matmul(h1, w_qkv[l], "nn", BF16, "proj_qkv")
        rest = _matmul(h1, w_rest[l], "nn", F32, "proj_rest")
        z = rest[:, 3 * d:3 * d + heads].T
        bias = jnp.broadcast_to(b_forget[l][:, None], (heads, CHUNK))
        f_run = _logf_fwd(z, bias).reshape(heads // 2, 2, rows)
        o, lse = _attn_fwd(qkv, f_run, d)
        merged, ypre = _pool_merge_fwd(rest, o, w_pl[l], pool_scale[l], d)
        mix = _matmul(merged, w_o[l], "nn", F32, "mix_out")
        h_mid, h2 = _resid_norm_fwd(h_res, mix, norm_mix_post[l], norm_ffn_pre[l])
        ab = _matmul(h2, w_gu[l], "nn", F32, "ffn_in")
        act = _swiglu_fwd(ab)
        ffo = _matmul(act, w_dn[l], "nn", F32, "ffn_out")
        g_next = norm_mix_pre[l + 1] if l + 1 < depth else ones
        h_next, h1_next = _resid_norm_fwd(h_mid, ffo, norm_ffn_post[l], g_next)
        saved.append(dict(h_in=h_res, h1=h1, qkv=qkv, rest=rest, z=z, bias=bias, f_run=f_run, o=o, lse=lse,
                          merged=merged, ypre=ypre, mix=mix, h_mid=h_mid, h2=h2, ab=ab, act=act, ffo=ffo))
        h_res, h1 = h_next, h1_next

    dh, loss_local = _loss_grad(h_res, target)
    loss = lax.psum(loss_local, ("x", "y", "c"))

    grads = {n: [None] * depth for n in BIG + SMALL + ("b_forget",)}
    for l in reversed(range(depth)):
        s = saved[l]
        dffo, grads["norm_ffn_post"][l] = _norm_bwd(s["ffo"], norm_ffn_post[l], dh, None, BF16)
        grads["w_down"][l] = _matmul(s["act"], dffo, "tn", F32, "grad_w_down")
        dact = _matmul(dffo, w_dn[l], "nt", F32, "ffn_out_dx")
        da, db = _swiglu_bwd(s["ab"], dact)
        dab = jnp.concatenate([da, db], axis=1)
        dgu = _matmul(s["h2"], dab, "tn", F32, "grad_w_gu")
        grads["w_gate"][l], grads["w_up"][l] = dgu[:, :ff], dgu[:, ff:]
        dh2 = _matmul(dab, w_gu[l], "nt", F32, "ffn_in_dx")
        dh_mid, grads["norm_ffn_pre"][l] = _norm_bwd(s["h_mid"], norm_ffn_pre[l], dh2, dh, F32)

        dmix, grads["norm_mix_post"][l] = _norm_bwd(s["mix"], norm_mix_post[l], dh_mid, None, BF16)
        grads["w_out"][l] = _matmul(s["merged"], dmix, "tn", F32, "grad_w_out")
        dm = _matmul(dmix, w_o[l], "nt", F32, "mix_out_dx")
        dgp, dga, do, delta, dypre, grads["pool_scale"][l] = _gate_bwd(dm, s["rest"], s["o"], s["ypre"], pool_scale[l], d)
        du, grads["w_pool"][l] = _pool_bwd(dypre, s["rest"], w_pl[l], d)
        dq, dk, dv, df_key, df_query = _attn_bwd(s["qkv"], do, s["lse"], delta, s["f_run"], d)
        df = df_key.reshape(heads, rows) + df_query.reshape(rows, heads, HEAD_DIM)[:, :, 0].T
        dz, grads["b_forget"][l] = _logf_bwd(df, s["z"], s["bias"])
        dzt = jnp.pad(dz.T.astype(BF16), ((0, 0), (0, FORGET_PAD - heads)))
        dproj = jnp.concatenate([dq, dk, dv, du, dgp, dga, dzt], axis=1)
        dwc = _matmul(s["h1"], dproj, "tn", F32, "grad_w_in")
        grads["w_in"][l] = jnp.concatenate([dwc[:, 3 * d:4 * d], dwc[:, :3 * d], dwc[:, 6 * d:6 * d + heads],
                                            dwc[:, 4 * d:6 * d]], axis=1)
        dh1 = _matmul(dproj, w_cat[l], "nt", F32, "proj_dx")
        dh, grads["norm_mix_pre"][l] = _norm_bwd(s["h_in"], norm_mix_pre[l], dh1, dh_mid, F32)

    grad_x = dh[PAD_ROWS + META_TOKENS:][None]
    dmeta = dh[PAD_ROWS:PAD_ROWS + META_TOKENS]

    slots = [_split(jnp.stack(grads[n]), SHARD_AXIS[n]).astype(BF16) for n in BIG]
    recv = _exchange([s_.reshape(N_DEV, -1, s_.shape[-1]) for s_ in slots], False, "scatter_grads")
    big_out = {}
    for n, r in zip(BIG, recv):
        outs = _adamw(r, _rows2d(big[n]), _rows2d(big_m[n]), _rows2d(big_v[n]), "adamw_" + n)
        big_out[n] = [o_.reshape(big[n].shape) for o_ in outs]

    def table(parts, forget):
        t = jnp.concatenate([_rows2d(p) for p in parts] + [jnp.pad(forget, ((0, 0), (0, d - heads)))], axis=0)
        return jnp.pad(t, ((0, -t.shape[0] % 8), (0, 0)))

    g_table = table([jnp.stack(grads[n]) for n in SMALL], jnp.stack(grads["b_forget"]))
    rep_rows = g_table.shape[0]
    got = _exchange([jnp.concatenate([g_table, dmeta], axis=0)], True, "gather_small_grads")[0]
    outs = _adamw(got[:, :rep_rows], table([small[n] for n in SMALL], b_forget),
                  table([small_m[n] for n in SMALL], m_b_forget), table([small_v[n] for n in SMALL], v_b_forget),
                  "adamw_small")
    dcols = d // N_DEV
    meta_slots = lax.dynamic_slice_in_dim(got[:, rep_rows:], me * dcols, dcols, axis=2)
    meta_out = _adamw(meta_slots, meta_tokens, m_meta_tokens, v_meta_tokens, "adamw_meta")

    def ordered(k):
        t = outs[k]
        so = {n: t[a * depth:(a + 1) * depth] for a, n in enumerate(SMALL)}
        forget = t[len(SMALL) * depth:(len(SMALL) + 1) * depth, :heads]
        return (meta_out[k], so["norm_mix_pre"], so["norm_mix_post"], so["norm_ffn_pre"], so["norm_ffn_post"],
                big_out["w_in"][k], forget, big_out["w_pool"][k], so["pool_scale"], big_out["w_out"][k],
                big_out["w_gate"][k], big_out["w_up"][k], big_out["w_down"][k])

    return (loss, grad_x) + ordered(0) + ordered(1) + ordered(2) + ordered(3)
```

```python
import math

import jax
import jax.numpy as jnp
from jax import lax
from jax.experimental import pallas as pl
from jax.experimental.pallas import tpu as pltpu

F32 = jnp.float32
BF16 = jnp.bfloat16

N_DEV = 8
META_TOKENS = 16
PAD_ROWS = 112
CHUNK = 128
HEAD_DIM = 64
POOL_WINDOWS = (2, 4, 8, 16)
FORGET_PAD = 256
RMS_EPS = 1e-6
NEG_INF = -1e30
ADAM_LR, ADAM_B1, ADAM_B2, ADAM_EPS, ADAM_WD, ADAM_STEP = 0.001, 0.9, 0.999, 1e-08, 0.01, 10

VMEM_LIMIT = 56 * 1024 * 1024
VMEM_TILE_BUDGET = 36 * 1024 * 1024
ADAM_TILE_BYTES = 8 * 1024 * 1024
MESH_ID = pl.DeviceIdType.MESH


def _params(sem, vmem=VMEM_LIMIT):
    return pltpu.CompilerParams(dimension_semantics=sem, vmem_limit_bytes=vmem)


def _divisors(n, mult):
    return [d for d in range(mult, n + 1, mult) if n % d == 0]


def _row_tile(rows, cap):
    return max(d for d in _divisors(rows, CHUNK) if d <= max(cap, CHUNK))


def _fold8(x):
    r, c = x.shape
    return x.reshape(r // 8, 8, c).sum(axis=0)


def _split_bf16(x, parts):
    out = []
    for _ in range(parts - 1):
        hi = x.astype(BF16)
        out.append(hi)
        x = x - hi.astype(F32)
    out.append(x.astype(BF16))
    return out


def _apply01(mat, x, parts, left=True):
    acc = None
    for p in _split_bf16(x, parts):
        t = jnp.dot(mat, p, preferred_element_type=F32) if left else jnp.dot(p, mat, preferred_element_type=F32)
        acc = t if acc is None else acc + t
    return acc


def _matmul_tiles(m, n, k, mode, out_bytes):
    if mode == "tn":
        tk = _row_tile(k, 640)
    else:
        tk = max(d for d in _divisors(k, CHUNK) if d <= 1536)
    nk = k // tk
    best = None
    m_opts = _divisors(m, CHUNK)
    n_opts = _divisors(n, CHUNK)
    for tm in m_opts:
        for tn in n_opts:
            need = 2 * 2 * (tm * tk + tk * tn) + 2 * tm * tn * out_bytes
            if nk > 1 or mode == "tn":
                need += tm * tn * 4
            need += tm * tn * 4
            if need > VMEM_TILE_BUDGET:
                continue
            key = (tm * tn, tn)
            if best is None or key > best[0]:
                best = (key, tm, tn)
    return best[1], best[2], tk


def _matmul(a, b, mode, out_dtype, name):
    if mode == "nn":
        (m, k), (k2, n) = a.shape, b.shape
    elif mode == "nt":
        (m, k), (n, k2) = a.shape, b.shape
    else:
        (k, m), (k2, n) = a.shape, b.shape
    assert k == k2 and a.dtype == BF16 and b.dtype == BF16
    tm, tn, tk = _matmul_tiles(m, n, k, mode, jnp.dtype(out_dtype).itemsize)
    nk = k // tk
    if mode == "nn":
        a_spec = pl.BlockSpec((tm, tk), lambda i, j, r: (i, r))
        b_spec = pl.BlockSpec((tk, tn), lambda i, j, r: (r, j))
        dims = (((1,), (0,)), ((), ()))
    elif mode == "nt":
        a_spec = pl.BlockSpec((tm, tk), lambda i, j, r: (i, r))
        b_spec = pl.BlockSpec((tn, tk), lambda i, j, r: (j, r))
        dims = (((1,), (1,)), ((), ()))
    else:
        a_spec = pl.BlockSpec((tk, tm), lambda i, j, r: (r, i))
        b_spec = pl.BlockSpec((tk, tn), lambda i, j, r: (r, j))
        dims = (((0,), (0,)), ((), ()))

    def body(a_ref, b_ref, o_ref, *acc):
        part = lax.dot_general(a_ref[...], b_ref[...], dims, preferred_element_type=F32)
        if nk == 1:
            o_ref[...] = part.astype(o_ref.dtype)
        else:
            r = pl.program_id(2)

            @pl.when(r == 0)
            def _():
                acc[0][...] = part

            @pl.when(r > 0)
            def _():
                acc[0][...] += part

            @pl.when(r == nk - 1)
            def _():
                o_ref[...] = acc[0][...].astype(o_ref.dtype)

    return pl.pallas_call(
        body, name=name,
        out_shape=jax.ShapeDtypeStruct((m, n), out_dtype),
        grid=(m // tm, n // tn, nk),
        in_specs=[a_spec, b_spec],
        out_specs=pl.BlockSpec((tm, tn), lambda i, j, r: (i, j)),
        scratch_shapes=[pltpu.VMEM((tm, tn), F32)] if nk > 1 else [],
        compiler_params=_params(("parallel", "parallel", "arbitrary")),
    )(a, b)


def _rms(x, g):
    rstd = lax.rsqrt(jnp.mean(x * x, axis=-1, keepdims=True) + RMS_EPS)
    return x * rstd * g


def _norm_fwd(x, g):
    rows, d = x.shape
    tm = _row_tile(rows, 640)

    def body(x_ref, g_ref, h_ref):
        h_ref[...] = _rms(x_ref[...], g_ref[...]).astype(BF16)

    return pl.pallas_call(
        body, name="norm_fwd",
        out_shape=jax.ShapeDtypeStruct((rows, d), BF16),
        grid=(rows // tm,),
        in_specs=[pl.BlockSpec((tm, d), lambda i: (i, 0)), pl.BlockSpec((1, d), lambda i: (0, 0))],
        out_specs=pl.BlockSpec((tm, d), lambda i: (i, 0)),
        compiler_params=_params(("parallel",)),
    )(x, g.reshape(1, d))


def _resid_norm_fwd(h_res, y, g_post, g_next):
    rows, d = h_res.shape
    tm = _row_tile(rows, 640)

    def body(r_ref, y_ref, gp_ref, gn_ref, hn_ref, hx_ref):
        h_new = r_ref[...] + _rms(y_ref[...], gp_ref[...])
        hn_ref[...] = h_new
        hx_ref[...] = _rms(h_new, gn_ref[...]).astype(BF16)

    row = pl.BlockSpec((tm, d), lambda i: (i, 0))
    vec = pl.BlockSpec((1, d), lambda i: (0, 0))
    return pl.pallas_call(
        body, name="resid_norm_fwd",
        out_shape=(jax.ShapeDtypeStruct((rows, d), F32), jax.ShapeDtypeStruct((rows, d), BF16)),
        grid=(rows // tm,),
        in_specs=[row, row, vec, vec],
        out_specs=(row, row),
        compiler_params=_params(("parallel",)),
    )(h_res, y, g_post.reshape(1, d), g_next.reshape(1, d))


def _norm_bwd(x, g, dy, resid, out_dtype):
    rows, d = x.shape
    tm = _row_tile(rows, 640)
    has_resid = resid is not None

    def body(*refs):
        if has_resid:
            x_ref, g_ref, dy_ref, r_ref, dx_ref, dg_ref = refs
        else:
            x_ref, g_ref, dy_ref, dx_ref, dg_ref = refs
        xv = x_ref[...]
        dyv = dy_ref[...].astype(F32)
        rstd = lax.rsqrt(jnp.mean(xv * xv, axis=-1, keepdims=True) + RMS_EPS)
        xhat = xv * rstd
        gdy = dyv * g_ref[...]
        dx = rstd * (gdy - xhat * jnp.mean(gdy * xhat, axis=-1, keepdims=True))
        if has_resid:
            dx = dx + r_ref[...]
        dx_ref[...] = dx.astype(dx_ref.dtype)

        @pl.when(pl.program_id(0) == 0)
        def _():
            dg_ref[...] = jnp.zeros_like(dg_ref)

        dg_ref[...] += _fold8(dyv * xhat)

    row = pl.BlockSpec((tm, d), lambda i: (i, 0))
    vec = pl.BlockSpec((1, d), lambda i: (0, 0))
    args = [x, g.reshape(1, d), dy] + ([resid] if has_resid else [])
    dx, dg = pl.pallas_call(
        body, name="norm_bwd_resid" if has_resid else "norm_bwd",
        out_shape=(jax.ShapeDtypeStruct((rows, d), out_dtype), jax.ShapeDtypeStruct((8, d), F32)),
        grid=(rows // tm,),
        in_specs=[row, vec, row] + ([row] if has_resid else []),
        out_specs=(row, pl.BlockSpec((8, d), lambda i: (0, 0))),
        compiler_params=_params(("arbitrary",)),
    )(*args)
    return dx, dg.sum(axis=0)


def _swiglu_fwd(ab):
    rows, two_f = ab.shape
    f = two_f // 2
    tm = _row_tile(rows, 640)

    def body(a_ref, b_ref, o_ref):
        a = a_ref[...]
        o_ref[...] = (a * jax.nn.sigmoid(a) * b_ref[...]).astype(BF16)

    return pl.pallas_call(
        body, name="swiglu_fwd",
        out_shape=jax.ShapeDtypeStruct((rows, f), BF16),
        grid=(rows // tm,),
        in_specs=[pl.BlockSpec((tm, f), lambda i: (i, 0)), pl.BlockSpec((tm, f), lambda i: (i, 1))],
        out_specs=pl.BlockSpec((tm, f), lambda i: (i, 0)),
        compiler_params=_params(("parallel",)),
    )(ab, ab)


def _swiglu_bwd(ab, dff):
    rows, two_f = ab.shape
    f = two_f // 2
    tm = _row_tile(rows, 256)

    def body(a_ref, b_ref, d_ref, da_ref, db_ref):
        a = a_ref[...]
        d = d_ref[...]
        s = jax.nn.sigmoid(a)
        da_ref[...] = (d * b_ref[...] * (s * (1.0 + a * (1.0 - s)))).astype(BF16)
        db_ref[...] = (d * (a * s)).astype(BF16)

    lo = pl.BlockSpec((tm, f), lambda i: (i, 0))
    hi = pl.BlockSpec((tm, f), lambda i: (i, 1))
    return pl.pallas_call(
        body, name="swiglu_bwd",
        out_shape=(jax.ShapeDtypeStruct((rows, f), BF16), jax.ShapeDtypeStruct((rows, f), BF16)),
        grid=(rows // tm,),
        in_specs=[lo, hi, lo],
        out_specs=(lo, lo),
        compiler_params=_params(("parallel",)),
    )(ab, ab, dff)


def _tri(lower):
    r = lax.broadcasted_iota(jnp.int32, (CHUNK, CHUNK), 0)
    c = lax.broadcasted_iota(jnp.int32, (CHUNK, CHUNK), 1)
    return jnp.where((r >= c) if lower else (r <= c), 1.0, 0.0).astype(BF16)


def _logf_fwd(z, b):
    h, rows = z.shape
    n = rows // CHUNK

    def body(z_ref, b_ref, f_ref, carry):
        i = pl.program_id(0)

        @pl.when(i == 0)
        def _():
            carry[...] = jnp.zeros_like(carry)

        x = z_ref[...] + b_ref[...]
        lf = jnp.minimum(x, 0.0) - jnp.log(1.0 + jnp.exp(-jnp.abs(x)))
        col = i * CHUNK + lax.broadcasted_iota(jnp.int32, (1, CHUNK), 1)
        lf = jnp.where(col >= PAD_ROWS, lf, 0.0)
        run = _apply01(_tri(False), lf, 3, left=False) + carry[...]
        f_ref[...] = run
        carry[...] = jnp.broadcast_to(run[:, CHUNK - 1:CHUNK], carry.shape)

    blk = pl.BlockSpec((h, CHUNK), lambda i: (0, i))
    return pl.pallas_call(
        body, name="logf_fwd",
        out_shape=jax.ShapeDtypeStruct((h, rows), F32),
        grid=(n,),
        in_specs=[blk, pl.BlockSpec((h, CHUNK), lambda i: (0, 0))],
        out_specs=blk,
        scratch_shapes=[pltpu.VMEM((h, CHUNK), F32)],
        compiler_params=_params(("arbitrary",)),
    )(z, b)


def _logf_bwd(df, z, b):
    h, rows = z.shape
    n = rows // CHUNK

    def body(df_ref, z_ref, b_ref, dz_ref, db_ref, carry):
        i = pl.program_id(0)

        @pl.when(i == 0)
        def _():
            carry[...] = jnp.zeros_like(carry)
            db_ref[...] = jnp.zeros_like(db_ref)

        run = _apply01(_tri(True), df_ref[...], 3, left=False) + carry[...]
        carry[...] = jnp.broadcast_to(run[:, 0:1], carry.shape)
        x = z_ref[...] + b_ref[...]
        col = (n - 1 - i) * CHUNK + lax.broadcasted_iota(jnp.int32, (1, CHUNK), 1)
        dz = jnp.where(col >= PAD_ROWS, run * (1.0 - jax.nn.sigmoid(x)), 0.0)
        dz_ref[...] = dz
        db_ref[...] += dz

    rev = pl.BlockSpec((h, CHUNK), lambda i: (0, n - 1 - i))
    fix = pl.BlockSpec((h, CHUNK), lambda i: (0, 0))
    dz, db = pl.pallas_call(
        body, name="logf_bwd",
        out_shape=(jax.ShapeDtypeStruct((h, rows), F32), jax.ShapeDtypeStruct((h, CHUNK), F32)),
        grid=(n,),
        in_specs=[rev, rev, fix],
        out_specs=(rev, fix),
        scratch_shapes=[pltpu.VMEM((h, CHUNK), F32)],
        compiler_params=_params(("arbitrary",)),
    )(df, z, b)
    return dz, db.sum(axis=1)


def _lane_lo():
    return lax.broadcasted_iota(jnp.int32, (1, CHUNK), 1) < HEAD_DIM


def _attn_block(rows):
    return _row_tile(rows, min(640, rows // 2))


def _masked_logits(s, i, j, blk):
    row = i * blk + lax.broadcasted_iota(jnp.int32, (blk, 1), 0)
    col = j * blk + lax.broadcasted_iota(jnp.int32, (1, blk), 1)
    return jnp.where(col <= row, jnp.where(col >= PAD_ROWS, s, NEG_INF), NEG_INF)


def _attn_prep(qkv, f_t, d):
    rows = qkv.shape[0]
    heads = d // HEAD_DIM
    hp = heads // 2
    tm = _row_tile(rows, 640)
    scale = 1.0 / math.sqrt(HEAD_DIM)

    def body(q_ref, k_ref, v_ref, f_ref, qa_ref, ka_ref, vo_ref, va_ref):
        pr = pl.program_id(1)
        lane = lax.broadcasted_iota(jnp.int32, (1, CHUNK), 1)
        lo = lane < HEAD_DIM
        q2 = (q_ref[...].astype(F32) * scale).astype(BF16)
        k2 = k_ref[...]
        v2 = v_ref[...]
        zero = jnp.zeros_like(k2)
        head_id = lax.broadcasted_iota(jnp.int32, (1, heads), 1)
        ft = f_ref[...]
        for hh in range(2):
            base = HEAD_DIM if hh == 0 else 0
            neg_f = -jnp.sum(jnp.where(head_id == 2 * pr + hh, ft, 0.0), axis=1, keepdims=True)
            aug = zero
            ones = zero
            for t, part in enumerate(_split_bf16(neg_f, 3)):
                aug = jnp.where(lane == base + t, part, aug)
                ones = jnp.where(lane == base + t, jnp.ones_like(zero), ones)
            own = lo if hh == 0 else jnp.logical_not(lo)
            sl = slice(hh * CHUNK, (hh + 1) * CHUNK)
            qa_ref[:, sl] = jnp.where(own, q2, ones)
            ka_ref[:, sl] = jnp.where(own, k2, aug)
            vo_ref[:, sl] = jnp.where(own, v2, zero)
            va_ref[:, sl] = jnp.where(own, v2, jnp.where(lane == base, jnp.ones_like(zero), zero))

    pair_in = lambda c: pl.BlockSpec((tm, CHUNK), lambda i, p: (i, c * hp + p))
    pair_out = pl.BlockSpec((tm, 2 * CHUNK), lambda i, p: (i, p))
    sds = jax.ShapeDtypeStruct((rows, 2 * d), BF16)
    return pl.pallas_call(
        body, name="attn_prep",
        out_shape=(sds, sds, sds, sds),
        grid=(rows // tm, hp),
        in_specs=[pair_in(0), pair_in(1), pair_in(2), pl.BlockSpec((tm, heads), lambda i, p: (i, 0))],
        out_specs=(pair_out, pair_out, pair_out, pair_out),
        compiler_params=_params(("parallel", "parallel")),
    )(qkv, qkv, qkv, f_t)


def _attn_fwd(q_aug, k_aug, v_aug, d):
    rows = q_aug.shape[0]
    hp = d // CHUNK
    blk = _attn_block(rows)
    nb = rows // blk
    nt = (((1,), (1,)), ((), ()))
    den_lane = (HEAD_DIM, 0)

    def body(q_ref, k_ref, v_ref, o_ref, lse_ref, m_s, acc_s):
        i = pl.program_id(1)
        j = pl.program_id(2)

        @pl.when(j == 0)
        def _():
            m_s[...] = jnp.full(m_s.shape, NEG_INF, F32)
            acc_s[...] = jnp.zeros_like(acc_s)

        def step(masked):
            pair = [slice(hh * CHUNK, (hh + 1) * CHUNK) for hh in range(2)]
            scores = [lax.dot_general(q_ref[:, ln], k_ref[:, ln], nt, preferred_element_type=F32) for ln in pair]
            for hh in range(2):
                lanes = pair[hh]
                s = scores[hh]
                if masked:
                    s = _masked_logits(s, i, j, blk)
                m_prev = m_s[hh]
                m_new = jnp.maximum(m_prev, s.max(axis=-1, keepdims=True))
                p = jnp.exp(s - m_new)
                m_s[hh] = m_new
                pv = jnp.dot(p.astype(BF16), v_ref[:, lanes], preferred_element_type=F32)
                acc_s[hh] = acc_s[hh] * jnp.exp(m_prev - m_new) + pv

        edge = (j == i) | (j == 0)

        @pl.when(edge & (j <= i))
        def _():
            step(True)

        @pl.when(jnp.logical_not(edge) & (j <= i))
        def _():
            step(False)

        @pl.when(j == i)
        def _():
            row = i * blk + lax.broadcasted_iota(jnp.int32, (blk, 1), 0)
            lo = _lane_lo()
            acc = [acc_s[hh] for hh in range(2)]
            den = [acc[hh][:, den_lane[hh]:den_lane[hh] + 1] for hh in range(2)]
            o = jnp.where(lo, acc[0] * (1.0 / den[0]), acc[1] * (1.0 / den[1]))
            o_ref[...] = jnp.where(row >= PAD_ROWS, o, 0.0)
            lse_ref[...] = jnp.where(lo, m_s[0] + jnp.log(den[0]), m_s[1] + jnp.log(den[1]))

    q_spec = pl.BlockSpec((blk, 2 * CHUNK), lambda h, i, j: (i, h))
    k_spec = pl.BlockSpec((blk, 2 * CHUNK), lambda h, i, j: (jnp.minimum(j, i), h))
    o_spec = pl.BlockSpec((blk, CHUNK), lambda h, i, j: (i, h))
    return pl.pallas_call(
        body, name="attn_fwd",
        out_shape=(jax.ShapeDtypeStruct((rows, d), F32), jax.ShapeDtypeStruct((rows, d), F32)),
        grid=(hp, nb, nb),
        in_specs=[q_spec, k_spec, k_spec],
        out_specs=(o_spec, o_spec),
        scratch_shapes=[pltpu.VMEM((2, blk, 1), F32), pltpu.VMEM((2, blk, CHUNK), F32)],
        compiler_params=_params(("parallel", "parallel", "arbitrary")),
    )(q_aug, k_aug, v_aug)


def _attn_bwd(q_aug, k_aug, v_own, do, lse, delta, d):
    rows = q_aug.shape[0]
    hp = d // CHUNK
    blk = _attn_block(rows)
    nb = rows // blk
    scale = 1.0 / math.sqrt(HEAD_DIM)
    nt = (((1,), (1,)), ((), ()))
    tn = (((0,), (0,)), ((), ()))

    def body(q_ref, k_ref, v_ref, do_ref, lse_ref, dl_ref, dq_ref, dk_ref, dv_ref, df_ref, rs_ref,
             dq_s, dk_s, dv_s, df_s, rs_s):
        j = pl.program_id(1)
        i = pl.program_id(2)
        lo = _lane_lo()

        @pl.when((j == 0) & (i == 0))
        def _():
            dq_s[...] = jnp.zeros_like(dq_s)
            rs_s[...] = jnp.zeros_like(rs_s)

        @pl.when(i == j)
        def _():
            dk_s[...] = jnp.zeros_like(dk_s)
            dv_s[...] = jnp.zeros_like(dv_s)
            df_s[...] = jnp.zeros_like(df_s)

        def step(masked):
            dov = do_ref[...]
            t_dq, t_dk, t_dv, row_sums = [], [], [], []
            for hh in range(2):
                lanes = slice(hh * CHUNK, (hh + 1) * CHUNK)
                qh, kh = q_ref[:, lanes], k_ref[:, lanes]
                off = hh * HEAD_DIM
                s = lax.dot_general(qh, kh, nt, preferred_element_type=F32)
                if masked:
                    s = _masked_logits(s, i, j, blk)
                p = jnp.exp(s - lse_ref[:, off:off + 1])
                dp = lax.dot_general(dov, v_ref[:, lanes], nt, preferred_element_type=F32)
                ds = p * (dp - dl_ref[:, off:off + 1])
                df_s[hh:hh + 1, :] += ds.sum(axis=0, keepdims=True)
                row_sums.append(ds.sum(axis=1, keepdims=True))
                pb = p.astype(BF16)
                dsb = ds.astype(BF16)
                t_dv.append(lax.dot_general(pb, dov, tn, preferred_element_type=F32))
                t_dk.append(lax.dot_general(dsb, qh, tn, preferred_element_type=F32))
                t_dq.append(jnp.dot(dsb, kh, preferred_element_type=F32))
            dv_s[...] += jnp.where(lo, t_dv[0], t_dv[1])
            dk_s[...] += jnp.where(lo, t_dk[0], t_dk[1])
            r0 = pl.multiple_of(i * blk, blk)
            dq_s[pl.ds(r0, blk), :] += jnp.where(lo, t_dq[0], t_dq[1])
            rs_s[pl.ds(r0, blk), :] += jnp.where(lo, row_sums[0], row_sums[1])

        edge = (j == i) | (j == 0)

        @pl.when(edge & (i >= j))
        def _():
            step(True)

        @pl.when(jnp.logical_not(edge) & (i >= j))
        def _():
            step(False)

        @pl.when(i == nb - 1)
        def _():
            dk_ref[...] = dk_s[...].astype(BF16)
            dv_ref[...] = dv_s[...].astype(BF16)
            df_ref[...] = -df_s[...]

        @pl.when((i == nb - 1) & (j == nb - 1))
        def _():
            dq_ref[...] = (dq_s[...] * scale).astype(BF16)
            rs_ref[...] = rs_s[...]

    qi = lambda h, j, i: (jnp.maximum(i, j), h)
    q_spec = pl.BlockSpec((blk, 2 * CHUNK), qi)
    kv_spec = pl.BlockSpec((blk, 2 * CHUNK), lambda h, j, i: (j, h))
    row_spec = pl.BlockSpec((blk, CHUNK), qi)
    kv_out = pl.BlockSpec((blk, CHUNK), lambda h, j, i: (j, h))
    dq_out = pl.BlockSpec((rows, CHUNK), lambda h, j, i: (0, h))
    return pl.pallas_call(
        body, name="attn_bwd",
        out_shape=(jax.ShapeDtypeStruct((rows, d), BF16), jax.ShapeDtypeStruct((rows, d), BF16),
                   jax.ShapeDtypeStruct((rows, d), BF16), jax.ShapeDtypeStruct((hp, 2, rows), F32),
                   jax.ShapeDtypeStruct((rows, d), F32)),
        grid=(hp, nb, nb),
        in_specs=[q_spec, kv_spec, kv_spec, row_spec, row_spec, row_spec],
        out_specs=(dq_out, kv_out, kv_out, pl.BlockSpec((None, 2, blk), lambda h, j, i: (h, 0, j)), dq_out),
        scratch_shapes=[pltpu.VMEM((rows, CHUNK), F32), pltpu.VMEM((blk, CHUNK), F32),
                        pltpu.VMEM((blk, CHUNK), F32), pltpu.VMEM((2, blk), F32), pltpu.VMEM((rows, CHUNK), F32)],
        compiler_params=_params(("parallel", "arbitrary", "arbitrary")),
    )(q_aug, k_aug, v_own, do, lse, delta)


def _band(w, transposed, other):
    r = lax.broadcasted_iota(jnp.int32, (CHUNK, CHUNK), 0)
    c = lax.broadcasted_iota(jnp.int32, (CHUNK, CHUNK), 1)
    dist = (c - r) if transposed else (r - c)
    if other:
        dist = dist + CHUNK
    return jnp.where(dist >= 0, jnp.where(dist < w, 1.0, 0.0), 0.0).astype(BF16)


def _inv_count(chunk_index, w):
    row = chunk_index * CHUNK + lax.broadcasted_iota(jnp.int32, (CHUNK, 1), 0)
    cnt = jnp.clip(row - PAD_ROWS + 1, 1, w).astype(F32)
    return 1.0 / cnt


def _pool_diff(u_cur, u_prev, i, w):
    ws = _apply01(_band(w, False, False), u_cur, 3)
    ws = ws + jnp.where(i > 0, _apply01(_band(w, False, True), u_prev, 3), 0.0)
    return ws * _inv_count(i, w) - u_cur


def _pool_merge_fwd(rest, o, w_pool, scale, d):
    rows = rest.shape[0]
    n = rows // CHUNK
    cg = d // len(POOL_WINDOWS)

    def body(up_ref, uc_ref, gp_ref, ga_ref, o_ref, wp_ref, sc_ref, mg_ref, yp_ref):
        i = pl.program_id(0)
        for g, w in enumerate(POOL_WINDOWS):
            sl = slice(g * cg, (g + 1) * cg)
            diff = _pool_diff(uc_ref[:, sl], up_ref[:, sl], i, w)
            ypre = jnp.dot(diff.astype(BF16), wp_ref[g], preferred_element_type=F32)
            yp_ref[:, sl] = ypre
            merged = (jax.nn.sigmoid(gp_ref[:, sl]) * (ypre * sc_ref[:, sl])
                      + jax.nn.sigmoid(ga_ref[:, sl]) * o_ref[:, sl])
            mg_ref[:, sl] = merged.astype(BF16)

    col = lambda c: pl.BlockSpec((CHUNK, d), lambda i: (i, c))
    return pl.pallas_call(
        body, name="pool_merge_fwd",
        out_shape=(jax.ShapeDtypeStruct((rows, d), BF16), jax.ShapeDtypeStruct((rows, d), F32)),
        grid=(n,),
        in_specs=[pl.BlockSpec((CHUNK, d), lambda i: (jnp.maximum(i - 1, 0), 0)), col(0), col(1), col(2), col(0),
                  pl.BlockSpec((len(POOL_WINDOWS), cg, cg), lambda i: (0, 0, 0)),
                  pl.BlockSpec((1, d), lambda i: (0, 0))],
        out_specs=(col(0), col(0)),
        compiler_params=_params(("parallel",)),
    )(rest, rest, rest, rest, o, w_pool, scale.reshape(1, d))


def _gate_bwd(dm, rest, o, ypre, scale, d):
    rows = rest.shape[0]
    n = rows // CHUNK

    def body(dm_ref, gp_ref, ga_ref, o_ref, yp_ref, sc_ref, dgp_ref, dga_ref, do_ref, dl_ref, dy_ref, ds_ref):
        @pl.when(pl.program_id(0) == 0)
        def _():
            ds_ref[...] = jnp.zeros_like(ds_ref)

        dmv = dm_ref[...]
        sp = jax.nn.sigmoid(gp_ref[...])
        sa = jax.nn.sigmoid(ga_ref[...])
        ov = o_ref[...]
        ypre_v = yp_ref[...]
        sc = sc_ref[...]
        dgp_ref[...] = (dmv * (ypre_v * sc) * (sp * (1.0 - sp))).astype(BF16)
        dga_ref[...] = (dmv * ov * (sa * (1.0 - sa))).astype(BF16)
        t = dmv * sp
        dy_ref[...] = (t * sc).astype(BF16)
        ds_ref[...] += _fold8(t * ypre_v)
        dob = (dmv * sa).astype(BF16)
        do_ref[...] = dob
        prod = dob.astype(F32) * ov
        lo = _lane_lo()
        for pr in range(d // CHUNK):
            sl = slice(pr * CHUNK, (pr + 1) * CHUNK)
            tp = prod[:, sl]
            s_lo = jnp.where(lo, tp, 0.0).sum(axis=-1, keepdims=True)
            s_hi = jnp.where(lo, 0.0, tp).sum(axis=-1, keepdims=True)
            dl_ref[:, sl] = jnp.where(lo, s_lo, s_hi)

    col = lambda c: pl.BlockSpec((CHUNK, d), lambda i: (i, c))
    row_bf = jax.ShapeDtypeStruct((rows, d), BF16)
    outs = pl.pallas_call(
        body, name="gate_bwd",
        out_shape=(row_bf, row_bf, row_bf, jax.ShapeDtypeStruct((rows, d), F32), row_bf,
                   jax.ShapeDtypeStruct((8, d), F32)),
        grid=(n,),
        in_specs=[col(0), col(1), col(2), col(0), col(0), pl.BlockSpec((1, d), lambda i: (0, 0))],
        out_specs=(col(0), col(0), col(0), col(0), col(0), pl.BlockSpec((8, d), lambda i: (0, 0))),
        compiler_params=_params(("arbitrary",)),
    )(dm, rest, rest, o, ypre, scale.reshape(1, d))
    return outs[:5] + (outs[5].sum(axis=0),)


def _pool_bwd(dypre, rest, w_pool, d):
    rows = rest.shape[0]
    n = rows // CHUNK
    ng = len(POOL_WINDOWS)
    cg = d // ng
    nt = (((1,), (1,)), ((), ()))
    tn = (((0,), (0,)), ((), ()))

    def body(dc_ref, dn_ref, up_ref, uc_ref, wp_ref, du_ref, dw_ref):
        i = pl.program_id(0)

        @pl.when(i == 0)
        def _():
            dw_ref[...] = jnp.zeros_like(dw_ref)

        row = i * CHUNK + lax.broadcasted_iota(jnp.int32, (CHUNK, 1), 0)
        for g, w in enumerate(POOL_WINDOWS):
            sl = slice(g * cg, (g + 1) * cg)
            diff = _pool_diff(uc_ref[:, sl], up_ref[:, sl], i, w)
            dyc = dc_ref[:, sl]
            dw_ref[g] += lax.dot_general(diff.astype(BF16), dyc, tn, preferred_element_type=F32)
            wg = wp_ref[g]
            dd_cur = lax.dot_general(dyc, wg, nt, preferred_element_type=F32)
            dd_next = lax.dot_general(dn_ref[:, sl], wg, nt, preferred_element_type=F32)
            du = _apply01(_band(w, True, False), dd_cur * _inv_count(i, w), 2)
            du = du + jnp.where(i < n - 1, _apply01(_band(w, True, True), dd_next * _inv_count(i + 1, w), 2), 0.0)
            du = du - dd_cur
            du_ref[:, sl] = jnp.where(row >= PAD_ROWS, du, 0.0).astype(BF16)

    cur = pl.BlockSpec((CHUNK, d), lambda i: (i, 0))
    return pl.pallas_call(
        body, name="pool_bwd",
        out_shape=(jax.ShapeDtypeStruct((rows, d), BF16), jax.ShapeDtypeStruct((ng, cg, cg), F32)),
        grid=(n,),
        in_specs=[cur, pl.BlockSpec((CHUNK, d), lambda i: (jnp.minimum(i + 1, n - 1), 0)),
                  pl.BlockSpec((CHUNK, d), lambda i: (jnp.maximum(i - 1, 0), 0)), cur,
                  pl.BlockSpec((ng, cg, cg), lambda i: (0, 0, 0))],
        out_specs=(cur, pl.BlockSpec((ng, cg, cg), lambda i: (0, 0, 0))),
        compiler_params=_params(("arbitrary",)),
    )(dypre, dypre, rest, rest, w_pool)


def _loss_grad(h_res, target):
    rows, d = h_res.shape
    n = rows // CHUNK

    def body(h_ref, t_ref, dh_ref, acc_ref):
        i = pl.program_id(0)

        @pl.when(i == 0)
        def _():
            acc_ref[...] = jnp.zeros_like(acc_ref)
            dh_ref[...] = jnp.zeros_like(dh_ref)

        @pl.when(i > 0)
        def _():
            err = h_ref[...] - t_ref[...]
            dh_ref[...] = err * (1.0 / d)
            e2 = _fold8(err * err)
            part = e2[:, 0:CHUNK]
            for c in range(1, d // CHUNK):
                part = part + e2[:, c * CHUNK:(c + 1) * CHUNK]
            acc_ref[...] += part

    dh, acc = pl.pallas_call(
        body, name="loss_grad",
        out_shape=(jax.ShapeDtypeStruct((rows, d), F32), jax.ShapeDtypeStruct((8, CHUNK), F32)),
        grid=(n,),
        in_specs=[pl.BlockSpec((CHUNK, d), lambda i: (i, 0)),
                  pl.BlockSpec((CHUNK, d), lambda i: (jnp.maximum(i - 1, 0), 0))],
        out_specs=(pl.BlockSpec((CHUNK, d), lambda i: (i, 0)), pl.BlockSpec((8, CHUNK), lambda i: (0, 0))),
        compiler_params=_params(("arbitrary",)),
    )(h_res, target)
    return dh, (0.5 / d) * acc.sum()


def _adamw(slots, w, m, v, name):
    rows, cols = w.shape
    lanes = -(-cols // CHUNK) * CHUNK
    row_bytes = lanes * (N_DEV * slots.dtype.itemsize + 7 * 4)
    tr = max(t for t in _divisors(rows, 8) if t <= max(8, ADAM_TILE_BYTES // row_bytes))
    c1 = 1.0 - ADAM_B1 ** ADAM_STEP
    c2 = 1.0 - ADAM_B2 ** ADAM_STEP

    def body(s_ref, w_ref, m_ref, v_ref, g_ref, d_ref, mo_ref, vo_ref):
        g = s_ref[0].astype(F32)
        for k in range(1, N_DEV):
            g = g + s_ref[k].astype(F32)
        m_new = ADAM_B1 * m_ref[...] + (1.0 - ADAM_B1) * g
        v_new = ADAM_B2 * v_ref[...] + (1.0 - ADAM_B2) * (g * g)
        m_hat = m_new / c1
        v_hat = v_new / c2
        g_ref[...] = g
        d_ref[...] = -ADAM_LR * (m_hat / (jnp.sqrt(v_hat) + ADAM_EPS) + ADAM_WD * w_ref[...])
        mo_ref[...] = m_new
        vo_ref[...] = v_new

    tile = pl.BlockSpec((tr, cols), lambda i: (i, 0))
    sds = jax.ShapeDtypeStruct((rows, cols), F32)
    return pl.pallas_call(
        body, name=name,
        out_shape=(sds, sds, sds, sds),
        grid=(rows // tr,),
        in_specs=[pl.BlockSpec((N_DEV, tr, cols), lambda i: (0, i, 0)), tile, tile, tile],
        out_specs=(tile, tile, tile, tile),
        compiler_params=_params(("parallel",)),
    )(slots, w, m, v)


def _exchange(srcs, gather, name):
    n = len(srcs)
    shapes = [((N_DEV,) + s.shape) if gather else s.shape for s in srcs]

    def body(*refs):
        src_refs, out_refs = refs[:n], refs[n:2 * n]
        send_sems, recv_sems, local_sems = refs[2 * n:]
        x, y, c = lax.axis_index("x"), lax.axis_index("y"), lax.axis_index("c")
        me = 4 * x + 2 * y + c

        def payload(a, slot):
            return src_refs[a] if gather else src_refs[a].at[slot]

        own = [pltpu.make_async_copy(payload(a, me), out_refs[a].at[me], local_sems.at[a]) for a in range(n)]
        for cp in own:
            cp.start()
        sends, recvs = [], []
        for k in range(1, N_DEV):
            px = 1 - x if k & 4 else x
            py = 1 - y if k & 2 else y
            pc = 1 - c if k & 1 else c
            peer = 4 * px + 2 * py + pc
            for a in range(n):
                sems = dict(send_sem=send_sems.at[(k - 1) * n + a], recv_sem=recv_sems.at[(k - 1) * n + a],
                            device_id=(px, py, pc), device_id_type=MESH_ID)
                sends.append(pltpu.make_async_remote_copy(src_ref=payload(a, peer), dst_ref=out_refs[a].at[me], **sems))
                recvs.append(pltpu.make_async_remote_copy(src_ref=payload(a, peer), dst_ref=out_refs[a].at[peer], **sems))
        for cp in sends:
            cp.start()
        for cp in recvs:
            cp.wait_recv()
        for cp in sends:
            cp.wait_send()
        for cp in own:
            cp.wait()

    return pl.pallas_call(
        body, name=name,
        out_shape=tuple(jax.ShapeDtypeStruct(sh, s.dtype) for sh, s in zip(shapes, srcs)),
        in_specs=[pl.BlockSpec(memory_space=pl.ANY)] * n,
        out_specs=tuple([pl.BlockSpec(memory_space=pl.ANY)] * n),
        scratch_shapes=[pltpu.SemaphoreType.DMA(((N_DEV - 1) * n,)), pltpu.SemaphoreType.DMA(((N_DEV - 1) * n,)),
                        pltpu.SemaphoreType.DMA((n,))],
    )(*srcs)


BIG = ("w_in", "w_pool", "w_out", "w_gate", "w_up", "w_down")
SHARD_AXIS = dict(w_in=-1, w_pool=-2, w_out=-2, w_gate=-1, w_up=-1, w_down=-2)
SMALL = ("norm_mix_pre", "norm_mix_post", "norm_ffn_pre", "norm_ffn_post", "pool_scale")


def _join(g, axis):
    return jnp.concatenate([g[j] for j in range(N_DEV)], axis=axis)


def _split(full, axis):
    return jnp.stack(jnp.split(full, N_DEV, axis=axis))


def _rows2d(a):
    return a.reshape(-1, a.shape[-1])


def kernel(x, meta_tokens, norm_mix_pre, norm_mix_post, norm_ffn_pre, norm_ffn_post, w_in, b_forget, w_pool, pool_scale, w_out, w_ffn_gate, w_ffn_up, w_ffn_down, loss_target, m_meta_tokens, m_norm_mix_pre, m_norm_mix_post, m_norm_ffn_pre, m_norm_ffn_post, m_w_in, m_b_forget, m_w_pool, m_pool_scale, m_w_out, m_w_ffn_gate, m_w_ffn_up, m_w_ffn_down, v_meta_tokens, v_norm_mix_pre, v_norm_mix_post, v_norm_ffn_pre, v_norm_ffn_post, v_w_in, v_b_forget, v_w_pool, v_pool_scale, v_w_out, v_w_ffn_gate, v_w_ffn_up, v_w_ffn_down):
    x2 = x[0]
    target = loss_target[0]
    seq, d = x2.shape
    depth = w_in.shape[0]
    heads = d // HEAD_DIM
    ff = w_ffn_gate.shape[2] * N_DEV
    rows = PAD_ROWS + META_TOKENS + seq
    assert seq % CHUNK == 0 and d % (2 * CHUNK) == 0 and heads <= FORGET_PAD
    me = 4 * lax.axis_index("x") + 2 * lax.axis_index("y") + lax.axis_index("c")

    big = dict(w_in=w_in, w_pool=w_pool, w_out=w_out, w_gate=w_ffn_gate, w_up=w_ffn_up, w_down=w_ffn_down)
    big_m = dict(w_in=m_w_in, w_pool=m_w_pool, w_out=m_w_out, w_gate=m_w_ffn_gate, w_up=m_w_ffn_up, w_down=m_w_ffn_down)
    big_v = dict(w_in=v_w_in, w_pool=v_w_pool, w_out=v_w_out, w_gate=v_w_ffn_gate, w_up=v_w_ffn_up, w_down=v_w_ffn_down)
    small = dict(norm_mix_pre=norm_mix_pre, norm_mix_post=norm_mix_post, norm_ffn_pre=norm_ffn_pre,
                 norm_ffn_post=norm_ffn_post, pool_scale=pool_scale)
    small_m = dict(norm_mix_pre=m_norm_mix_pre, norm_mix_post=m_norm_mix_post, norm_ffn_pre=m_norm_ffn_pre,
                   norm_ffn_post=m_norm_ffn_post, pool_scale=m_pool_scale)
    small_v = dict(norm_mix_pre=v_norm_mix_pre, norm_mix_post=v_norm_mix_post, norm_ffn_pre=v_norm_ffn_pre,
                   norm_ffn_post=v_norm_ffn_post, pool_scale=v_pool_scale)

    gathered = _exchange([big[n].astype(BF16) for n in BIG] + [meta_tokens], True, "gather_weights")
    full = {n: _join(g, SHARD_AXIS[n]) for n, g in zip(BIG, gathered)}
    meta_full = _join(gathered[-1], -1)

    win = full["w_in"]
    fcol = 4 * d
    w_qkv = win[:, :, d:4 * d]
    w_rest = jnp.concatenate([win[:, :, :d], win[:, :, fcol + heads:], win[:, :, fcol:fcol + heads],
                              jnp.zeros((depth, d, FORGET_PAD - heads), BF16)], axis=2)
    w_cat = jnp.concatenate([w_qkv, w_rest], axis=2)
    w_gu = jnp.concatenate([full["w_gate"], full["w_up"]], axis=2)
    w_o, w_dn, w_pl = full["w_out"], full["w_down"], full["w_pool"]

    h_res = jnp.concatenate([jnp.zeros((PAD_ROWS, d), F32), meta_full, x2], axis=0)
    h1 = _norm_fwd(h_res, norm_mix_pre[0])
    ones = jnp.ones((d,), F32)
    saved = []
    for l in range(depth):
        qkv = _matmul(h1, w_qkv[l], "nn", BF16, "proj_qkv")
        rest = _matmul(h1, w_rest[l], "nn", F32, "proj_rest")
        z = rest[:, 3 * d:3 * d + heads].T
        bias = jnp.broadcast_to(b_forget[l][:, None], (heads, CHUNK))
        q_aug, k_aug, v_own, v_aug = _attn_prep(qkv, _logf_fwd(z, bias).T, d)
        o, lse = _attn_fwd(q_aug, k_aug, v_aug, d)
        merged, ypre = _pool_merge_fwd(rest, o, w_pl[l], pool_scale[l], d)
        mix = _matmul(merged, w_o[l], "nn", F32, "mix_out")
        h_mid, h2 = _resid_norm_fwd(h_res, mix, norm_mix_post[l], norm_ffn_pre[l])
        ab = _matmul(h2, w_gu[l], "nn", F32, "ffn_in")
        act = _swiglu_fwd(ab)
        ffo = _matmul(act, w_dn[l], "nn", F32, "ffn_out")
        g_next = norm_mix_pre[l + 1] if l + 1 < depth else ones
        h_next, h1_next = _resid_norm_fwd(h_mid, ffo, norm_ffn_post[l], g_next)
        saved.append(dict(h_in=h_res, h1=h1, q_aug=q_aug, k_aug=k_aug, v_own=v_own, rest=rest, z=z,
                          bias=bias, o=o, lse=lse,
                          merged=merged, ypre=ypre, mix=mix, h_mid=h_mid, h2=h2, ab=ab, act=act, ffo=ffo))
        h_res, h1 = h_next, h1_next

    dh, loss_local = _loss_grad(h_res, target)
    loss = lax.psum(loss_local, ("x", "y", "c"))

    grads = {n: [None] * depth for n in BIG + SMALL + ("b_forget",)}
    for l in reversed(range(depth)):
        s = saved[l]
        dffo, grads["norm_ffn_post"][l] = _norm_bwd(s["ffo"], norm_ffn_post[l], dh, None, BF16)
        grads["w_down"][l] = _matmul(s["act"], dffo, "tn", F32, "grad_w_down")
        dact = _matmul(dffo, w_dn[l], "nt", F32, "ffn_out_dx")
        da, db = _swiglu_bwd(s["ab"], dact)
        dab = jnp.concatenate([da, db], axis=1)
        dgu = _matmul(s["h2"], dab, "tn", F32, "grad_w_gu")
        grads["w_gate"][l], grads["w_up"][l] = dgu[:, :ff], dgu[:, ff:]
        dh2 = _matmul(dab, w_gu[l], "nt", F32, "ffn_in_dx")
        dh_mid, grads["norm_ffn_pre"][l] = _norm_bwd(s["h_mid"], norm_ffn_pre[l], dh2, dh, F32)

        dmix, grads["norm_mix_post"][l] = _norm_bwd(s["mix"], norm_mix_post[l], dh_mid, None, BF16)
        grads["w_out"][l] = _matmul(s["merged"], dmix, "tn", F32, "grad_w_out")
        dm = _matmul(dmix, w_o[l], "nt", F32, "mix_out_dx")
        dgp, dga, do, delta, dypre, grads["pool_scale"][l] = _gate_bwd(dm, s["rest"], s["o"], s["ypre"], pool_scale[l], d)
        du, grads["w_pool"][l] = _pool_bwd(dypre, s["rest"], w_pl[l], d)
        dq, dk, dv, df_key, df_query = _attn_bwd(s["q_aug"], s["k_aug"], s["v_own"], do, s["lse"], delta, d)
        df = df_key.reshape(heads, rows) + df_query.reshape(rows, heads, HEAD_DIM)[:, :, 0].T
        dz, grads["b_forget"][l] = _logf_bwd(df, s["z"], s["bias"])
        dzt = jnp.pad(dz.T.astype(BF16), ((0, 0), (0, FORGET_PAD - heads)))
        dproj = jnp.concatenate([dq, dk, dv, du, dgp, dga, dzt], axis=1)
        dwc = _matmul(s["h1"], dproj, "tn", F32, "grad_w_in")
        grads["w_in"][l] = jnp.concatenate([dwc[:, 3 * d:4 * d], dwc[:, :3 * d], dwc[:, 6 * d:6 * d + heads],
                                            dwc[:, 4 * d:6 * d]], axis=1)
        dh1 = _matmul(dproj, w_cat[l], "nt", F32, "proj_dx")
        dh, grads["norm_mix_pre"][l] = _norm_bwd(s["h_in"], norm_mix_pre[l], dh1, dh_mid, F32)

    grad_x = dh[PAD_ROWS + META_TOKENS:][None]
    dmeta = dh[PAD_ROWS:PAD_ROWS + META_TOKENS]

    slots = [_split(jnp.stack(grads[n]), SHARD_AXIS[n]).astype(BF16) for n in BIG]
    recv = _exchange([s_.reshape(N_DEV, -1, s_.shape[-1]) for s_ in slots], False, "scatter_grads")
    big_out = {}
    for n, r in zip(BIG, recv):
        outs = _adamw(r, _rows2d(big[n]), _rows2d(big_m[n]), _rows2d(big_v[n]), "adamw_" + n)
        big_out[n] = [o_.reshape(big[n].shape) for o_ in outs]

    def table(parts, forget):
        t = jnp.concatenate([_rows2d(p) for p in parts] + [jnp.pad(forget, ((0, 0), (0, d - heads)))], axis=0)
        return jnp.pad(t, ((0, -t.shape[0] % 8), (0, 0)))

    g_table = table([jnp.stack(grads[n]) for n in SMALL], jnp.stack(grads["b_forget"]))
    rep_rows = g_table.shape[0]
    got = _exchange([jnp.concatenate([g_table, dmeta], axis=0)], True, "gather_small_grads")[0]
    outs = _adamw(got[:, :rep_rows], table([small[n] for n in SMALL], b_forget),
                  table([small_m[n] for n in SMALL], m_b_forget), table([small_v[n] for n in SMALL], v_b_forget),
                  "adamw_small")
    dcols = d // N_DEV
    meta_slots = lax.dynamic_slice_in_dim(got[:, rep_rows:], me * dcols, dcols, axis=2)
    meta_out = _adamw(meta_slots, meta_tokens, m_meta_tokens, v_meta_tokens, "adamw_meta")

    def ordered(k):
        t = outs[k]
        so = {n: t[a * depth:(a + 1) * depth] for a, n in enumerate(SMALL)}
        forget = t[len(SMALL) * depth:(len(SMALL) + 1) * depth, :heads]
        return (meta_out[k], so["norm_mix_pre"], so["norm_mix_post"], so["norm_ffn_pre"], so["norm_ffn_post"],
                big_out["w_in"][k], forget, big_out["w_pool"][k], so["pool_scale"], big_out["w_out"][k],
                big_out["w_gate"][k], big_out["w_up"][k], big_out["w_down"][k])

    return (loss, grad_x) + ordered(0) + ordered(1) + ordered(2) + ordered(3)
```

```python
import math

import jax
import jax.numpy as jnp
from jax import lax
from jax.experimental import pallas as pl
from jax.experimental.pallas import tpu as pltpu

F32 = jnp.float32
BF16 = jnp.bfloat16

N_DEV = 8
META_TOKENS = 16
PAD_ROWS = 112
CHUNK = 128
HEAD_DIM = 64
POOL_WINDOWS = (2, 4, 8, 16)
FORGET_PAD = 256
RMS_EPS = 1e-6
NEG_INF = -1e30
ADAM_LR, ADAM_B1, ADAM_B2, ADAM_EPS, ADAM_WD, ADAM_STEP = 0.001, 0.9, 0.999, 1e-08, 0.01, 10

VMEM_LIMIT = 56 * 1024 * 1024
VMEM_TILE_BUDGET = 36 * 1024 * 1024
ADAM_TILE_BYTES = 8 * 1024 * 1024
MESH_ID = pl.DeviceIdType.MESH


def _params(sem, vmem=VMEM_LIMIT):
    return pltpu.CompilerParams(dimension_semantics=sem, vmem_limit_bytes=vmem)


def _divisors(n, mult):
    return [d for d in range(mult, n + 1, mult) if n % d == 0]


def _row_tile(rows, cap):
    return max(d for d in _divisors(rows, CHUNK) if d <= max(cap, CHUNK))


def _fold8(x):
    r, c = x.shape
    return x.reshape(r // 8, 8, c).sum(axis=0)


def _split_bf16(x, parts):
    out = []
    for _ in range(parts - 1):
        hi = x.astype(BF16)
        out.append(hi)
        x = x - hi.astype(F32)
    out.append(x.astype(BF16))
    return out


def _apply01(mat, x, parts, left=True):
    acc = None
    for p in _split_bf16(x, parts):
        t = jnp.dot(mat, p, preferred_element_type=F32) if left else jnp.dot(p, mat, preferred_element_type=F32)
        acc = t if acc is None else acc + t
    return acc


def _matmul_tiles(m, n, k, mode, out_bytes):
    if mode == "tn":
        tk = _row_tile(k, 640)
    else:
        tk = max(d for d in _divisors(k, CHUNK) if d <= 1536)
    nk = k // tk
    best = None
    m_opts = _divisors(m, CHUNK)
    n_opts = _divisors(n, CHUNK)
    for tm in m_opts:
        for tn in n_opts:
            need = 2 * 2 * (tm * tk + tk * tn) + 2 * tm * tn * out_bytes
            if nk > 1 or mode == "tn":
                need += tm * tn * 4
            need += tm * tn * 4
            if need > VMEM_TILE_BUDGET:
                continue
            key = (tm * tn, tn)
            if best is None or key > best[0]:
                best = (key, tm, tn)
    return best[1], best[2], tk


def _matmul(a, b, mode, out_dtype, name):
    if mode == "nn":
        (m, k), (k2, n) = a.shape, b.shape
    elif mode == "nt":
        (m, k), (n, k2) = a.shape, b.shape
    else:
        (k, m), (k2, n) = a.shape, b.shape
    assert k == k2 and a.dtype == BF16 and b.dtype == BF16
    tm, tn, tk = _matmul_tiles(m, n, k, mode, jnp.dtype(out_dtype).itemsize)
    nk = k // tk
    if mode == "nn":
        a_spec = pl.BlockSpec((tm, tk), lambda i, j, r: (i, r))
        b_spec = pl.BlockSpec((tk, tn), lambda i, j, r: (r, j))
        dims = (((1,), (0,)), ((), ()))
    elif mode == "nt":
        a_spec = pl.BlockSpec((tm, tk), lambda i, j, r: (i, r))
        b_spec = pl.BlockSpec((tn, tk), lambda i, j, r: (j, r))
        dims = (((1,), (1,)), ((), ()))
    else:
        a_spec = pl.BlockSpec((tk, tm), lambda i, j, r: (r, i))
        b_spec = pl.BlockSpec((tk, tn), lambda i, j, r: (r, j))
        dims = (((0,), (0,)), ((), ()))

    def body(a_ref, b_ref, o_ref, *acc):
        part = lax.dot_general(a_ref[...], b_ref[...], dims, preferred_element_type=F32)
        if nk == 1:
            o_ref[...] = part.astype(o_ref.dtype)
        else:
            r = pl.program_id(2)

            @pl.when(r == 0)
            def _():
                acc[0][...] = part

            @pl.when(r > 0)
            def _():
                acc[0][...] += part

            @pl.when(r == nk - 1)
            def _():
                o_ref[...] = acc[0][...].astype(o_ref.dtype)

    return pl.pallas_call(
        body, name=name,
        out_shape=jax.ShapeDtypeStruct((m, n), out_dtype),
        grid=(m // tm, n // tn, nk),
        in_specs=[a_spec, b_spec],
        out_specs=pl.BlockSpec((tm, tn), lambda i, j, r: (i, j)),
        scratch_shapes=[pltpu.VMEM((tm, tn), F32)] if nk > 1 else [],
        compiler_params=_params(("parallel", "parallel", "arbitrary")),
    )(a, b)


def _rms(x, g):
    rstd = lax.rsqrt(jnp.mean(x * x, axis=-1, keepdims=True) + RMS_EPS)
    return x * rstd * g


def _norm_fwd(x, g):
    rows, d = x.shape
    tm = _row_tile(rows, 640)

    def body(x_ref, g_ref, h_ref):
        h_ref[...] = _rms(x_ref[...], g_ref[...]).astype(BF16)

    return pl.pallas_call(
        body, name="norm_fwd",
        out_shape=jax.ShapeDtypeStruct((rows, d), BF16),
        grid=(rows // tm,),
        in_specs=[pl.BlockSpec((tm, d), lambda i: (i, 0)), pl.BlockSpec((1, d), lambda i: (0, 0))],
        out_specs=pl.BlockSpec((tm, d), lambda i: (i, 0)),
        compiler_params=_params(("parallel",)),
    )(x, g.reshape(1, d))


def _resid_norm_fwd(h_res, y, g_post, g_next):
    rows, d = h_res.shape
    tm = _row_tile(rows, 640)

    def body(r_ref, y_ref, gp_ref, gn_ref, hn_ref, hx_ref):
        h_new = r_ref[...] + _rms(y_ref[...], gp_ref[...])
        hn_ref[...] = h_new
        hx_ref[...] = _rms(h_new, gn_ref[...]).astype(BF16)

    row = pl.BlockSpec((tm, d), lambda i: (i, 0))
    vec = pl.BlockSpec((1, d), lambda i: (0, 0))
    return pl.pallas_call(
        body, name="resid_norm_fwd",
        out_shape=(jax.ShapeDtypeStruct((rows, d), F32), jax.ShapeDtypeStruct((rows, d), BF16)),
        grid=(rows // tm,),
        in_specs=[row, row, vec, vec],
        out_specs=(row, row),
        compiler_params=_params(("parallel",)),
    )(h_res, y, g_post.reshape(1, d), g_next.reshape(1, d))


def _norm_bwd(x, g, dy, resid, out_dtype):
    rows, d = x.shape
    tm = _row_tile(rows, 640)
    has_resid = resid is not None

    def body(*refs):
        if has_resid:
            x_ref, g_ref, dy_ref, r_ref, dx_ref, dg_ref = refs
        else:
            x_ref, g_ref, dy_ref, dx_ref, dg_ref = refs
        xv = x_ref[...]
        dyv = dy_ref[...].astype(F32)
        rstd = lax.rsqrt(jnp.mean(xv * xv, axis=-1, keepdims=True) + RMS_EPS)
        xhat = xv * rstd
        gdy = dyv * g_ref[...]
        dx = rstd * (gdy - xhat * jnp.mean(gdy * xhat, axis=-1, keepdims=True))
        if has_resid:
            dx = dx + r_ref[...]
        dx_ref[...] = dx.astype(dx_ref.dtype)

        @pl.when(pl.program_id(0) == 0)
        def _():
            dg_ref[...] = jnp.zeros_like(dg_ref)

        dg_ref[...] += _fold8(dyv * xhat)

    row = pl.BlockSpec((tm, d), lambda i: (i, 0))
    vec = pl.BlockSpec((1, d), lambda i: (0, 0))
    args = [x, g.reshape(1, d), dy] + ([resid] if has_resid else [])
    dx, dg = pl.pallas_call(
        body, name="norm_bwd_resid" if has_resid else "norm_bwd",
        out_shape=(jax.ShapeDtypeStruct((rows, d), out_dtype), jax.ShapeDtypeStruct((8, d), F32)),
        grid=(rows // tm,),
        in_specs=[row, vec, row] + ([row] if has_resid else []),
        out_specs=(row, pl.BlockSpec((8, d), lambda i: (0, 0))),
        compiler_params=_params(("arbitrary",)),
    )(*args)
    return dx, dg.sum(axis=0)


def _swiglu_fwd(ab):
    rows, two_f = ab.shape
    f = two_f // 2
    tm = _row_tile(rows, 640)

    def body(a_ref, b_ref, o_ref):
        a = a_ref[...]
        o_ref[...] = (a * jax.nn.sigmoid(a) * b_ref[...]).astype(BF16)

    return pl.pallas_call(
        body, name="swiglu_fwd",
        out_shape=jax.ShapeDtypeStruct((rows, f), BF16),
        grid=(rows // tm,),
        in_specs=[pl.BlockSpec((tm, f), lambda i: (i, 0)), pl.BlockSpec((tm, f), lambda i: (i, 1))],
        out_specs=pl.BlockSpec((tm, f), lambda i: (i, 0)),
        compiler_params=_params(("parallel",)),
    )(ab, ab)


def _swiglu_bwd(ab, dff):
    rows, two_f = ab.shape
    f = two_f // 2
    tm = _row_tile(rows, 256)

    def body(a_ref, b_ref, d_ref, da_ref, db_ref):
        a = a_ref[...]
        d = d_ref[...]
        s = jax.nn.sigmoid(a)
        da_ref[...] = (d * b_ref[...] * (s * (1.0 + a * (1.0 - s)))).astype(BF16)
        db_ref[...] = (d * (a * s)).astype(BF16)

    lo = pl.BlockSpec((tm, f), lambda i: (i, 0))
    hi = pl.BlockSpec((tm, f), lambda i: (i, 1))
    return pl.pallas_call(
        body, name="swiglu_bwd",
        out_shape=(jax.ShapeDtypeStruct((rows, f), BF16), jax.ShapeDtypeStruct((rows, f), BF16)),
        grid=(rows // tm,),
        in_specs=[lo, hi, lo],
        out_specs=(lo, lo),
        compiler_params=_params(("parallel",)),
    )(ab, ab, dff)


def _tri(lower):
    r = lax.broadcasted_iota(jnp.int32, (CHUNK, CHUNK), 0)
    c = lax.broadcasted_iota(jnp.int32, (CHUNK, CHUNK), 1)
    return jnp.where((r >= c) if lower else (r <= c), 1.0, 0.0).astype(BF16)


def _logf_fwd(z, b):
    h, rows = z.shape
    n = rows // CHUNK

    def body(z_ref, b_ref, f_ref, carry):
        i = pl.program_id(0)

        @pl.when(i == 0)
        def _():
            carry[...] = jnp.zeros_like(carry)

        x = z_ref[...] + b_ref[...]
        lf = jnp.minimum(x, 0.0) - jnp.log(1.0 + jnp.exp(-jnp.abs(x)))
        col = i * CHUNK + lax.broadcasted_iota(jnp.int32, (1, CHUNK), 1)
        lf = jnp.where(col >= PAD_ROWS, lf, 0.0)
        run = _apply01(_tri(False), lf, 3, left=False) + carry[...]
        f_ref[...] = run
        carry[...] = jnp.broadcast_to(run[:, CHUNK - 1:CHUNK], carry.shape)

    blk = pl.BlockSpec((h, CHUNK), lambda i: (0, i))
    return pl.pallas_call(
        body, name="logf_fwd",
        out_shape=jax.ShapeDtypeStruct((h, rows), F32),
        grid=(n,),
        in_specs=[blk, pl.BlockSpec((h, CHUNK), lambda i: (0, 0))],
        out_specs=blk,
        scratch_shapes=[pltpu.VMEM((h, CHUNK), F32)],
        compiler_params=_params(("arbitrary",)),
    )(z, b)


def _logf_bwd(df, z, b):
    h, rows = z.shape
    n = rows // CHUNK

    def body(df_ref, z_ref, b_ref, dz_ref, db_ref, carry):
        i = pl.program_id(0)

        @pl.when(i == 0)
        def _():
            carry[...] = jnp.zeros_like(carry)
            db_ref[...] = jnp.zeros_like(db_ref)

        run = _apply01(_tri(True), df_ref[...], 3, left=False) + carry[...]
        carry[...] = jnp.broadcast_to(run[:, 0:1], carry.shape)
        x = z_ref[...] + b_ref[...]
        col = (n - 1 - i) * CHUNK + lax.broadcasted_iota(jnp.int32, (1, CHUNK), 1)
        dz = jnp.where(col >= PAD_ROWS, run * (1.0 - jax.nn.sigmoid(x)), 0.0)
        dz_ref[...] = dz
        db_ref[...] += dz

    rev = pl.BlockSpec((h, CHUNK), lambda i: (0, n - 1 - i))
    fix = pl.BlockSpec((h, CHUNK), lambda i: (0, 0))
    dz, db = pl.pallas_call(
        body, name="logf_bwd",
        out_shape=(jax.ShapeDtypeStruct((h, rows), F32), jax.ShapeDtypeStruct((h, CHUNK), F32)),
        grid=(n,),
        in_specs=[rev, rev, fix],
        out_specs=(rev, fix),
        scratch_shapes=[pltpu.VMEM((h, CHUNK), F32)],
        compiler_params=_params(("arbitrary",)),
    )(df, z, b)
    return dz, db.sum(axis=1)


def _lane_lo():
    return lax.broadcasted_iota(jnp.int32, (1, CHUNK), 1) < HEAD_DIM


def _attn_block(rows):
    return _row_tile(rows, min(640, rows // 2))


def _masked_logits(s, i, j, blk):
    row = i * blk + lax.broadcasted_iota(jnp.int32, (blk, 1), 0)
    col = j * blk + lax.broadcasted_iota(jnp.int32, (1, blk), 1)
    return jnp.where(col <= row, jnp.where(col >= PAD_ROWS, s, NEG_INF), NEG_INF)


def _attn_prep(qkv, f_t, d):
    rows = qkv.shape[0]
    heads = d // HEAD_DIM
    hp = heads // 2
    tm = _row_tile(rows, 640)
    scale = 1.0 / math.sqrt(HEAD_DIM)

    def body(q_ref, k_ref, v_ref, f_ref, qa_ref, ka_ref, vo_ref, va_ref):
        pr = pl.program_id(1)
        lane = lax.broadcasted_iota(jnp.int32, (1, CHUNK), 1)
        lo = lane < HEAD_DIM
        q2 = (q_ref[...].astype(F32) * scale).astype(BF16)
        k2 = k_ref[...]
        v2 = v_ref[...]
        zero = jnp.zeros_like(k2)
        head_id = lax.broadcasted_iota(jnp.int32, (1, heads), 1)
        ft = f_ref[...]
        for hh in range(2):
            base = HEAD_DIM if hh == 0 else 0
            neg_f = -jnp.sum(jnp.where(head_id == 2 * pr + hh, ft, 0.0), axis=1, keepdims=True)
            aug = zero
            ones = zero
            for t, part in enumerate(_split_bf16(neg_f, 3)):
                aug = jnp.where(lane == base + t, part, aug)
                ones = jnp.where(lane == base + t, jnp.ones_like(zero), ones)
            own = lo if hh == 0 else jnp.logical_not(lo)
            sl = slice(hh * CHUNK, (hh + 1) * CHUNK)
            qa_ref[:, sl] = jnp.where(own, q2, ones)
            ka_ref[:, sl] = jnp.where(own, k2, aug)
            vo_ref[:, sl] = jnp.where(own, v2, zero)
            va_ref[:, sl] = jnp.where(own, v2, jnp.where(lane == base, jnp.ones_like(zero), zero))

    pair_in = lambda c: pl.BlockSpec((tm, CHUNK), lambda i, p: (i, c * hp + p))
    pair_out = pl.BlockSpec((tm, 2 * CHUNK), lambda i, p: (i, p))
    sds = jax.ShapeDtypeStruct((rows, 2 * d), BF16)
    return pl.pallas_call(
        body, name="attn_prep",
        out_shape=(sds, sds, sds, sds),
        grid=(rows // tm, hp),
        in_specs=[pair_in(0), pair_in(1), pair_in(2), pl.BlockSpec((tm, heads), lambda i, p: (i, 0))],
        out_specs=(pair_out, pair_out, pair_out, pair_out),
        compiler_params=_params(("parallel", "parallel")),
    )(qkv, qkv, qkv, f_t)


def _attn_fwd(q_aug, k_aug, v_aug, d):
    rows = q_aug.shape[0]
    hp = d // CHUNK
    blk = _attn_block(rows)
    nb = rows // blk
    nt = (((1,), (1,)), ((), ()))
    den_lane = (HEAD_DIM, 0)

    def body(q_ref, k_ref, v_ref, o_ref, lse_ref, m_s, acc_s):
        i = pl.program_id(1)
        j = pl.program_id(2)

        @pl.when(j == 0)
        def _():
            m_s[...] = jnp.full(m_s.shape, NEG_INF, F32)
            acc_s[...] = jnp.zeros_like(acc_s)

        def step(masked):
            pair = [slice(hh * CHUNK, (hh + 1) * CHUNK) for hh in range(2)]
            scores = [lax.dot_general(q_ref[:, ln], k_ref[:, ln], nt, preferred_element_type=F32) for ln in pair]
            for hh in range(2):
                lanes = pair[hh]
                s = scores[hh]
                if masked:
                    s = _masked_logits(s, i, j, blk)
                m_prev = m_s[hh]
                m_new = jnp.maximum(m_prev, s.max(axis=-1, keepdims=True))
                p = jnp.exp(s - m_new)
                m_s[hh] = m_new
                pv = jnp.dot(p.astype(BF16), v_ref[:, lanes], preferred_element_type=F32)
                acc_s[hh] = acc_s[hh] * jnp.exp(m_prev - m_new) + pv

        edge = (j == i) | (j == 0)

        @pl.when(edge & (j <= i))
        def _():
            step(True)

        @pl.when(jnp.logical_not(edge) & (j <= i))
        def _():
            step(False)

        @pl.when(j == i)
        def _():
            row = i * blk + lax.broadcasted_iota(jnp.int32, (blk, 1), 0)
            lo = _lane_lo()
            acc = [acc_s[hh] for hh in range(2)]
            den = [acc[hh][:, den_lane[hh]:den_lane[hh] + 1] for hh in range(2)]
            o = jnp.where(lo, acc[0] * (1.0 / den[0]), acc[1] * (1.0 / den[1]))
            o_ref[...] = jnp.where(row >= PAD_ROWS, o, 0.0)
            lse_ref[...] = jnp.where(lo, m_s[0] + jnp.log(den[0]), m_s[1] + jnp.log(den[1]))

    q_spec = pl.BlockSpec((blk, 2 * CHUNK), lambda h, i, j: (i, h))
    k_spec = pl.BlockSpec((blk, 2 * CHUNK), lambda h, i, j: (jnp.minimum(j, i), h))
    o_spec = pl.BlockSpec((blk, CHUNK), lambda h, i, j: (i, h))
    return pl.pallas_call(
        body, name="attn_fwd",
        out_shape=(jax.ShapeDtypeStruct((rows, d), F32), jax.ShapeDtypeStruct((rows, d), F32)),
        grid=(hp, nb, nb),
        in_specs=[q_spec, k_spec, k_spec],
        out_specs=(o_spec, o_spec),
        scratch_shapes=[pltpu.VMEM((2, blk, 1), F32), pltpu.VMEM((2, blk, CHUNK), F32)],
        compiler_params=_params(("parallel", "parallel", "arbitrary")),
    )(q_aug, k_aug, v_aug)


def _attn_bwd(q_aug, k_aug, v_own, do, lse, delta, d):
    rows = q_aug.shape[0]
    hp = d // CHUNK
    blk = _attn_block(rows)
    nb = rows // blk
    scale = 1.0 / math.sqrt(HEAD_DIM)
    nt = (((1,), (1,)), ((), ()))
    tn = (((0,), (0,)), ((), ()))

    def body(q_ref, k_ref, v_ref, do_ref, lse_ref, dl_ref, dq_ref, dk_ref, dv_ref, df_ref, rs_ref,
             dq_s, dk_s, dv_s, df_s, rs_s):
        j = pl.program_id(1)
        i = pl.program_id(2)
        lo = _lane_lo()

        @pl.when((j == 0) & (i == 0))
        def _():
            dq_s[...] = jnp.zeros_like(dq_s)
            rs_s[...] = jnp.zeros_like(rs_s)

        @pl.when(i == j)
        def _():
            dk_s[...] = jnp.zeros_like(dk_s)
            dv_s[...] = jnp.zeros_like(dv_s)
            df_s[...] = jnp.zeros_like(df_s)

        def step(masked):
            dov = do_ref[...]
            t_dq, t_dk, t_dv, row_sums = [], [], [], []
            for hh in range(2):
                lanes = slice(hh * CHUNK, (hh + 1) * CHUNK)
                qh, kh = q_ref[:, lanes], k_ref[:, lanes]
                off = hh * HEAD_DIM
                s = lax.dot_general(qh, kh, nt, preferred_element_type=F32)
                if masked:
                    s = _masked_logits(s, i, j, blk)
                p = jnp.exp(s - lse_ref[:, off:off + 1])
                dp = lax.dot_general(dov, v_ref[:, lanes], nt, preferred_element_type=F32)
                ds = p * (dp - dl_ref[:, off:off + 1])
                df_s[hh:hh + 1, :] += ds.sum(axis=0, keepdims=True)
                row_sums.append(ds.sum(axis=1, keepdims=True))
                pb = p.astype(BF16)
                dsb = ds.astype(BF16)
                t_dv.append(lax.dot_general(pb, dov, tn, preferred_element_type=F32))
                t_dk.append(lax.dot_general(dsb, qh, tn, preferred_element_type=F32))
                t_dq.append(jnp.dot(dsb, kh, preferred_element_type=F32))
            dv_s[...] += jnp.where(lo, t_dv[0], t_dv[1])
            dk_s[...] += jnp.where(lo, t_dk[0], t_dk[1])
            r0 = pl.multiple_of(i * blk, blk)
            dq_s[pl.ds(r0, blk), :] += jnp.where(lo, t_dq[0], t_dq[1])
            rs_s[pl.ds(r0, blk), :] += jnp.where(lo, row_sums[0], row_sums[1])

        edge = (j == i) | (j == 0)

        @pl.when(edge & (i >= j))
        def _():
            step(True)

        @pl.when(jnp.logical_not(edge) & (i >= j))
        def _():
            step(False)

        @pl.when(i == nb - 1)
        def _():
            dk_ref[...] = dk_s[...].astype(BF16)
            dv_ref[...] = dv_s[...].astype(BF16)
            df_ref[...] = -df_s[...]

        @pl.when((i == nb - 1) & (j == nb - 1))
        def _():
            dq_ref[...] = (dq_s[...] * scale).astype(BF16)
            rs_ref[...] = rs_s[...]

    qi = lambda h, j, i: (jnp.maximum(i, j), h)
    q_spec = pl.BlockSpec((blk, 2 * CHUNK), qi)
    kv_spec = pl.BlockSpec((blk, 2 * CHUNK), lambda h, j, i: (j, h))
    row_spec = pl.BlockSpec((blk, CHUNK), qi)
    kv_out = pl.BlockSpec((blk, CHUNK), lambda h, j, i: (j, h))
    dq_out = pl.BlockSpec((rows, CHUNK), lambda h, j, i: (0, h))
    return pl.pallas_call(
        body, name="attn_bwd",
        out_shape=(jax.ShapeDtypeStruct((rows, d), BF16), jax.ShapeDtypeStruct((rows, d), BF16),
                   jax.ShapeDtypeStruct((rows, d), BF16), jax.ShapeDtypeStruct((hp, 2, rows), F32),
                   jax.ShapeDtypeStruct((rows, d), F32)),
        grid=(hp, nb, nb),
        in_specs=[q_spec, kv_spec, kv_spec, row_spec, row_spec, row_spec],
        out_specs=(dq_out, kv_out, kv_out, pl.BlockSpec((None, 2, blk), lambda h, j, i: (h, 0, j)), dq_out),
        scratch_shapes=[pltpu.VMEM((rows, CHUNK), F32), pltpu.VMEM((blk, CHUNK), F32),
                        pltpu.VMEM((blk, CHUNK), F32), pltpu.VMEM((2, blk), F32), pltpu.VMEM((rows, CHUNK), F32)],
        compiler_params=_params(("parallel", "arbitrary", "arbitrary")),
    )(q_aug, k_aug, v_own, do, lse, delta)


def _band(w, transposed, other):
    r = lax.broadcasted_iota(jnp.int32, (CHUNK, CHUNK), 0)
    c = lax.broadcasted_iota(jnp.int32, (CHUNK, CHUNK), 1)
    dist = (c - r) if transposed else (r - c)
    if other:
        dist = dist + CHUNK
    return jnp.where(dist >= 0, jnp.where(dist < w, 1.0, 0.0), 0.0).astype(BF16)


def _inv_count(chunk_index, w):
    row = chunk_index * CHUNK + lax.broadcasted_iota(jnp.int32, (CHUNK, 1), 0)
    cnt = jnp.clip(row - PAD_ROWS + 1, 1, w).astype(F32)
    return 1.0 / cnt


def _pool_diff(u_cur, u_prev, i, w):
    ws = _apply01(_band(w, False, False), u_cur, 3)
    ws = ws + jnp.where(i > 0, _apply01(_band(w, False, True), u_prev, 3), 0.0)
    return ws * _inv_count(i, w) - u_cur


def _pool_merge_fwd(rest, o, w_pool, scale, d):
    rows = rest.shape[0]
    n = rows // CHUNK
    cg = d // len(POOL_WINDOWS)

    def body(up_ref, uc_ref, gp_ref, ga_ref, o_ref, wp_ref, sc_ref, mg_ref, yp_ref):
        i = pl.program_id(0)
        for g, w in enumerate(POOL_WINDOWS):
            sl = slice(g * cg, (g + 1) * cg)
            diff = _pool_diff(uc_ref[:, sl], up_ref[:, sl], i, w)
            ypre = jnp.dot(diff.astype(BF16), wp_ref[g], preferred_element_type=F32)
            yp_ref[:, sl] = ypre
            merged = (jax.nn.sigmoid(gp_ref[:, sl]) * (ypre * sc_ref[:, sl])
                      + jax.nn.sigmoid(ga_ref[:, sl]) * o_ref[:, sl])
            mg_ref[:, sl] = merged.astype(BF16)

    col = lambda c: pl.BlockSpec((CHUNK, d), lambda i: (i, c))
    return pl.pallas_call(
        body, name="pool_merge_fwd",
        out_shape=(jax.ShapeDtypeStruct((rows, d), BF16), jax.ShapeDtypeStruct((rows, d), F32)),
        grid=(n,),
        in_specs=[pl.BlockSpec((CHUNK, d), lambda i: (jnp.maximum(i - 1, 0), 0)), col(0), col(1), col(2), col(0),
                  pl.BlockSpec((len(POOL_WINDOWS), cg, cg), lambda i: (0, 0, 0)),
                  pl.BlockSpec((1, d), lambda i: (0, 0))],
        out_specs=(col(0), col(0)),
        compiler_params=_params(("parallel",)),
    )(rest, rest, rest, rest, o, w_pool, scale.reshape(1, d))


def _gate_bwd(dm, rest, o, ypre, scale, d):
    rows = rest.shape[0]
    n = rows // CHUNK

    def body(dm_ref, gp_ref, ga_ref, o_ref, yp_ref, sc_ref, dgp_ref, dga_ref, do_ref, dl_ref, dy_ref, ds_ref):
        @pl.when(pl.program_id(0) == 0)
        def _():
            ds_ref[...] = jnp.zeros_like(ds_ref)

        dmv = dm_ref[...]
        sp = jax.nn.sigmoid(gp_ref[...])
        sa = jax.nn.sigmoid(ga_ref[...])
        ov = o_ref[...]
        ypre_v = yp_ref[...]
        sc = sc_ref[...]
        dgp_ref[...] = (dmv * (ypre_v * sc) * (sp * (1.0 - sp))).astype(BF16)
        dga_ref[...] = (dmv * ov * (sa * (1.0 - sa))).astype(BF16)
        t = dmv * sp
        dy_ref[...] = (t * sc).astype(BF16)
        ds_ref[...] += _fold8(t * ypre_v)
        dob = (dmv * sa).astype(BF16)
        do_ref[...] = dob
        prod = dob.astype(F32) * ov
        lo = _lane_lo()
        for pr in range(d // CHUNK):
            sl = slice(pr * CHUNK, (pr + 1) * CHUNK)
            tp = prod[:, sl]
            s_lo = jnp.where(lo, tp, 0.0).sum(axis=-1, keepdims=True)
            s_hi = jnp.where(lo, 0.0, tp).sum(axis=-1, keepdims=True)
            dl_ref[:, sl] = jnp.where(lo, s_lo, s_hi)

    col = lambda c: pl.BlockSpec((CHUNK, d), lambda i: (i, c))
    row_bf = jax.ShapeDtypeStruct((rows, d), BF16)
    outs = pl.pallas_call(
        body, name="gate_bwd",
        out_shape=(row_bf, row_bf, row_bf, jax.ShapeDtypeStruct((rows, d), F32), row_bf,
                   jax.ShapeDtypeStruct((8, d), F32)),
        grid=(n,),
        in_specs=[col(0), col(1), col(2), col(0), col(0), pl.BlockSpec((1, d), lambda i: (0, 0))],
        out_specs=(col(0), col(0), col(0), col(0), col(0), pl.BlockSpec((8, d), lambda i: (0, 0))),
        compiler_params=_params(("arbitrary",)),
    )(dm, rest, rest, o, ypre, scale.reshape(1, d))
    return outs[:5] + (outs[5].sum(axis=0),)


def _pool_bwd(dypre, rest, w_pool, d):
    rows = rest.shape[0]
    n = rows // CHUNK
    ng = len(POOL_WINDOWS)
    cg = d // ng
    nt = (((1,), (1,)), ((), ()))
    tn = (((0,), (0,)), ((), ()))

    def body(dc_ref, dn_ref, up_ref, uc_ref, wp_ref, du_ref, dw_ref):
        i = pl.program_id(0)

        @pl.when(i == 0)
        def _():
            dw_ref[...] = jnp.zeros_like(dw_ref)

        row = i * CHUNK + lax.broadcasted_iota(jnp.int32, (CHUNK, 1), 0)
        for g, w in enumerate(POOL_WINDOWS):
            sl = slice(g * cg, (g + 1) * cg)
            diff = _pool_diff(uc_ref[:, sl], up_ref[:, sl], i, w)
            dyc = dc_ref[:, sl]
            dw_ref[g] += lax.dot_general(diff.astype(BF16), dyc, tn, preferred_element_type=F32)
            wg = wp_ref[g]
            dd_cur = lax.dot_general(dyc, wg, nt, preferred_element_type=F32)
            dd_next = lax.dot_general(dn_ref[:, sl], wg, nt, preferred_element_type=F32)
            du = _apply01(_band(w, True, False), dd_cur * _inv_count(i, w), 2)
            du = du + jnp.where(i < n - 1, _apply01(_band(w, True, True), dd_next * _inv_count(i + 1, w), 2), 0.0)
            du = du - dd_cur
            du_ref[:, sl] = jnp.where(row >= PAD_ROWS, du, 0.0).astype(BF16)

    cur = pl.BlockSpec((CHUNK, d), lambda i: (i, 0))
    return pl.pallas_call(
        body, name="pool_bwd",
        out_shape=(jax.ShapeDtypeStruct((rows, d), BF16), jax.ShapeDtypeStruct((ng, cg, cg), F32)),
        grid=(n,),
        in_specs=[cur, pl.BlockSpec((CHUNK, d), lambda i: (jnp.minimum(i + 1, n - 1), 0)),
                  pl.BlockSpec((CHUNK, d), lambda i: (jnp.maximum(i - 1, 0), 0)), cur,
                  pl.BlockSpec((ng, cg, cg), lambda i: (0, 0, 0))],
        out_specs=(cur, pl.BlockSpec((ng, cg, cg), lambda i: (0, 0, 0))),
        compiler_params=_params(("arbitrary",)),
    )(dypre, dypre, rest, rest, w_pool)


def _loss_grad(h_res, target):
    rows, d = h_res.shape
    n = rows // CHUNK

    def body(h_ref, t_ref, dh_ref, acc_ref):
        i = pl.program_id(0)

        @pl.when(i == 0)
        def _():
            acc_ref[...] = jnp.zeros_like(acc_ref)
            dh_ref[...] = jnp.zeros_like(dh_ref)

        @pl.when(i > 0)
        def _():
            err = h_ref[...] - t_ref[...]
            dh_ref[...] = err * (1.0 / d)
            e2 = _fold8(err * err)
            part = e2[:, 0:CHUNK]
            for c in range(1, d // CHUNK):
                part = part + e2[:, c * CHUNK:(c + 1) * CHUNK]
            acc_ref[...] += part

    dh, acc = pl.pallas_call(
        body, name="loss_grad",
        out_shape=(jax.ShapeDtypeStruct((rows, d), F32), jax.ShapeDtypeStruct((8, CHUNK), F32)),
        grid=(n,),
        in_specs=[pl.BlockSpec((CHUNK, d), lambda i: (i, 0)),
                  pl.BlockSpec((CHUNK, d), lambda i: (jnp.maximum(i - 1, 0), 0))],
        out_specs=(pl.BlockSpec((CHUNK, d), lambda i: (i, 0)), pl.BlockSpec((8, CHUNK), lambda i: (0, 0))),
        compiler_params=_params(("arbitrary",)),
    )(h_res, target)
    return dh, (0.5 / d) * acc.sum()


def _adamw(slots, w, m, v, name):
    rows, cols = w.shape
    lanes = -(-cols // CHUNK) * CHUNK
    row_bytes = lanes * (N_DEV * slots.dtype.itemsize + 7 * 4)
    tr = max(t for t in _divisors(rows, 8) if t <= max(8, ADAM_TILE_BYTES // row_bytes))
    c1 = 1.0 - ADAM_B1 ** ADAM_STEP
    c2 = 1.0 - ADAM_B2 ** ADAM_STEP

    def body(s_ref, w_ref, m_ref, v_ref, g_ref, d_ref, mo_ref, vo_ref):
        g = s_ref[0].astype(F32)
        for k in range(1, N_DEV):
            g = g + s_ref[k].astype(F32)
        m_new = ADAM_B1 * m_ref[...] + (1.0 - ADAM_B1) * g
        v_new = ADAM_B2 * v_ref[...] + (1.0 - ADAM_B2) * (g * g)
        m_hat = m_new / c1
        v_hat = v_new / c2
        g_ref[...] = g
        d_ref[...] = -ADAM_LR * (m_hat / (jnp.sqrt(v_hat) + ADAM_EPS) + ADAM_WD * w_ref[...])
        mo_ref[...] = m_new
        vo_ref[...] = v_new

    tile = pl.BlockSpec((tr, cols), lambda i: (i, 0))
    sds = jax.ShapeDtypeStruct((rows, cols), F32)
    return pl.pallas_call(
        body, name=name,
        out_shape=(sds, sds, sds, sds),
        grid=(rows // tr,),
        in_specs=[pl.BlockSpec((N_DEV, tr, cols), lambda i: (0, i, 0)), tile, tile, tile],
        out_specs=(tile, tile, tile, tile),
        compiler_params=_params(("parallel",)),
    )(slots, w, m, v)


def _exchange(srcs, gather, name):
    n = len(srcs)
    shapes = [((N_DEV,) + s.shape) if gather else s.shape for s in srcs]

    def body(*refs):
        src_refs, out_refs = refs[:n], refs[n:2 * n]
        send_sems, recv_sems, local_sems = refs[2 * n:]
        x, y, c = lax.axis_index("x"), lax.axis_index("y"), lax.axis_index("c")
        me = 4 * x + 2 * y + c

        def payload(a, slot):
            return src_refs[a] if gather else src_refs[a].at[slot]

        own = [pltpu.make_async_copy(payload(a, me), out_refs[a].at[me], local_sems.at[a]) for a in range(n)]
        for cp in own:
            cp.start()
        sends, recvs = [], []
        for k in range(1, N_DEV):
            px = 1 - x if k & 4 else x
            py = 1 - y if k & 2 else y
            pc = 1 - c if k & 1 else c
            peer = 4 * px + 2 * py + pc
            for a in range(n):
                sems = dict(send_sem=send_sems.at[(k - 1) * n + a], recv_sem=recv_sems.at[(k - 1) * n + a],
                            device_id=(px, py, pc), device_id_type=MESH_ID)
                sends.append(pltpu.make_async_remote_copy(src_ref=payload(a, peer), dst_ref=out_refs[a].at[me], **sems))
                recvs.append(pltpu.make_async_remote_copy(src_ref=payload(a, peer), dst_ref=out_refs[a].at[peer], **sems))
        for cp in sends:
            cp.start()
        for cp in recvs:
            cp.wait_recv()
        for cp in sends:
            cp.wait_send()
        for cp in own:
            cp.wait()

    return pl.pallas_call(
        body, name=name,
        out_shape=tuple(jax.ShapeDtypeStruct(sh, s.dtype) for sh, s in zip(shapes, srcs)),
        in_specs=[pl.BlockSpec(memory_space=pl.ANY)] * n,
        out_specs=tuple([pl.BlockSpec(memory_space=pl.ANY)] * n),
        scratch_shapes=[pltpu.SemaphoreType.DMA(((N_DEV - 1) * n,)), pltpu.SemaphoreType.DMA(((N_DEV - 1) * n,)),
                        pltpu.SemaphoreType.DMA((n,))],
    )(*srcs)


HBM_SPEC = pl.BlockSpec(memory_space=pltpu.HBM)
SEM_SPEC = pl.BlockSpec(memory_space=pltpu.SEMAPHORE)
DATAFLOW = pltpu.SideEffectType.DATAFLOW_SIDE_EFFECTING


def _peer_copies(src_refs, land_refs, send_sems, recv_sems, gather):
    n = len(src_refs)
    x, y, c = lax.axis_index("x"), lax.axis_index("y"), lax.axis_index("c")
    me = 4 * x + 2 * y + c
    sends, lands = [], []
    for k in range(1, N_DEV):
        px = 1 - x if k & 4 else x
        py = 1 - y if k & 2 else y
        pc = 1 - c if k & 1 else c
        peer = 4 * px + 2 * py + pc
        for a in range(n):
            src = src_refs[a] if gather else src_refs[a].at[peer]
            sems = dict(send_sem=send_sems.at[(k - 1) * n + a], recv_sem=recv_sems.at[(k - 1) * n + a],
                        device_id=(px, py, pc), device_id_type=MESH_ID)
            sends.append(pltpu.make_async_remote_copy(src_ref=src, dst_ref=land_refs[a].at[me], **sems))
            lands.append(pltpu.make_async_remote_copy(src_ref=src, dst_ref=land_refs[a].at[peer], **sems))
    return sends, lands


def _exchange_start(srcs, gather, name):
    n = len(srcs)
    shapes = [((N_DEV,) + s.shape) if gather else s.shape for s in srcs]
    n_sem = (N_DEV - 1) * n

    def body(*refs):
        src_refs, land_refs = refs[:n], refs[n:2 * n]
        send_sems, recv_sems = refs[2 * n], refs[2 * n + 1]
        token = refs[-1]
        sends, _ = _peer_copies(src_refs, land_refs, send_sems, recv_sems, gather)
        for cp in sends:
            cp.start()
        token[...] = jnp.zeros_like(token)

    hbm = lambda arrays_shapes: [pltpu.HBM(sh, dt) for sh, dt in arrays_shapes]
    src_types = [(s.shape, s.dtype) for s in srcs]
    land_types = [(sh, s.dtype) for sh, s in zip(shapes, srcs)]
    outs = pl.pallas_call(
        body, name=name,
        out_shape=(pltpu.SemaphoreType.DMA((n_sem,)), pltpu.SemaphoreType.DMA((n_sem,)),
                   *hbm(src_types), *hbm(land_types), jax.ShapeDtypeStruct((8, CHUNK), F32)),
        in_specs=[HBM_SPEC] * (2 * n),
        out_specs=(SEM_SPEC, SEM_SPEC, *([HBM_SPEC] * (2 * n)), pl.BlockSpec(memory_space=pltpu.VMEM)),
        input_output_aliases={a: 2 + a for a in range(2 * n)},
        compiler_params=pltpu.CompilerParams(has_side_effects=DATAFLOW),
    )(*[pltpu.with_memory_space_constraint(s, pltpu.HBM) for s in srcs],
      *[pltpu.with_memory_space_constraint(lax.empty(sh, dt), pltpu.HBM) for sh, dt in land_types])
    return (outs[0], outs[1], outs[2:2 + n], outs[2 + n:2 + 2 * n]), outs[-1]


def _exchange_wait(handle, after, gather, name):
    send_sems, recv_sems, src_thru, land_thru = handle
    n = len(src_thru)

    def body(*refs):
        src_refs, land_refs = refs[:n], refs[n:2 * n]
        _, lands = _peer_copies(src_refs, land_refs, refs[2 * n], refs[2 * n + 1], gather)
        for cp in lands:
            cp.wait_send()
            cp.wait_recv()

    outs = pl.pallas_call(
        body, name=name,
        out_shape=tuple(pltpu.HBM(t.shape, t.dtype) for t in (*src_thru, *land_thru)),
        in_specs=[*([HBM_SPEC] * (2 * n)), SEM_SPEC, SEM_SPEC, pl.BlockSpec(memory_space=pl.ANY)],
        out_specs=tuple([HBM_SPEC] * (2 * n)),
        input_output_aliases={a: a for a in range(2 * n)},
        compiler_params=pltpu.CompilerParams(has_side_effects=DATAFLOW),
    )(*src_thru, *land_thru, send_sems, recv_sems, after)
    return outs[n:]


def _fill_own(land, own, me):
    return lax.dynamic_update_slice_in_dim(land, own[None].astype(land.dtype), me, axis=0)


BIG = ("w_in", "w_pool", "w_out", "w_gate", "w_up", "w_down")
SHARD_AXIS = dict(w_in=-1, w_pool=-2, w_out=-2, w_gate=-1, w_up=-1, w_down=-2)
SMALL = ("norm_mix_pre", "norm_mix_post", "norm_ffn_pre", "norm_ffn_post", "pool_scale")


def _join(g, axis):
    return jnp.concatenate([g[j] for j in range(N_DEV)], axis=axis)


def _split(full, axis):
    return jnp.stack(jnp.split(full, N_DEV, axis=axis))


def _rows2d(a):
    return a.reshape(-1, a.shape[-1])


def kernel(x, meta_tokens, norm_mix_pre, norm_mix_post, norm_ffn_pre, norm_ffn_post, w_in, b_forget, w_pool, pool_scale, w_out, w_ffn_gate, w_ffn_up, w_ffn_down, loss_target, m_meta_tokens, m_norm_mix_pre, m_norm_mix_post, m_norm_ffn_pre, m_norm_ffn_post, m_w_in, m_b_forget, m_w_pool, m_pool_scale, m_w_out, m_w_ffn_gate, m_w_ffn_up, m_w_ffn_down, v_meta_tokens, v_norm_mix_pre, v_norm_mix_post, v_norm_ffn_pre, v_norm_ffn_post, v_w_in, v_b_forget, v_w_pool, v_pool_scale, v_w_out, v_w_ffn_gate, v_w_ffn_up, v_w_ffn_down):
    x2 = x[0]
    target = loss_target[0]
    seq, d = x2.shape
    depth = w_in.shape[0]
    heads = d // HEAD_DIM
    ff = w_ffn_gate.shape[2] * N_DEV
    rows = PAD_ROWS + META_TOKENS + seq
    assert seq % CHUNK == 0 and d % (2 * CHUNK) == 0 and heads <= FORGET_PAD and depth >= 2
    me = 4 * lax.axis_index("x") + 2 * lax.axis_index("y") + lax.axis_index("c")

    big = dict(w_in=w_in, w_pool=w_pool, w_out=w_out, w_gate=w_ffn_gate, w_up=w_ffn_up, w_down=w_ffn_down)
    big_m = dict(w_in=m_w_in, w_pool=m_w_pool, w_out=m_w_out, w_gate=m_w_ffn_gate, w_up=m_w_ffn_up, w_down=m_w_ffn_down)
    big_v = dict(w_in=v_w_in, w_pool=v_w_pool, w_out=v_w_out, w_gate=v_w_ffn_gate, w_up=v_w_ffn_up, w_down=v_w_ffn_down)
    small = dict(norm_mix_pre=norm_mix_pre, norm_mix_post=norm_mix_post, norm_ffn_pre=norm_ffn_pre,
                 norm_ffn_post=norm_ffn_post, pool_scale=pool_scale)
    small_m = dict(norm_mix_pre=m_norm_mix_pre, norm_mix_post=m_norm_mix_post, norm_ffn_pre=m_norm_ffn_pre,
                   norm_ffn_post=m_norm_ffn_post, pool_scale=m_pool_scale)
    small_v = dict(norm_mix_pre=v_norm_mix_pre, norm_mix_post=v_norm_mix_post, norm_ffn_pre=v_norm_ffn_pre,
                   norm_ffn_post=v_norm_ffn_post, pool_scale=v_pool_scale)

    wire = [big[n].astype(BF16) for n in BIG]
    gathered = _exchange([w[:1] for w in wire] + [meta_tokens], True, "gather_first")
    meta_full = _join(gathered[-1], -1)
    later, started = _exchange_start([w[1:] for w in wire], True, "gather_later_start")

    def layer_weights(lands):
        full = {n: _join(g, SHARD_AXIS[n]) for n, g in zip(BIG, lands)}
        win = full["w_in"]
        fcol = 4 * d
        qkv_w = win[:, :, d:4 * d]
        rest_w = jnp.concatenate([win[:, :, :d], win[:, :, fcol + heads:], win[:, :, fcol:fcol + heads],
                                  jnp.zeros(win.shape[:2] + (FORGET_PAD - heads,), BF16)], axis=2)
        return dict(qkv=qkv_w, rest=rest_w, cat=jnp.concatenate([qkv_w, rest_w], axis=2),
                    gu=jnp.concatenate([full["w_gate"], full["w_up"]], axis=2),
                    o=full["w_out"], dn=full["w_down"], pl=full["w_pool"])

    groups = [layer_weights(gathered[:-1])]
    weight = lambda key, l: groups[min(l, 1)][key][l - min(l, 1)]
    w_qkv, w_rest, w_cat, w_gu, w_o, w_dn, w_pl = (
        [None] * depth for _ in range(7))

    h_res = jnp.concatenate([jnp.zeros((PAD_ROWS, d), F32), meta_full, x2], axis=0)
    h1 = _norm_fwd(h_res, norm_mix_pre[0] + started[0, 0])
    ones = jnp.ones((d,), F32)
    saved = []
    for l in range(depth):
        if l == 1:
            lands = _exchange_wait(later, h_res, True, "gather_later_wait")
            groups.append(layer_weights([_fill_own(g, w[1:], me) for g, w in zip(lands, wire)]))
        for table_, key in ((w_qkv, "qkv"), (w_rest, "rest"), (w_cat, "cat"), (w_gu, "gu"), (w_o, "o"),
                            (w_dn, "dn"), (w_pl, "pl")):
            table_[l] = weight(key, l)
        qkv = _matmul(h1, w_qkv[l], "nn", BF16, "proj_qkv")
        rest = _matmul(h1, w_rest[l], "nn", F32, "proj_rest")
        z = rest[:, 3 * d:3 * d + heads].T
        bias = jnp.broadcast_to(b_forget[l][:, None], (heads, CHUNK))
        q_aug, k_aug, v_own, v_aug = _attn_prep(qkv, _logf_fwd(z, bias).T, d)
        o, lse = _attn_fwd(q_aug, k_aug, v_aug, d)
        merged, ypre = _pool_merge_fwd(rest, o, w_pl[l], pool_scale[l], d)
        mix = _matmul(merged, w_o[l], "nn", F32, "mix_out")
        h_mid, h2 = _resid_norm_fwd(h_res, mix, norm_mix_post[l], norm_ffn_pre[l])
        ab = _matmul(h2, w_gu[l], "nn", F32, "ffn_in")
        act = _swiglu_fwd(ab)
        ffo = _matmul(act, w_dn[l], "nn", F32, "ffn_out")
        g_next = norm_mix_pre[l + 1] if l + 1 < depth else ones
        h_next, h1_next = _resid_norm_fwd(h_mid, ffo, norm_ffn_post[l], g_next)
        saved.append(dict(h_in=h_res, h1=h1, q_aug=q_aug, k_aug=k_aug, v_own=v_own, rest=rest, z=z,
                          bias=bias, o=o, lse=lse,
                          merged=merged, ypre=ypre, mix=mix, h_mid=h_mid, h2=h2, ab=ab, act=act, ffo=ffo))
        h_res, h1 = h_next, h1_next

    dh, loss_local = _loss_grad(h_res, target)
    loss = lax.psum(loss_local, ("x", "y", "c"))

    grads = {n: [None] * depth for n in BIG + SMALL + ("b_forget",)}

    def grad_slots(first, last):
        parts = [_split(jnp.stack(grads[n][first:last]), SHARD_AXIS[n]).astype(BF16) for n in BIG]
        return [p.reshape(N_DEV, -1, p.shape[-1]) for p in parts]

    for l in reversed(range(depth)):
        s = saved[l]
        g_post = norm_ffn_post[l]
        if l == 0:
            early_slots = grad_slots(1, depth)
            early, started = _exchange_start(early_slots, False, "scatter_early_start")
            g_post = g_post + started[0, 0]
        dffo, grads["norm_ffn_post"][l] = _norm_bwd(s["ffo"], g_post, dh, None, BF16)
        grads["w_down"][l] = _matmul(s["act"], dffo, "tn", F32, "grad_w_down")
        dact = _matmul(dffo, w_dn[l], "nt", F32, "ffn_out_dx")
        da, db = _swiglu_bwd(s["ab"], dact)
        dab = jnp.concatenate([da, db], axis=1)
        dgu = _matmul(s["h2"], dab, "tn", F32, "grad_w_gu")
        grads["w_gate"][l], grads["w_up"][l] = dgu[:, :ff], dgu[:, ff:]
        dh2 = _matmul(dab, w_gu[l], "nt", F32, "ffn_in_dx")
        dh_mid, grads["norm_ffn_pre"][l] = _norm_bwd(s["h_mid"], norm_ffn_pre[l], dh2, dh, F32)

        dmix, grads["norm_mix_post"][l] = _norm_bwd(s["mix"], norm_mix_post[l], dh_mid, None, BF16)
        grads["w_out"][l] = _matmul(s["merged"], dmix, "tn", F32, "grad_w_out")
        dm = _matmul(dmix, w_o[l], "nt", F32, "mix_out_dx")
        dgp, dga, do, delta, dypre, grads["pool_scale"][l] = _gate_bwd(dm, s["rest"], s["o"], s["ypre"], pool_scale[l], d)
        du, grads["w_pool"][l] = _pool_bwd(dypre, s["rest"], w_pl[l], d)
        dq, dk, dv, df_key, df_query = _attn_bwd(s["q_aug"], s["k_aug"], s["v_own"], do, s["lse"], delta, d)
        df = df_key.reshape(heads, rows) + df_query.reshape(rows, heads, HEAD_DIM)[:, :, 0].T
        dz, grads["b_forget"][l] = _logf_bwd(df, s["z"], s["bias"])
        dzt = jnp.pad(dz.T.astype(BF16), ((0, 0), (0, FORGET_PAD - heads)))
        dproj = jnp.concatenate([dq, dk, dv, du, dgp, dga, dzt], axis=1)
        dwc = _matmul(s["h1"], dproj, "tn", F32, "grad_w_in")
        grads["w_in"][l] = jnp.concatenate([dwc[:, 3 * d:4 * d], dwc[:, :3 * d], dwc[:, 6 * d:6 * d + heads],
                                            dwc[:, 4 * d:6 * d]], axis=1)
        dh1 = _matmul(dproj, w_cat[l], "nt", F32, "proj_dx")
        dh, grads["norm_mix_pre"][l] = _norm_bwd(s["h_in"], norm_mix_pre[l], dh1, dh_mid, F32)

    grad_x = dh[PAD_ROWS + META_TOKENS:][None]
    dmeta = dh[PAD_ROWS:PAD_ROWS + META_TOKENS]

    late_recv = _exchange(grad_slots(0, 1), False, "scatter_first")
    early_recv = _exchange_wait(early, dh, False, "scatter_early_wait")
    early_recv = [_fill_own(r, lax.dynamic_index_in_dim(s_, me, 0, keepdims=False), me)
                  for r, s_ in zip(early_recv, early_slots)]
    recv = [jnp.concatenate([a, b], axis=1) for a, b in zip(late_recv, early_recv)]
    big_out = {}
    for n, r in zip(BIG, recv):
        outs = _adamw(r, _rows2d(big[n]), _rows2d(big_m[n]), _rows2d(big_v[n]), "adamw_" + n)
        big_out[n] = [o_.reshape(big[n].shape) for o_ in outs]

    def table(parts, forget):
        t = jnp.concatenate([_rows2d(p) for p in parts] + [jnp.pad(forget, ((0, 0), (0, d - heads)))], axis=0)
        return jnp.pad(t, ((0, -t.shape[0] % 8), (0, 0)))

    g_table = table([jnp.stack(grads[n]) for n in SMALL], jnp.stack(grads["b_forget"]))
    rep_rows = g_table.shape[0]
    got = _exchange([jnp.concatenate([g_table, dmeta], axis=0)], True, "gather_small_grads")[0]
    outs = _adamw(got[:, :rep_rows], table([small[n] for n in SMALL], b_forget),
                  table([small_m[n] for n in SMALL], m_b_forget), table([small_v[n] for n in SMALL], v_b_forget),
                  "adamw_small")
    dcols = d // N_DEV
    meta_slots = lax.dynamic_slice_in_dim(got[:, rep_rows:], me * dcols, dcols, axis=2)
    meta_out = _adamw(meta_slots, meta_tokens, m_meta_tokens, v_meta_tokens, "adamw_meta")

    def ordered(k):
        t = outs[k]
        so = {n: t[a * depth:(a + 1) * depth] for a, n in enumerate(SMALL)}
        forget = t[len(SMALL) * depth:(len(SMALL) + 1) * depth, :heads]
        return (meta_out[k], so["norm_mix_pre"], so["norm_mix_post"], so["norm_ffn_pre"], so["norm_ffn_post"],
                big_out["w_in"][k], forget, big_out["w_pool"][k], so["pool_scale"], big_out["w_out"][k],
                big_out["w_gate"][k], big_out["w_up"][k], big_out["w_down"][k])

    return (loss, grad_x) + ordered(0) + ordered(1) + ordered(2) + ordered(3)
```

```python
import math

import jax
import jax.numpy as jnp
from jax import lax
from jax.experimental import pallas as pl
from jax.experimental.pallas import tpu as pltpu

F32 = jnp.float32
BF16 = jnp.bfloat16

N_DEV = 8
META_TOKENS = 16
PAD_ROWS = 112
CHUNK = 128
HEAD_DIM = 64
POOL_WINDOWS = (2, 4, 8, 16)
FORGET_PAD = 256
RMS_EPS = 1e-6
NEG_INF = -1e30
ADAM_LR, ADAM_B1, ADAM_B2, ADAM_EPS, ADAM_WD, ADAM_STEP = 0.001, 0.9, 0.999, 1e-08, 0.01, 10

VMEM_LIMIT = 56 * 1024 * 1024
VMEM_TILE_BUDGET = 36 * 1024 * 1024
ADAM_TILE_BYTES = 8 * 1024 * 1024
MESH_ID = pl.DeviceIdType.MESH


def _params(sem, vmem=VMEM_LIMIT):
    return pltpu.CompilerParams(dimension_semantics=sem, vmem_limit_bytes=vmem)


def _divisors(n, mult):
    return [d for d in range(mult, n + 1, mult) if n % d == 0]


def _row_tile(rows, cap):
    return max(d for d in _divisors(rows, CHUNK) if d <= max(cap, CHUNK))


def _fold8(x):
    r, c = x.shape
    return x.reshape(r // 8, 8, c).sum(axis=0)


def _split_bf16(x, parts):
    out = []
    for _ in range(parts - 1):
        hi = x.astype(BF16)
        out.append(hi)
        x = x - hi.astype(F32)
    out.append(x.astype(BF16))
    return out


def _apply01(mat, x, parts, left=True):
    acc = None
    for p in _split_bf16(x, parts):
        t = jnp.dot(mat, p, preferred_element_type=F32) if left else jnp.dot(p, mat, preferred_element_type=F32)
        acc = t if acc is None else acc + t
    return acc


def _matmul_tiles(m, n, k, mode, out_bytes):
    if mode == "tn":
        tk = _row_tile(k, 640)
    else:
        tk = max(d for d in _divisors(k, CHUNK) if d <= 1536)
    nk = k // tk
    best = None
    m_opts = _divisors(m, CHUNK)
    n_opts = _divisors(n, CHUNK)
    for tm in m_opts:
        for tn in n_opts:
            need = 2 * 2 * (tm * tk + tk * tn) + 2 * tm * tn * out_bytes
            if nk > 1 or mode == "tn":
                need += tm * tn * 4
            need += tm * tn * 4
            if need > VMEM_TILE_BUDGET:
                continue
            key = (tm * tn, tn)
            if best is None or key > best[0]:
                best = (key, tm, tn)
    return best[1], best[2], tk


def _matmul(a, b, mode, out_dtype, name):
    if mode == "nn":
        (m, k), (k2, n) = a.shape, b.shape
    elif mode == "nt":
        (m, k), (n, k2) = a.shape, b.shape
    else:
        (k, m), (k2, n) = a.shape, b.shape
    assert k == k2 and a.dtype == BF16 and b.dtype == BF16
    tm, tn, tk = _matmul_tiles(m, n, k, mode, jnp.dtype(out_dtype).itemsize)
    nk = k // tk
    if mode == "nn":
        a_spec = pl.BlockSpec((tm, tk), lambda i, j, r: (i, r))
        b_spec = pl.BlockSpec((tk, tn), lambda i, j, r: (r, j))
        dims = (((1,), (0,)), ((), ()))
    elif mode == "nt":
        a_spec = pl.BlockSpec((tm, tk), lambda i, j, r: (i, r))
        b_spec = pl.BlockSpec((tn, tk), lambda i, j, r: (j, r))
        dims = (((1,), (1,)), ((), ()))
    else:
        a_spec = pl.BlockSpec((tk, tm), lambda i, j, r: (r, i))
        b_spec = pl.BlockSpec((tk, tn), lambda i, j, r: (r, j))
        dims = (((0,), (0,)), ((), ()))

    def body(a_ref, b_ref, o_ref, *acc):
        part = lax.dot_general(a_ref[...], b_ref[...], dims, preferred_element_type=F32)
        if nk == 1:
            o_ref[...] = part.astype(o_ref.dtype)
        else:
            r = pl.program_id(2)

            @pl.when(r == 0)
            def _():
                acc[0][...] = part

            @pl.when(r > 0)
            def _():
                acc[0][...] += part

            @pl.when(r == nk - 1)
            def _():
                o_ref[...] = acc[0][...].astype(o_ref.dtype)

    return pl.pallas_call(
        body, name=name,
        out_shape=jax.ShapeDtypeStruct((m, n), out_dtype),
        grid=(m // tm, n // tn, nk),
        in_specs=[a_spec, b_spec],
        out_specs=pl.BlockSpec((tm, tn), lambda i, j, r: (i, j)),
        scratch_shapes=[pltpu.VMEM((tm, tn), F32)] if nk > 1 else [],
        compiler_params=_params(("parallel", "parallel", "arbitrary")),
    )(a, b)


def _rms(x, g):
    rstd = lax.rsqrt(jnp.mean(x * x, axis=-1, keepdims=True) + RMS_EPS)
    return x * rstd * g


def _norm_fwd(x, g):
    rows, d = x.shape
    tm = _row_tile(rows, 640)

    def body(x_ref, g_ref, h_ref):
        h_ref[...] = _rms(x_ref[...], g_ref[...]).astype(BF16)

    return pl.pallas_call(
        body, name="norm_fwd",
        out_shape=jax.ShapeDtypeStruct((rows, d), BF16),
        grid=(rows // tm,),
        in_specs=[pl.BlockSpec((tm, d), lambda i: (i, 0)), pl.BlockSpec((1, d), lambda i: (0, 0))],
        out_specs=pl.BlockSpec((tm, d), lambda i: (i, 0)),
        compiler_params=_params(("parallel",)),
    )(x, g.reshape(1, d))


def _resid_norm_fwd(h_res, y, g_post, g_next):
    rows, d = h_res.shape
    tm = _row_tile(rows, 640)

    def body(r_ref, y_ref, gp_ref, gn_ref, hn_ref, hx_ref):
        h_new = r_ref[...] + _rms(y_ref[...], gp_ref[...])
        hn_ref[...] = h_new
        hx_ref[...] = _rms(h_new, gn_ref[...]).astype(BF16)

    row = pl.BlockSpec((tm, d), lambda i: (i, 0))
    vec = pl.BlockSpec((1, d), lambda i: (0, 0))
    return pl.pallas_call(
        body, name="resid_norm_fwd",
        out_shape=(jax.ShapeDtypeStruct((rows, d), F32), jax.ShapeDtypeStruct((rows, d), BF16)),
        grid=(rows // tm,),
        in_specs=[row, row, vec, vec],
        out_specs=(row, row),
        compiler_params=_params(("parallel",)),
    )(h_res, y, g_post.reshape(1, d), g_next.reshape(1, d))


def _norm_bwd(x, g, dy, resid, out_dtype):
    rows, d = x.shape
    tm = _row_tile(rows, 640)
    has_resid = resid is not None

    def body(*refs):
        if has_resid:
            x_ref, g_ref, dy_ref, r_ref, dx_ref, dg_ref = refs
        else:
            x_ref, g_ref, dy_ref, dx_ref, dg_ref = refs
        xv = x_ref[...]
        dyv = dy_ref[...].astype(F32)
        rstd = lax.rsqrt(jnp.mean(xv * xv, axis=-1, keepdims=True) + RMS_EPS)
        xhat = xv * rstd
        gdy = dyv * g_ref[...]
        dx = rstd * (gdy - xhat * jnp.mean(gdy * xhat, axis=-1, keepdims=True))
        if has_resid:
            dx = dx + r_ref[...]
        dx_ref[...] = dx.astype(dx_ref.dtype)

        @pl.when(pl.program_id(0) == 0)
        def _():
            dg_ref[...] = jnp.zeros_like(dg_ref)

        dg_ref[...] += _fold8(dyv * xhat)

    row = pl.BlockSpec((tm, d), lambda i: (i, 0))
    vec = pl.BlockSpec((1, d), lambda i: (0, 0))
    args = [x, g.reshape(1, d), dy] + ([resid] if has_resid else [])
    dx, dg = pl.pallas_call(
        body, name="norm_bwd_resid" if has_resid else "norm_bwd",
        out_shape=(jax.ShapeDtypeStruct((rows, d), out_dtype), jax.ShapeDtypeStruct((8, d), F32)),
        grid=(rows // tm,),
        in_specs=[row, vec, row] + ([row] if has_resid else []),
        out_specs=(row, pl.BlockSpec((8, d), lambda i: (0, 0))),
        compiler_params=_params(("arbitrary",)),
    )(*args)
    return dx, dg.sum(axis=0)


def _swiglu_fwd(ab):
    rows, two_f = ab.shape
    f = two_f // 2
    tm = _row_tile(rows, 640)

    def body(a_ref, b_ref, o_ref):
        a = a_ref[...]
        o_ref[...] = (a * jax.nn.sigmoid(a) * b_ref[...]).astype(BF16)

    return pl.pallas_call(
        body, name="swiglu_fwd",
        out_shape=jax.ShapeDtypeStruct((rows, f), BF16),
        grid=(rows // tm,),
        in_specs=[pl.BlockSpec((tm, f), lambda i: (i, 0)), pl.BlockSpec((tm, f), lambda i: (i, 1))],
        out_specs=pl.BlockSpec((tm, f), lambda i: (i, 0)),
        compiler_params=_params(("parallel",)),
    )(ab, ab)


def _swiglu_bwd(ab, dff):
    rows, two_f = ab.shape
    f = two_f // 2
    tm = _row_tile(rows, 256)

    def body(a_ref, b_ref, d_ref, dab_ref):
        a = a_ref[...]
        d = d_ref[...]
        s = jax.nn.sigmoid(a)
        dab_ref[:, :f] = (d * b_ref[...] * (s * (1.0 + a * (1.0 - s)))).astype(BF16)
        dab_ref[:, f:] = (d * (a * s)).astype(BF16)

    lo = pl.BlockSpec((tm, f), lambda i: (i, 0))
    hi = pl.BlockSpec((tm, f), lambda i: (i, 1))
    return pl.pallas_call(
        body, name="swiglu_bwd",
        out_shape=jax.ShapeDtypeStruct((rows, two_f), BF16),
        grid=(rows // tm,),
        in_specs=[lo, hi, lo],
        out_specs=pl.BlockSpec((tm, two_f), lambda i: (i, 0)),
        compiler_params=_params(("parallel",)),
    )(ab, ab, dff)


def _tri(lower):
    r = lax.broadcasted_iota(jnp.int32, (CHUNK, CHUNK), 0)
    c = lax.broadcasted_iota(jnp.int32, (CHUNK, CHUNK), 1)
    return jnp.where((r >= c) if lower else (r <= c), 1.0, 0.0).astype(BF16)


def _logf_fwd(z, b):
    h, rows = z.shape
    n = rows // CHUNK

    def body(z_ref, b_ref, f_ref, carry):
        i = pl.program_id(0)

        @pl.when(i == 0)
        def _():
            carry[...] = jnp.zeros_like(carry)

        x = z_ref[...] + b_ref[...]
        lf = jnp.minimum(x, 0.0) - jnp.log(1.0 + jnp.exp(-jnp.abs(x)))
        col = i * CHUNK + lax.broadcasted_iota(jnp.int32, (1, CHUNK), 1)
        lf = jnp.where(col >= PAD_ROWS, lf, 0.0)
        run = _apply01(_tri(False), lf, 3, left=False) + carry[...]
        f_ref[...] = run
        carry[...] = jnp.broadcast_to(run[:, CHUNK - 1:CHUNK], carry.shape)

    blk = pl.BlockSpec((h, CHUNK), lambda i: (0, i))
    return pl.pallas_call(
        body, name="logf_fwd",
        out_shape=jax.ShapeDtypeStruct((h, rows), F32),
        grid=(n,),
        in_specs=[blk, pl.BlockSpec((h, CHUNK), lambda i: (0, 0))],
        out_specs=blk,
        scratch_shapes=[pltpu.VMEM((h, CHUNK), F32)],
        compiler_params=_params(("arbitrary",)),
    )(z, b)


def _logf_bwd(df, z, b):
    h, rows = z.shape
    n = rows // CHUNK

    def body(df_ref, z_ref, b_ref, dz_ref, db_ref, carry):
        i = pl.program_id(0)

        @pl.when(i == 0)
        def _():
            carry[...] = jnp.zeros_like(carry)
            db_ref[...] = jnp.zeros_like(db_ref)

        run = _apply01(_tri(True), df_ref[...], 3, left=False) + carry[...]
        carry[...] = jnp.broadcast_to(run[:, 0:1], carry.shape)
        x = z_ref[...] + b_ref[...]
        col = (n - 1 - i) * CHUNK + lax.broadcasted_iota(jnp.int32, (1, CHUNK), 1)
        dz = jnp.where(col >= PAD_ROWS, run * (1.0 - jax.nn.sigmoid(x)), 0.0)
        dz_ref[...] = dz
        db_ref[...] += dz

    rev = pl.BlockSpec((h, CHUNK), lambda i: (0, n - 1 - i))
    fix = pl.BlockSpec((h, CHUNK), lambda i: (0, 0))
    dz, db = pl.pallas_call(
        body, name="logf_bwd",
        out_shape=(jax.ShapeDtypeStruct((h, rows), F32), jax.ShapeDtypeStruct((h, CHUNK), F32)),
        grid=(n,),
        in_specs=[rev, rev, fix],
        out_specs=(rev, fix),
        scratch_shapes=[pltpu.VMEM((h, CHUNK), F32)],
        compiler_params=_params(("arbitrary",)),
    )(df, z, b)
    return dz, db.sum(axis=1)


def _lane_lo():
    return lax.broadcasted_iota(jnp.int32, (1, CHUNK), 1) < HEAD_DIM


def _attn_block(rows):
    return _row_tile(rows, min(640, rows // 2))


def _masked_logits(s, i, j, blk):
    row = i * blk + lax.broadcasted_iota(jnp.int32, (blk, 1), 0)
    col = j * blk + lax.broadcasted_iota(jnp.int32, (1, blk), 1)
    return jnp.where(col <= row, jnp.where(col >= PAD_ROWS, s, NEG_INF), NEG_INF)


def _attn_prep(qkv, f_t, d):
    rows = qkv.shape[0]
    heads = d // HEAD_DIM
    hp = heads // 2
    tm = _row_tile(rows, 640)
    scale = 1.0 / math.sqrt(HEAD_DIM)

    def body(q_ref, k_ref, v_ref, f_ref, qa_ref, ka_ref, vo_ref, va_ref):
        pr = pl.program_id(1)
        lane = lax.broadcasted_iota(jnp.int32, (1, CHUNK), 1)
        lo = lane < HEAD_DIM
        q2 = (q_ref[...].astype(F32) * scale).astype(BF16)
        k2 = k_ref[...]
        v2 = v_ref[...]
        zero = jnp.zeros_like(k2)
        head_id = lax.broadcasted_iota(jnp.int32, (1, heads), 1)
        ft = f_ref[...]
        for hh in range(2):
            base = HEAD_DIM if hh == 0 else 0
            neg_f = -jnp.sum(jnp.where(head_id == 2 * pr + hh, ft, 0.0), axis=1, keepdims=True)
            aug = zero
            ones = zero
            for t, part in enumerate(_split_bf16(neg_f, 3)):
                aug = jnp.where(lane == base + t, part, aug)
                ones = jnp.where(lane == base + t, jnp.ones_like(zero), ones)
            own = lo if hh == 0 else jnp.logical_not(lo)
            sl = slice(hh * CHUNK, (hh + 1) * CHUNK)
            qa_ref[:, sl] = jnp.where(own, q2, ones)
            ka_ref[:, sl] = jnp.where(own, k2, aug)
            vo_ref[:, sl] = jnp.where(own, v2, zero)
            va_ref[:, sl] = jnp.where(own, v2, jnp.where(lane == base, jnp.ones_like(zero), zero))

    pair_in = lambda c: pl.BlockSpec((tm, CHUNK), lambda i, p: (i, c * hp + p))
    pair_out = pl.BlockSpec((tm, 2 * CHUNK), lambda i, p: (i, p))
    sds = jax.ShapeDtypeStruct((rows, 2 * d), BF16)
    return pl.pallas_call(
        body, name="attn_prep",
        out_shape=(sds, sds, sds, sds),
        grid=(rows // tm, hp),
        in_specs=[pair_in(0), pair_in(1), pair_in(2), pl.BlockSpec((tm, heads), lambda i, p: (i, 0))],
        out_specs=(pair_out, pair_out, pair_out, pair_out),
        compiler_params=_params(("parallel", "parallel")),
    )(qkv, qkv, qkv, f_t)


def _attn_fwd(q_aug, k_aug, v_aug, d):
    rows = q_aug.shape[0]
    hp = d // CHUNK
    blk = _attn_block(rows)
    nb = rows // blk
    nt = (((1,), (1,)), ((), ()))
    den_lane = (HEAD_DIM, 0)

    def body(qi_ref, kj_ref, q_ref, k_ref, v_ref, o_ref, lse_ref, m_s, acc_s):
        i = qi_ref[pl.program_id(1)]
        j = kj_ref[pl.program_id(1)]

        @pl.when(j == 0)
        def _():
            m_s[...] = jnp.full(m_s.shape, NEG_INF, F32)
            acc_s[...] = jnp.zeros_like(acc_s)

        def step(masked):
            pair = [slice(hh * CHUNK, (hh + 1) * CHUNK) for hh in range(2)]
            scores = [lax.dot_general(q_ref[:, ln], k_ref[:, ln], nt, preferred_element_type=F32) for ln in pair]
            for hh in range(2):
                lanes = pair[hh]
                s = scores[hh]
                if masked:
                    s = _masked_logits(s, i, j, blk)
                m_prev = m_s[hh]
                m_new = jnp.maximum(m_prev, s.max(axis=-1, keepdims=True))
                p = jnp.exp(s - m_new)
                m_s[hh] = m_new
                pv = jnp.dot(p.astype(BF16), v_ref[:, lanes], preferred_element_type=F32)
                acc_s[hh] = acc_s[hh] * jnp.exp(m_prev - m_new) + pv

        edge = (j == i) | (j == 0)

        @pl.when(edge)
        def _():
            step(True)

        @pl.when(jnp.logical_not(edge))
        def _():
            step(False)

        @pl.when(j == i)
        def _():
            row = i * blk + lax.broadcasted_iota(jnp.int32, (blk, 1), 0)
            lo = _lane_lo()
            acc = [acc_s[hh] for hh in range(2)]
            den = [acc[hh][:, den_lane[hh]:den_lane[hh] + 1] for hh in range(2)]
            o = jnp.where(lo, acc[0] * (1.0 / den[0]), acc[1] * (1.0 / den[1]))
            o_ref[...] = jnp.where(row >= PAD_ROWS, o, 0.0)
            lse_ref[...] = jnp.where(lo, m_s[0] + jnp.log(den[0]), m_s[1] + jnp.log(den[1]))

    pairs = [(i, j) for i in range(nb) for j in range(i + 1)]
    q_spec = pl.BlockSpec((blk, 2 * CHUNK), lambda h, t, qi, kj: (qi[t], h))
    k_spec = pl.BlockSpec((blk, 2 * CHUNK), lambda h, t, qi, kj: (kj[t], h))
    o_spec = pl.BlockSpec((blk, CHUNK), lambda h, t, qi, kj: (qi[t], h))
    return pl.pallas_call(
        body, name="attn_fwd",
        out_shape=(jax.ShapeDtypeStruct((rows, d), F32), jax.ShapeDtypeStruct((rows, d), F32)),
        grid_spec=pltpu.PrefetchScalarGridSpec(
            num_scalar_prefetch=2, grid=(hp, len(pairs)),
            in_specs=[q_spec, k_spec, k_spec], out_specs=(o_spec, o_spec),
            scratch_shapes=[pltpu.VMEM((2, blk, 1), F32), pltpu.VMEM((2, blk, CHUNK), F32)]),
        compiler_params=_params(("parallel", "arbitrary")),
    )(jnp.array([p[0] for p in pairs], jnp.int32), jnp.array([p[1] for p in pairs], jnp.int32),
      q_aug, k_aug, v_aug)


def _attn_bwd(q_aug, k_aug, v_own, do, lse, delta, d):
    rows = q_aug.shape[0]
    hp = d // CHUNK
    blk = _attn_block(rows)
    nb = rows // blk
    scale = 1.0 / math.sqrt(HEAD_DIM)
    nt = (((1,), (1,)), ((), ()))
    tn = (((0,), (0,)), ((), ()))

    def body(kj_ref, qi_ref, q_ref, k_ref, v_ref, do_ref, lse_ref, dl_ref, dq_ref, dk_ref, dv_ref, df_ref, rs_ref,
             dq_s, dk_s, dv_s, df_s, rs_s):
        j = kj_ref[pl.program_id(1)]
        i = qi_ref[pl.program_id(1)]
        lo = _lane_lo()

        @pl.when((j == 0) & (i == 0))
        def _():
            dq_s[...] = jnp.zeros_like(dq_s)
            rs_s[...] = jnp.zeros_like(rs_s)

        @pl.when(i == j)
        def _():
            dk_s[...] = jnp.zeros_like(dk_s)
            dv_s[...] = jnp.zeros_like(dv_s)
            df_s[...] = jnp.zeros_like(df_s)

        def step(masked):
            dov = do_ref[...]
            t_dq, t_dk, t_dv, row_sums = [], [], [], []
            for hh in range(2):
                lanes = slice(hh * CHUNK, (hh + 1) * CHUNK)
                qh, kh = q_ref[:, lanes], k_ref[:, lanes]
                off = hh * HEAD_DIM
                s = lax.dot_general(qh, kh, nt, preferred_element_type=F32)
                if masked:
                    s = _masked_logits(s, i, j, blk)
                p = jnp.exp(s - lse_ref[:, off:off + 1])
                dp = lax.dot_general(dov, v_ref[:, lanes], nt, preferred_element_type=F32)
                ds = p * (dp - dl_ref[:, off:off + 1])
                df_s[hh:hh + 1, :] += ds.sum(axis=0, keepdims=True)
                row_sums.append(ds.sum(axis=1, keepdims=True))
                pb = p.astype(BF16)
                dsb = ds.astype(BF16)
                t_dv.append(lax.dot_general(pb, dov, tn, preferred_element_type=F32))
                t_dk.append(lax.dot_general(dsb, qh, tn, preferred_element_type=F32))
                t_dq.append(jnp.dot(dsb, kh, preferred_element_type=F32))
            dv_s[...] += jnp.where(lo, t_dv[0], t_dv[1])
            dk_s[...] += jnp.where(lo, t_dk[0], t_dk[1])
            r0 = pl.multiple_of(i * blk, blk)
            dq_s[pl.ds(r0, blk), :] += jnp.where(lo, t_dq[0], t_dq[1])
            rs_s[pl.ds(r0, blk), :] += jnp.where(lo, row_sums[0], row_sums[1])

        edge = (j == i) | (j == 0)

        @pl.when(edge)
        def _():
            step(True)

        @pl.when(jnp.logical_not(edge))
        def _():
            step(False)

        @pl.when(i == nb - 1)
        def _():
            dk_ref[...] = dk_s[...].astype(BF16)
            dv_ref[...] = dv_s[...].astype(BF16)
            df_ref[...] = -df_s[...]

        @pl.when((i == nb - 1) & (j == nb - 1))
        def _():
            dq_ref[...] = (dq_s[...] * scale).astype(BF16)
            rs_ref[...] = rs_s[...]

    pairs = [(j, i) for j in range(nb) for i in range(j, nb)]
    q_spec = pl.BlockSpec((blk, 2 * CHUNK), lambda h, t, kj, qi: (qi[t], h))
    kv_spec = pl.BlockSpec((blk, 2 * CHUNK), lambda h, t, kj, qi: (kj[t], h))
    row_spec = pl.BlockSpec((blk, CHUNK), lambda h, t, kj, qi: (qi[t], h))
    kv_out = pl.BlockSpec((blk, CHUNK), lambda h, t, kj, qi: (kj[t], h))
    dq_out = pl.BlockSpec((rows, CHUNK), lambda h, t, kj, qi: (0, h))
    return pl.pallas_call(
        body, name="attn_bwd",
        out_shape=(jax.ShapeDtypeStruct((rows, d), BF16), jax.ShapeDtypeStruct((rows, d), BF16),
                   jax.ShapeDtypeStruct((rows, d), BF16), jax.ShapeDtypeStruct((hp, 2, rows), F32),
                   jax.ShapeDtypeStruct((rows, d), F32)),
        grid_spec=pltpu.PrefetchScalarGridSpec(
            num_scalar_prefetch=2, grid=(hp, len(pairs)),
            in_specs=[q_spec, kv_spec, kv_spec, row_spec, row_spec, row_spec],
            out_specs=(dq_out, kv_out, kv_out,
                       pl.BlockSpec((None, 2, blk), lambda h, t, kj, qi: (h, 0, kj[t])), dq_out),
            scratch_shapes=[pltpu.VMEM((rows, CHUNK), F32), pltpu.VMEM((blk, CHUNK), F32),
                            pltpu.VMEM((blk, CHUNK), F32), pltpu.VMEM((2, blk), F32),
                            pltpu.VMEM((rows, CHUNK), F32)]),
        compiler_params=_params(("parallel", "arbitrary")),
    )(jnp.array([p[0] for p in pairs], jnp.int32), jnp.array([p[1] for p in pairs], jnp.int32),
      q_aug, k_aug, v_own, do, lse, delta)


def _band(w, transposed, other):
    r = lax.broadcasted_iota(jnp.int32, (CHUNK, CHUNK), 0)
    c = lax.broadcasted_iota(jnp.int32, (CHUNK, CHUNK), 1)
    dist = (c - r) if transposed else (r - c)
    if other:
        dist = dist + CHUNK
    return jnp.where(dist >= 0, jnp.where(dist < w, 1.0, 0.0), 0.0).astype(BF16)


def _inv_count(chunk_index, w):
    row = chunk_index * CHUNK + lax.broadcasted_iota(jnp.int32, (CHUNK, 1), 0)
    cnt = jnp.clip(row - PAD_ROWS + 1, 1, w).astype(F32)
    return 1.0 / cnt


def _pool_diff(u_cur, u_prev, i, w):
    ws = _apply01(_band(w, False, False), u_cur, 3)
    ws = ws + jnp.where(i > 0, _apply01(_band(w, False, True), u_prev, 3), 0.0)
    return ws * _inv_count(i, w) - u_cur


def _pool_merge_fwd(rest, o, w_pool, scale, d):
    rows = rest.shape[0]
    n = rows // CHUNK
    cg = d // len(POOL_WINDOWS)

    def body(up_ref, uc_ref, gp_ref, ga_ref, o_ref, wp_ref, sc_ref, mg_ref, yp_ref):
        i = pl.program_id(0)
        for g, w in enumerate(POOL_WINDOWS):
            sl = slice(g * cg, (g + 1) * cg)
            diff = _pool_diff(uc_ref[:, sl], up_ref[:, sl], i, w)
            ypre = jnp.dot(diff.astype(BF16), wp_ref[g], preferred_element_type=F32)
            yp_ref[:, sl] = ypre
            merged = (jax.nn.sigmoid(gp_ref[:, sl]) * (ypre * sc_ref[:, sl])
                      + jax.nn.sigmoid(ga_ref[:, sl]) * o_ref[:, sl])
            mg_ref[:, sl] = merged.astype(BF16)

    col = lambda c: pl.BlockSpec((CHUNK, d), lambda i: (i, c))
    return pl.pallas_call(
        body, name="pool_merge_fwd",
        out_shape=(jax.ShapeDtypeStruct((rows, d), BF16), jax.ShapeDtypeStruct((rows, d), F32)),
        grid=(n,),
        in_specs=[pl.BlockSpec((CHUNK, d), lambda i: (jnp.maximum(i - 1, 0), 0)), col(0), col(1), col(2), col(0),
                  pl.BlockSpec((len(POOL_WINDOWS), cg, cg), lambda i: (0, 0, 0)),
                  pl.BlockSpec((1, d), lambda i: (0, 0))],
        out_specs=(col(0), col(0)),
        compiler_params=_params(("parallel",)),
    )(rest, rest, rest, rest, o, w_pool, scale.reshape(1, d))


def _gate_bwd(dm, rest, o, ypre, scale, d):
    rows = rest.shape[0]
    n = rows // CHUNK

    def body(dm_ref, gp_ref, ga_ref, o_ref, yp_ref, sc_ref, dgp_ref, dga_ref, do_ref, dl_ref, dy_ref, ds_ref):
        @pl.when(pl.program_id(0) == 0)
        def _():
            ds_ref[...] = jnp.zeros_like(ds_ref)

        dmv = dm_ref[...]
        sp = jax.nn.sigmoid(gp_ref[...])
        sa = jax.nn.sigmoid(ga_ref[...])
        ov = o_ref[...]
        ypre_v = yp_ref[...]
        sc = sc_ref[...]
        dgp_ref[...] = (dmv * (ypre_v * sc) * (sp * (1.0 - sp))).astype(BF16)
        dga_ref[...] = (dmv * ov * (sa * (1.0 - sa))).astype(BF16)
        t = dmv * sp
        dy_ref[...] = (t * sc).astype(BF16)
        ds_ref[...] += _fold8(t * ypre_v)
        dob = (dmv * sa).astype(BF16)
        do_ref[...] = dob
        prod = dob.astype(F32) * ov
        lo = _lane_lo()
        for pr in range(d // CHUNK):
            sl = slice(pr * CHUNK, (pr + 1) * CHUNK)
            tp = prod[:, sl]
            s_lo = jnp.where(lo, tp, 0.0).sum(axis=-1, keepdims=True)
            s_hi = jnp.where(lo, 0.0, tp).sum(axis=-1, keepdims=True)
            dl_ref[:, sl] = jnp.where(lo, s_lo, s_hi)

    col = lambda c: pl.BlockSpec((CHUNK, d), lambda i: (i, c))
    row_bf = jax.ShapeDtypeStruct((rows, d), BF16)
    outs = pl.pallas_call(
        body, name="gate_bwd",
        out_shape=(row_bf, row_bf, row_bf, jax.ShapeDtypeStruct((rows, d), F32), row_bf,
                   jax.ShapeDtypeStruct((8, d), F32)),
        grid=(n,),
        in_specs=[col(0), col(1), col(2), col(0), col(0), pl.BlockSpec((1, d), lambda i: (0, 0))],
        out_specs=(col(0), col(0), col(0), col(0), col(0), pl.BlockSpec((8, d), lambda i: (0, 0))),
        compiler_params=_params(("arbitrary",)),
    )(dm, rest, rest, o, ypre, scale.reshape(1, d))
    return outs[:5] + (outs[5].sum(axis=0),)


def _pool_bwd(dypre, rest, w_pool, d):
    rows = rest.shape[0]
    n = rows // CHUNK
    ng = len(POOL_WINDOWS)
    cg = d // ng
    nt = (((1,), (1,)), ((), ()))
    tn = (((0,), (0,)), ((), ()))

    def body(dc_ref, dn_ref, up_ref, uc_ref, wp_ref, du_ref, dw_ref):
        i = pl.program_id(0)

        @pl.when(i == 0)
        def _():
            dw_ref[...] = jnp.zeros_like(dw_ref)

        row = i * CHUNK + lax.broadcasted_iota(jnp.int32, (CHUNK, 1), 0)
        for g, w in enumerate(POOL_WINDOWS):
            sl = slice(g * cg, (g + 1) * cg)
            diff = _pool_diff(uc_ref[:, sl], up_ref[:, sl], i, w)
            dyc = dc_ref[:, sl]
            dw_ref[g] += lax.dot_general(diff.astype(BF16), dyc, tn, preferred_element_type=F32)
            wg = wp_ref[g]
            dd_cur = lax.dot_general(dyc, wg, nt, preferred_element_type=F32)
            dd_next = lax.dot_general(dn_ref[:, sl], wg, nt, preferred_element_type=F32)
            du = _apply01(_band(w, True, False), dd_cur * _inv_count(i, w), 2)
            du = du + jnp.where(i < n - 1, _apply01(_band(w, True, True), dd_next * _inv_count(i + 1, w), 2), 0.0)
            du = du - dd_cur
            du_ref[:, sl] = jnp.where(row >= PAD_ROWS, du, 0.0).astype(BF16)

    cur = pl.BlockSpec((CHUNK, d), lambda i: (i, 0))
    return pl.pallas_call(
        body, name="pool_bwd",
        out_shape=(jax.ShapeDtypeStruct((rows, d), BF16), jax.ShapeDtypeStruct((ng, cg, cg), F32)),
        grid=(n,),
        in_specs=[cur, pl.BlockSpec((CHUNK, d), lambda i: (jnp.minimum(i + 1, n - 1), 0)),
                  pl.BlockSpec((CHUNK, d), lambda i: (jnp.maximum(i - 1, 0), 0)), cur,
                  pl.BlockSpec((ng, cg, cg), lambda i: (0, 0, 0))],
        out_specs=(cur, pl.BlockSpec((ng, cg, cg), lambda i: (0, 0, 0))),
        compiler_params=_params(("arbitrary",)),
    )(dypre, dypre, rest, rest, w_pool)


def _loss_grad(h_res, target):
    rows, d = h_res.shape
    n = rows // CHUNK

    def body(h_ref, t_ref, dh_ref, acc_ref):
        i = pl.program_id(0)

        @pl.when(i == 0)
        def _():
            acc_ref[...] = jnp.zeros_like(acc_ref)
            dh_ref[...] = jnp.zeros_like(dh_ref)

        @pl.when(i > 0)
        def _():
            err = h_ref[...] - t_ref[...]
            dh_ref[...] = err * (1.0 / d)
            e2 = _fold8(err * err)
            part = e2[:, 0:CHUNK]
            for c in range(1, d // CHUNK):
                part = part + e2[:, c * CHUNK:(c + 1) * CHUNK]
            acc_ref[...] += part

    dh, acc = pl.pallas_call(
        body, name="loss_grad",
        out_shape=(jax.ShapeDtypeStruct((rows, d), F32), jax.ShapeDtypeStruct((8, CHUNK), F32)),
        grid=(n,),
        in_specs=[pl.BlockSpec((CHUNK, d), lambda i: (i, 0)),
                  pl.BlockSpec((CHUNK, d), lambda i: (jnp.maximum(i - 1, 0), 0))],
        out_specs=(pl.BlockSpec((CHUNK, d), lambda i: (i, 0)), pl.BlockSpec((8, CHUNK), lambda i: (0, 0))),
        compiler_params=_params(("arbitrary",)),
    )(h_res, target)
    return dh, (0.5 / d) * acc.sum()


def _adamw(slots, w, m, v, name):
    rows, cols = w.shape
    lanes = -(-cols // CHUNK) * CHUNK
    row_bytes = lanes * (N_DEV * slots.dtype.itemsize + 7 * 4)
    tr = max(t for t in _divisors(rows, 8) if t <= max(8, ADAM_TILE_BYTES // row_bytes))
    c1 = 1.0 - ADAM_B1 ** ADAM_STEP
    c2 = 1.0 - ADAM_B2 ** ADAM_STEP

    def body(s_ref, w_ref, m_ref, v_ref, g_ref, d_ref, mo_ref, vo_ref):
        g = s_ref[0].astype(F32)
        for k in range(1, N_DEV):
            g = g + s_ref[k].astype(F32)
        m_new = ADAM_B1 * m_ref[...] + (1.0 - ADAM_B1) * g
        v_new = ADAM_B2 * v_ref[...] + (1.0 - ADAM_B2) * (g * g)
        m_hat = m_new / c1
        v_hat = v_new / c2
        g_ref[...] = g
        d_ref[...] = -ADAM_LR * (m_hat / (jnp.sqrt(v_hat) + ADAM_EPS) + ADAM_WD * w_ref[...])
        mo_ref[...] = m_new
        vo_ref[...] = v_new

    tile = pl.BlockSpec((tr, cols), lambda i: (i, 0))
    sds = jax.ShapeDtypeStruct((rows, cols), F32)
    return pl.pallas_call(
        body, name=name,
        out_shape=(sds, sds, sds, sds),
        grid=(rows // tr,),
        in_specs=[pl.BlockSpec((N_DEV, tr, cols), lambda i: (0, i, 0)), tile, tile, tile],
        out_specs=(tile, tile, tile, tile),
        compiler_params=_params(("parallel",)),
    )(slots, w, m, v)


def _exchange(srcs, gather, name):
    n = len(srcs)
    shapes = [((N_DEV,) + s.shape) if gather else s.shape for s in srcs]

    def body(*refs):
        src_refs, out_refs = refs[:n], refs[n:2 * n]
        send_sems, recv_sems, local_sems = refs[2 * n:]
        x, y, c = lax.axis_index("x"), lax.axis_index("y"), lax.axis_index("c")
        me = 4 * x + 2 * y + c

        def payload(a, slot):
            return src_refs[a] if gather else src_refs[a].at[slot]

        own = [pltpu.make_async_copy(payload(a, me), out_refs[a].at[me], local_sems.at[a]) for a in range(n)]
        for cp in own:
            cp.start()
        sends, recvs = [], []
        for k in range(1, N_DEV):
            px = 1 - x if k & 4 else x
            py = 1 - y if k & 2 else y
            pc = 1 - c if k & 1 else c
            peer = 4 * px + 2 * py + pc
            for a in range(n):
                sems = dict(send_sem=send_sems.at[(k - 1) * n + a], recv_sem=recv_sems.at[(k - 1) * n + a],
                            device_id=(px, py, pc), device_id_type=MESH_ID)
                sends.append(pltpu.make_async_remote_copy(src_ref=payload(a, peer), dst_ref=out_refs[a].at[me], **sems))
                recvs.append(pltpu.make_async_remote_copy(src_ref=payload(a, peer), dst_ref=out_refs[a].at[peer], **sems))
        for cp in sends:
            cp.start()
        for cp in recvs:
            cp.wait_recv()
        for cp in sends:
            cp.wait_send()
        for cp in own:
            cp.wait()

    return pl.pallas_call(
        body, name=name,
        out_shape=tuple(jax.ShapeDtypeStruct(sh, s.dtype) for sh, s in zip(shapes, srcs)),
        in_specs=[pl.BlockSpec(memory_space=pl.ANY)] * n,
        out_specs=tuple([pl.BlockSpec(memory_space=pl.ANY)] * n),
        scratch_shapes=[pltpu.SemaphoreType.DMA(((N_DEV - 1) * n,)), pltpu.SemaphoreType.DMA(((N_DEV - 1) * n,)),
                        pltpu.SemaphoreType.DMA((n,))],
    )(*srcs)


HBM_SPEC = pl.BlockSpec(memory_space=pltpu.HBM)
SEM_SPEC = pl.BlockSpec(memory_space=pltpu.SEMAPHORE)
DATAFLOW = pltpu.SideEffectType.DATAFLOW_SIDE_EFFECTING


def _peer_copies(src_refs, land_refs, send_sems, recv_sems, gather):
    n = len(src_refs)
    x, y, c = lax.axis_index("x"), lax.axis_index("y"), lax.axis_index("c")
    me = 4 * x + 2 * y + c
    sends, lands = [], []
    for k in range(1, N_DEV):
        px = 1 - x if k & 4 else x
        py = 1 - y if k & 2 else y
        pc = 1 - c if k & 1 else c
        peer = 4 * px + 2 * py + pc
        for a in range(n):
            src = src_refs[a] if gather else src_refs[a].at[peer]
            sems = dict(send_sem=send_sems.at[(k - 1) * n + a], recv_sem=recv_sems.at[(k - 1) * n + a],
                        device_id=(px, py, pc), device_id_type=MESH_ID)
            sends.append(pltpu.make_async_remote_copy(src_ref=src, dst_ref=land_refs[a].at[me], **sems))
            lands.append(pltpu.make_async_remote_copy(src_ref=src, dst_ref=land_refs[a].at[peer], **sems))
    return sends, lands


def _exchange_start(srcs, gather, name):
    n = len(srcs)
    shapes = [((N_DEV,) + s.shape) if gather else s.shape for s in srcs]
    n_sem = (N_DEV - 1) * n

    def body(*refs):
        src_refs, land_refs = refs[:n], refs[n:2 * n]
        send_sems, recv_sems = refs[2 * n], refs[2 * n + 1]
        token = refs[-1]
        sends, _ = _peer_copies(src_refs, land_refs, send_sems, recv_sems, gather)
        for cp in sends:
            cp.start()
        token[...] = jnp.zeros_like(token)

    hbm = lambda arrays_shapes: [pltpu.HBM(sh, dt) for sh, dt in arrays_shapes]
    src_types = [(s.shape, s.dtype) for s in srcs]
    land_types = [(sh, s.dtype) for sh, s in zip(shapes, srcs)]
    outs = pl.pallas_call(
        body, name=name,
        out_shape=(pltpu.SemaphoreType.DMA((n_sem,)), pltpu.SemaphoreType.DMA((n_sem,)),
                   *hbm(src_types), *hbm(land_types), jax.ShapeDtypeStruct((8, CHUNK), F32)),
        in_specs=[HBM_SPEC] * (2 * n),
        out_specs=(SEM_SPEC, SEM_SPEC, *([HBM_SPEC] * (2 * n)), pl.BlockSpec(memory_space=pltpu.VMEM)),
        input_output_aliases={a: 2 + a for a in range(2 * n)},
        compiler_params=pltpu.CompilerParams(has_side_effects=DATAFLOW),
    )(*[pltpu.with_memory_space_constraint(s, pltpu.HBM) for s in srcs],
      *[pltpu.with_memory_space_constraint(lax.empty(sh, dt), pltpu.HBM) for sh, dt in land_types])
    return (outs[0], outs[1], outs[2:2 + n], outs[2 + n:2 + 2 * n]), outs[-1]


def _exchange_wait(handle, after, gather, name):
    send_sems, recv_sems, src_thru, land_thru = handle
    n = len(src_thru)

    def body(*refs):
        src_refs, land_refs = refs[:n], refs[n:2 * n]
        _, lands = _peer_copies(src_refs, land_refs, refs[2 * n], refs[2 * n + 1], gather)
        for cp in lands:
            cp.wait_send()
            cp.wait_recv()

    outs = pl.pallas_call(
        body, name=name,
        out_shape=tuple(pltpu.HBM(t.shape, t.dtype) for t in (*src_thru, *land_thru)),
        in_specs=[*([HBM_SPEC] * (2 * n)), SEM_SPEC, SEM_SPEC, pl.BlockSpec(memory_space=pl.ANY)],
        out_specs=tuple([HBM_SPEC] * (2 * n)),
        input_output_aliases={a: a for a in range(2 * n)},
        compiler_params=pltpu.CompilerParams(has_side_effects=DATAFLOW),
    )(*src_thru, *land_thru, send_sems, recv_sems, after)
    return outs[n:]


def _fill_own(land, own, me):
    return lax.dynamic_update_slice_in_dim(land, own[None].astype(land.dtype), me, axis=0)


BIG = ("w_in", "w_pool", "w_out", "w_gate", "w_up", "w_down")
SHARD_AXIS = dict(w_in=-1, w_pool=-2, w_out=-2, w_gate=-1, w_up=-1, w_down=-2)
SMALL = ("norm_mix_pre", "norm_mix_post", "norm_ffn_pre", "norm_ffn_post", "pool_scale")


def _join(g, axis):
    return jnp.concatenate([g[j] for j in range(N_DEV)], axis=axis)


def _split(full, axis):
    return jnp.stack(jnp.split(full, N_DEV, axis=axis))


def _rows2d(a):
    return a.reshape(-1, a.shape[-1])


def kernel(x, meta_tokens, norm_mix_pre, norm_mix_post, norm_ffn_pre, norm_ffn_post, w_in, b_forget, w_pool, pool_scale, w_out, w_ffn_gate, w_ffn_up, w_ffn_down, loss_target, m_meta_tokens, m_norm_mix_pre, m_norm_mix_post, m_norm_ffn_pre, m_norm_ffn_post, m_w_in, m_b_forget, m_w_pool, m_pool_scale, m_w_out, m_w_ffn_gate, m_w_ffn_up, m_w_ffn_down, v_meta_tokens, v_norm_mix_pre, v_norm_mix_post, v_norm_ffn_pre, v_norm_ffn_post, v_w_in, v_b_forget, v_w_pool, v_pool_scale, v_w_out, v_w_ffn_gate, v_w_ffn_up, v_w_ffn_down):
    x2 = x[0]
    target = loss_target[0]
    seq, d = x2.shape
    depth = w_in.shape[0]
    heads = d // HEAD_DIM
    ff = w_ffn_gate.shape[2] * N_DEV
    rows = PAD_ROWS + META_TOKENS + seq
    assert seq % CHUNK == 0 and d % (2 * CHUNK) == 0 and heads <= FORGET_PAD and depth >= 2
    me = 4 * lax.axis_index("x") + 2 * lax.axis_index("y") + lax.axis_index("c")

    big = dict(w_in=w_in, w_pool=w_pool, w_out=w_out, w_gate=w_ffn_gate, w_up=w_ffn_up, w_down=w_ffn_down)
    big_m = dict(w_in=m_w_in, w_pool=m_w_pool, w_out=m_w_out, w_gate=m_w_ffn_gate, w_up=m_w_ffn_up, w_down=m_w_ffn_down)
    big_v = dict(w_in=v_w_in, w_pool=v_w_pool, w_out=v_w_out, w_gate=v_w_ffn_gate, w_up=v_w_ffn_up, w_down=v_w_ffn_down)
    small = dict(norm_mix_pre=norm_mix_pre, norm_mix_post=norm_mix_post, norm_ffn_pre=norm_ffn_pre,
                 norm_ffn_post=norm_ffn_post, pool_scale=pool_scale)
    small_m = dict(norm_mix_pre=m_norm_mix_pre, norm_mix_post=m_norm_mix_post, norm_ffn_pre=m_norm_ffn_pre,
                   norm_ffn_post=m_norm_ffn_post, pool_scale=m_pool_scale)
    small_v = dict(norm_mix_pre=v_norm_mix_pre, norm_mix_post=v_norm_mix_post, norm_ffn_pre=v_norm_ffn_pre,
                   norm_ffn_post=v_norm_ffn_post, pool_scale=v_pool_scale)

    wire = [big[n].astype(BF16) for n in BIG]
    gathered = _exchange([w[:1] for w in wire] + [meta_tokens], True, "gather_first")
    meta_full = _join(gathered[-1], -1)
    later, started = _exchange_start([w[1:] for w in wire], True, "gather_later_start")

    def layer_weights(lands):
        full = {n: _join(g, SHARD_AXIS[n]) for n, g in zip(BIG, lands)}
        win = full["w_in"]
        fcol = 4 * d
        qkv_w = win[:, :, d:4 * d]
        rest_w = jnp.concatenate([win[:, :, :d], win[:, :, fcol + heads:], win[:, :, fcol:fcol + heads],
                                  jnp.zeros(win.shape[:2] + (FORGET_PAD - heads,), BF16)], axis=2)
        return dict(qkv=qkv_w, rest=rest_w, cat=jnp.concatenate([qkv_w, rest_w], axis=2),
                    gu=jnp.concatenate([full["w_gate"], full["w_up"]], axis=2),
                    o=full["w_out"], dn=full["w_down"], pl=full["w_pool"])

    groups = [layer_weights(gathered[:-1])]
    weight = lambda key, l: groups[min(l, 1)][key][l - min(l, 1)]
    w_qkv, w_rest, w_cat, w_gu, w_o, w_dn, w_pl = (
        [None] * depth for _ in range(7))

    h_res = jnp.concatenate([jnp.zeros((PAD_ROWS, d), F32), meta_full, x2], axis=0)
    h1 = _norm_fwd(h_res, norm_mix_pre[0] + started[0, 0])
    ones = jnp.ones((d,), F32)
    saved = []
    for l in range(depth):
        if l == 1:
            lands = _exchange_wait(later, h_res, True, "gather_later_wait")
            groups.append(layer_weights([_fill_own(g, w[1:], me) for g, w in zip(lands, wire)]))
        for table_, key in ((w_qkv, "qkv"), (w_rest, "rest"), (w_cat, "cat"), (w_gu, "gu"), (w_o, "o"),
                            (w_dn, "dn"), (w_pl, "pl")):
            table_[l] = weight(key, l)
        qkv = _matmul(h1, w_qkv[l], "nn", BF16, "proj_qkv")
        rest = _matmul(h1, w_rest[l], "nn", F32, "proj_rest")
        z = rest[:, 3 * d:3 * d + heads].T
        bias = jnp.broadcast_to(b_forget[l][:, None], (heads, CHUNK))
        q_aug, k_aug, v_own, v_aug = _attn_prep(qkv, _logf_fwd(z, bias).T, d)
        o, lse = _attn_fwd(q_aug, k_aug, v_aug, d)
        merged, ypre = _pool_merge_fwd(rest, o, w_pl[l], pool_scale[l], d)
        mix = _matmul(merged, w_o[l], "nn", F32, "mix_out")
        h_mid, h2 = _resid_norm_fwd(h_res, mix, norm_mix_post[l], norm_ffn_pre[l])
        ab = _matmul(h2, w_gu[l], "nn", F32, "ffn_in")
        act = _swiglu_fwd(ab)
        ffo = _matmul(act, w_dn[l], "nn", F32, "ffn_out")
        g_next = norm_mix_pre[l + 1] if l + 1 < depth else ones
        h_next, h1_next = _resid_norm_fwd(h_mid, ffo, norm_ffn_post[l], g_next)
        saved.append(dict(h_in=h_res, h1=h1, q_aug=q_aug, k_aug=k_aug, v_own=v_own, rest=rest, z=z,
                          bias=bias, o=o, lse=lse,
                          merged=merged, ypre=ypre, mix=mix, h_mid=h_mid, h2=h2, ab=ab, act=act, ffo=ffo))
        h_res, h1 = h_next, h1_next

    dh, loss_local = _loss_grad(h_res, target)
    loss = lax.psum(loss_local, ("x", "y", "c"))

    grads = {n: [None] * depth for n in BIG + SMALL + ("b_forget",)}

    def grad_slots(first, last):
        parts = [_split(jnp.stack(grads[n][first:last]), SHARD_AXIS[n]).astype(BF16) for n in BIG]
        return [p.reshape(N_DEV, -1, p.shape[-1]) for p in parts]

    for l in reversed(range(depth)):
        s = saved[l]
        g_post = norm_ffn_post[l]
        if l == 0:
            early_slots = grad_slots(1, depth)
            early, started = _exchange_start(early_slots, False, "scatter_early_start")
            g_post = g_post + started[0, 0]
        dffo, grads["norm_ffn_post"][l] = _norm_bwd(s["ffo"], g_post, dh, None, BF16)
        grads["w_down"][l] = _matmul(s["act"], dffo, "tn", F32, "grad_w_down")
        dact = _matmul(dffo, w_dn[l], "nt", F32, "ffn_out_dx")
        dab = _swiglu_bwd(s["ab"], dact)
        dgu = _matmul(s["h2"], dab, "tn", F32, "grad_w_gu")
        grads["w_gate"][l], grads["w_up"][l] = dgu[:, :ff], dgu[:, ff:]
        dh2 = _matmul(dab, w_gu[l], "nt", F32, "ffn_in_dx")
        dh_mid, grads["norm_ffn_pre"][l] = _norm_bwd(s["h_mid"], norm_ffn_pre[l], dh2, dh, F32)

        dmix, grads["norm_mix_post"][l] = _norm_bwd(s["mix"], norm_mix_post[l], dh_mid, None, BF16)
        grads["w_out"][l] = _matmul(s["merged"], dmix, "tn", F32, "grad_w_out")
        dm = _matmul(dmix, w_o[l], "nt", F32, "mix_out_dx")
        dgp, dga, do, delta, dypre, grads["pool_scale"][l] = _gate_bwd(dm, s["rest"], s["o"], s["ypre"], pool_scale[l], d)
        du, grads["w_pool"][l] = _pool_bwd(dypre, s["rest"], w_pl[l], d)
        dq, dk, dv, df_key, df_query = _attn_bwd(s["q_aug"], s["k_aug"], s["v_own"], do, s["lse"], delta, d)
        df = df_key.reshape(heads, rows) + df_query.reshape(rows, heads, HEAD_DIM)[:, :, 0].T
        dz, grads["b_forget"][l] = _logf_bwd(df, s["z"], s["bias"])
        dzt = jnp.pad(dz.T.astype(BF16), ((0, 0), (0, FORGET_PAD - heads)))
        dproj = jnp.concatenate([dq, dk, dv, du, dgp, dga, dzt], axis=1)
        dwc = _matmul(s["h1"], dproj, "tn", F32, "grad_w_in")
        grads["w_in"][l] = jnp.concatenate([dwc[:, 3 * d:4 * d], dwc[:, :3 * d], dwc[:, 6 * d:6 * d + heads],
                                            dwc[:, 4 * d:6 * d]], axis=1)
        dh1 = _matmul(dproj, w_cat[l], "nt", F32, "proj_dx")
        dh, grads["norm_mix_pre"][l] = _norm_bwd(s["h_in"], norm_mix_pre[l], dh1, dh_mid, F32)

    grad_x = dh[PAD_ROWS + META_TOKENS:][None]
    dmeta = dh[PAD_ROWS:PAD_ROWS + META_TOKENS]

    late_recv = _exchange(grad_slots(0, 1), False, "scatter_first")
    early_recv = _exchange_wait(early, dh, False, "scatter_early_wait")
    early_recv = [_fill_own(r, lax.dynamic_index_in_dim(s_, me, 0, keepdims=False), me)
                  for r, s_ in zip(early_recv, early_slots)]
    recv = [jnp.concatenate([a, b], axis=1) for a, b in zip(late_recv, early_recv)]
    big_out = {}
    for n, r in zip(BIG, recv):
        outs = _adamw(r, _rows2d(big[n]), _rows2d(big_m[n]), _rows2d(big_v[n]), "adamw_" + n)
        big_out[n] = [o_.reshape(big[n].shape) for o_ in outs]

    def table(parts, forget):
        t = jnp.concatenate([_rows2d(p) for p in parts] + [jnp.pad(forget, ((0, 0), (0, d - heads)))], axis=0)
        return jnp.pad(t, ((0, -t.shape[0] % 8), (0, 0)))

    g_table = table([jnp.stack(grads[n]) for n in SMALL], jnp.stack(grads["b_forget"]))
    rep_rows = g_table.shape[0]
    got = _exchange([jnp.concatenate([g_table, dmeta], axis=0)], True, "gather_small_grads")[0]
    outs = _adamw(got[:, :rep_rows], table([small[n] for n in SMALL], b_forget),
                  table([small_m[n] for n in SMALL], m_b_forget), table([small_v[n] for n in SMALL], v_b_forget),
                  "adamw_small")
    dcols = d // N_DEV
    meta_slots = lax.dynamic_slice_in_dim(got[:, rep_rows:], me * dcols, dcols, axis=2)
    meta_out = _adamw(meta_slots, meta_tokens, m_meta_tokens, v_meta_tokens, "adamw_meta")

    def ordered(k):
        t = outs[k]
        so = {n: t[a * depth:(a + 1) * depth] for a, n in enumerate(SMALL)}
        forget = t[len(SMALL) * depth:(len(SMALL) + 1) * depth, :heads]
        return (meta_out[k], so["norm_mix_pre"], so["norm_mix_post"], so["norm_ffn_pre"], so["norm_ffn_post"],
                big_out["w_in"][k], forget, big_out["w_pool"][k], so["pool_scale"], big_out["w_out"][k],
                big_out["w_gate"][k], big_out["w_up"][k], big_out["w_down"][k])

    return (loss, grad_x) + ordered(0) + ordered(1) + ordered(2) + ordered(3)
```

```python
import math

import jax
import jax.numpy as jnp
from jax import lax
from jax.experimental import pallas as pl
from jax.experimental.pallas import tpu as pltpu

F32 = jnp.float32
BF16 = jnp.bfloat16

N_DEV = 8
META_TOKENS = 16
PAD_ROWS = 112
CHUNK = 128
HEAD_DIM = 64
POOL_WINDOWS = (2, 4, 8, 16)
FORGET_PAD = 256
RMS_EPS = 1e-6
NEG_INF = -1e30
ADAM_LR, ADAM_B1, ADAM_B2, ADAM_EPS, ADAM_WD, ADAM_STEP = 0.001, 0.9, 0.999, 1e-08, 0.01, 10

VMEM_LIMIT = 56 * 1024 * 1024
VMEM_TILE_BUDGET = 36 * 1024 * 1024
ADAM_TILE_BYTES = 8 * 1024 * 1024
MESH_ID = pl.DeviceIdType.MESH


def _params(sem, vmem=VMEM_LIMIT):
    return pltpu.CompilerParams(dimension_semantics=sem, vmem_limit_bytes=vmem)


def _divisors(n, mult):
    return [d for d in range(mult, n + 1, mult) if n % d == 0]


def _row_tile(rows, cap):
    return max(d for d in _divisors(rows, CHUNK) if d <= max(cap, CHUNK))


def _fold8(x):
    r, c = x.shape
    return x.reshape(r // 8, 8, c).sum(axis=0)


def _split_bf16(x, parts):
    out = []
    for _ in range(parts - 1):
        hi = x.astype(BF16)
        out.append(hi)
        x = x - hi.astype(F32)
    out.append(x.astype(BF16))
    return out


def _apply01(mat, x, parts, left=True):
    acc = None
    for p in _split_bf16(x, parts):
        t = jnp.dot(mat, p, preferred_element_type=F32) if left else jnp.dot(p, mat, preferred_element_type=F32)
        acc = t if acc is None else acc + t
    return acc


def _matmul_tiles(m, n, k, mode, out_bytes):
    if mode == "tn":
        tk = _row_tile(k, 640)
    else:
        tk = max(d for d in _divisors(k, CHUNK) if d <= 1536)
    nk = k // tk
    best = None
    m_opts = _divisors(m, CHUNK)
    n_opts = _divisors(n, CHUNK)
    for tm in m_opts:
        for tn in n_opts:
            need = 2 * 2 * (tm * tk + tk * tn) + 2 * tm * tn * out_bytes
            if nk > 1 or mode == "tn":
                need += tm * tn * 4
            need += tm * tn * 4
            if need > VMEM_TILE_BUDGET:
                continue
            key = (tm * tn, tn)
            if best is None or key > best[0]:
                best = (key, tm, tn)
    return best[1], best[2], tk


def _matmul(a, b, mode, out_dtype, name):
    if mode == "nn":
        (m, k), (k2, n) = a.shape, b.shape
    elif mode == "nt":
        (m, k), (n, k2) = a.shape, b.shape
    else:
        (k, m), (k2, n) = a.shape, b.shape
    assert k == k2 and a.dtype == BF16 and b.dtype == BF16
    tm, tn, tk = _matmul_tiles(m, n, k, mode, jnp.dtype(out_dtype).itemsize)
    nk = k // tk
    if mode == "nn":
        a_spec = pl.BlockSpec((tm, tk), lambda i, j, r: (i, r))
        b_spec = pl.BlockSpec((tk, tn), lambda i, j, r: (r, j))
        dims = (((1,), (0,)), ((), ()))
    elif mode == "nt":
        a_spec = pl.BlockSpec((tm, tk), lambda i, j, r: (i, r))
        b_spec = pl.BlockSpec((tn, tk), lambda i, j, r: (j, r))
        dims = (((1,), (1,)), ((), ()))
    else:
        a_spec = pl.BlockSpec((tk, tm), lambda i, j, r: (r, i))
        b_spec = pl.BlockSpec((tk, tn), lambda i, j, r: (r, j))
        dims = (((0,), (0,)), ((), ()))

    def body(a_ref, b_ref, o_ref, *acc):
        part = lax.dot_general(a_ref[...], b_ref[...], dims, preferred_element_type=F32)
        if nk == 1:
            o_ref[...] = part.astype(o_ref.dtype)
        else:
            r = pl.program_id(2)

            @pl.when(r == 0)
            def _():
                acc[0][...] = part

            @pl.when(r > 0)
            def _():
                acc[0][...] += part

            @pl.when(r == nk - 1)
            def _():
                o_ref[...] = acc[0][...].astype(o_ref.dtype)

    return pl.pallas_call(
        body, name=name,
        out_shape=jax.ShapeDtypeStruct((m, n), out_dtype),
        grid=(m // tm, n // tn, nk),
        in_specs=[a_spec, b_spec],
        out_specs=pl.BlockSpec((tm, tn), lambda i, j, r: (i, j)),
        scratch_shapes=[pltpu.VMEM((tm, tn), F32)] if nk > 1 else [],
        compiler_params=_params(("parallel", "parallel", "arbitrary")),
    )(a, b)


def _rms(x, g):
    rstd = lax.rsqrt(jnp.mean(x * x, axis=-1, keepdims=True) + RMS_EPS)
    return x * rstd * g


def _norm_fwd(x, g):
    rows, d = x.shape
    tm = _row_tile(rows, 640)

    def body(x_ref, g_ref, h_ref):
        h_ref[...] = _rms(x_ref[...], g_ref[...]).astype(BF16)

    return pl.pallas_call(
        body, name="norm_fwd",
        out_shape=jax.ShapeDtypeStruct((rows, d), BF16),
        grid=(rows // tm,),
        in_specs=[pl.BlockSpec((tm, d), lambda i: (i, 0)), pl.BlockSpec((1, d), lambda i: (0, 0))],
        out_specs=pl.BlockSpec((tm, d), lambda i: (i, 0)),
        compiler_params=_params(("parallel",)),
    )(x, g.reshape(1, d))


def _resid_norm_fwd(h_res, y, g_post, g_next):
    rows, d = h_res.shape
    tm = _row_tile(rows, 640)

    def body(r_ref, y_ref, gp_ref, gn_ref, hn_ref, hx_ref):
        h_new = r_ref[...] + _rms(y_ref[...], gp_ref[...])
        hn_ref[...] = h_new
        hx_ref[...] = _rms(h_new, gn_ref[...]).astype(BF16)

    row = pl.BlockSpec((tm, d), lambda i: (i, 0))
    vec = pl.BlockSpec((1, d), lambda i: (0, 0))
    return pl.pallas_call(
        body, name="resid_norm_fwd",
        out_shape=(jax.ShapeDtypeStruct((rows, d), F32), jax.ShapeDtypeStruct((rows, d), BF16)),
        grid=(rows // tm,),
        in_specs=[row, row, vec, vec],
        out_specs=(row, row),
        compiler_params=_params(("parallel",)),
    )(h_res, y, g_post.reshape(1, d), g_next.reshape(1, d))


def _norm_bwd(x, g, dy, resid, out_dtype):
    rows, d = x.shape
    tm = _row_tile(rows, 640)
    has_resid = resid is not None

    def body(*refs):
        if has_resid:
            x_ref, g_ref, dy_ref, r_ref, dx_ref, dg_ref = refs
        else:
            x_ref, g_ref, dy_ref, dx_ref, dg_ref = refs
        xv = x_ref[...]
        dyv = dy_ref[...].astype(F32)
        rstd = lax.rsqrt(jnp.mean(xv * xv, axis=-1, keepdims=True) + RMS_EPS)
        xhat = xv * rstd
        gdy = dyv * g_ref[...]
        dx = rstd * (gdy - xhat * jnp.mean(gdy * xhat, axis=-1, keepdims=True))
        if has_resid:
            dx = dx + r_ref[...]
        dx_ref[...] = dx.astype(dx_ref.dtype)

        @pl.when(pl.program_id(0) == 0)
        def _():
            dg_ref[...] = jnp.zeros_like(dg_ref)

        dg_ref[...] += _fold8(dyv * xhat)

    row = pl.BlockSpec((tm, d), lambda i: (i, 0))
    vec = pl.BlockSpec((1, d), lambda i: (0, 0))
    args = [x, g.reshape(1, d), dy] + ([resid] if has_resid else [])
    dx, dg = pl.pallas_call(
        body, name="norm_bwd_resid" if has_resid else "norm_bwd",
        out_shape=(jax.ShapeDtypeStruct((rows, d), out_dtype), jax.ShapeDtypeStruct((8, d), F32)),
        grid=(rows // tm,),
        in_specs=[row, vec, row] + ([row] if has_resid else []),
        out_specs=(row, pl.BlockSpec((8, d), lambda i: (0, 0))),
        compiler_params=_params(("arbitrary",)),
    )(*args)
    return dx, dg.sum(axis=0)


def _swiglu_fwd(ab):
    rows, two_f = ab.shape
    f = two_f // 2
    tm = _row_tile(rows, 640)

    def body(a_ref, b_ref, o_ref):
        a = a_ref[...]
        o_ref[...] = (a * jax.nn.sigmoid(a) * b_ref[...]).astype(BF16)

    return pl.pallas_call(
        body, name="swiglu_fwd",
        out_shape=jax.ShapeDtypeStruct((rows, f), BF16),
        grid=(rows // tm,),
        in_specs=[pl.BlockSpec((tm, f), lambda i: (i, 0)), pl.BlockSpec((tm, f), lambda i: (i, 1))],
        out_specs=pl.BlockSpec((tm, f), lambda i: (i, 0)),
        compiler_params=_params(("parallel",)),
    )(ab, ab)


def _swiglu_bwd(ab, dff):
    rows, two_f = ab.shape
    f = two_f // 2
    tm = _row_tile(rows, 256)

    def body(a_ref, b_ref, d_ref, dab_ref):
        a = a_ref[...]
        d = d_ref[...]
        s = jax.nn.sigmoid(a)
        dab_ref[:, :f] = (d * b_ref[...] * (s * (1.0 + a * (1.0 - s)))).astype(BF16)
        dab_ref[:, f:] = (d * (a * s)).astype(BF16)

    lo = pl.BlockSpec((tm, f), lambda i: (i, 0))
    hi = pl.BlockSpec((tm, f), lambda i: (i, 1))
    return pl.pallas_call(
        body, name="swiglu_bwd",
        out_shape=jax.ShapeDtypeStruct((rows, two_f), BF16),
        grid=(rows // tm,),
        in_specs=[lo, hi, lo],
        out_specs=pl.BlockSpec((tm, two_f), lambda i: (i, 0)),
        compiler_params=_params(("parallel",)),
    )(ab, ab, dff)


def _tri(lower):
    r = lax.broadcasted_iota(jnp.int32, (CHUNK, CHUNK), 0)
    c = lax.broadcasted_iota(jnp.int32, (CHUNK, CHUNK), 1)
    return jnp.where((r >= c) if lower else (r <= c), 1.0, 0.0).astype(BF16)


def _logf_group(rows):
    n = rows // CHUNK
    return max(g for g in range(1, 9) if n % g == 0)


def _logf_fwd(z, b):
    h, rows = z.shape
    g = _logf_group(rows)
    n = rows // (g * CHUNK)

    def body(z_ref, b_ref, f_ref, carry):
        i = pl.program_id(0)

        @pl.when(i == 0)
        def _():
            carry[...] = jnp.zeros_like(carry)

        c = carry[...]
        for t in range(g):
            sl = slice(t * CHUNK, (t + 1) * CHUNK)
            x = z_ref[:, sl] + b_ref[...]
            lf = jnp.minimum(x, 0.0) - jnp.log(1.0 + jnp.exp(-jnp.abs(x)))
            col = (i * g + t) * CHUNK + lax.broadcasted_iota(jnp.int32, (1, CHUNK), 1)
            lf = jnp.where(col >= PAD_ROWS, lf, 0.0)
            run = _apply01(_tri(False), lf, 3, left=False) + c
            f_ref[:, sl] = run
            c = jnp.broadcast_to(run[:, CHUNK - 1:CHUNK], c.shape)
        carry[...] = c

    blk = pl.BlockSpec((h, g * CHUNK), lambda i: (0, i))
    return pl.pallas_call(
        body, name="logf_fwd",
        out_shape=jax.ShapeDtypeStruct((h, rows), F32),
        grid=(n,),
        in_specs=[blk, pl.BlockSpec((h, CHUNK), lambda i: (0, 0))],
        out_specs=blk,
        scratch_shapes=[pltpu.VMEM((h, CHUNK), F32)],
        compiler_params=_params(("arbitrary",)),
    )(z, b)


def _logf_bwd(df, z, b):
    h, rows = z.shape
    g = _logf_group(rows)
    n = rows // (g * CHUNK)

    def body(df_ref, z_ref, b_ref, dz_ref, db_ref, carry):
        i = pl.program_id(0)

        @pl.when(i == 0)
        def _():
            carry[...] = jnp.zeros_like(carry)
            db_ref[...] = jnp.zeros_like(db_ref)

        c = carry[...]
        db = db_ref[...]
        for t in reversed(range(g)):
            sl = slice(t * CHUNK, (t + 1) * CHUNK)
            run = _apply01(_tri(True), df_ref[:, sl], 3, left=False) + c
            c = jnp.broadcast_to(run[:, 0:1], c.shape)
            x = z_ref[:, sl] + b_ref[...]
            col = ((n - 1 - i) * g + t) * CHUNK + lax.broadcasted_iota(jnp.int32, (1, CHUNK), 1)
            dz = jnp.where(col >= PAD_ROWS, run * (1.0 - jax.nn.sigmoid(x)), 0.0)
            dz_ref[:, sl] = dz
            db = db + dz
        carry[...] = c
        db_ref[...] = db

    rev = pl.BlockSpec((h, g * CHUNK), lambda i: (0, n - 1 - i))
    fix = pl.BlockSpec((h, CHUNK), lambda i: (0, 0))
    dz, db = pl.pallas_call(
        body, name="logf_bwd",
        out_shape=(jax.ShapeDtypeStruct((h, rows), F32), jax.ShapeDtypeStruct((h, CHUNK), F32)),
        grid=(n,),
        in_specs=[rev, rev, fix],
        out_specs=(rev, fix),
        scratch_shapes=[pltpu.VMEM((h, CHUNK), F32)],
        compiler_params=_params(("arbitrary",)),
    )(df, z, b)
    return dz, db.sum(axis=1)


def _lane_lo():
    return lax.broadcasted_iota(jnp.int32, (1, CHUNK), 1) < HEAD_DIM


def _attn_block(rows):
    return _row_tile(rows, min(640, rows // 2))


def _masked_logits(s, i, j, blk):
    row = i * blk + lax.broadcasted_iota(jnp.int32, (blk, 1), 0)
    col = j * blk + lax.broadcasted_iota(jnp.int32, (1, blk), 1)
    return jnp.where(col <= row, jnp.where(col >= PAD_ROWS, s, NEG_INF), NEG_INF)


def _attn_prep(qkv, f_t, d):
    rows = qkv.shape[0]
    heads = d // HEAD_DIM
    hp = heads // 2
    tm = _row_tile(rows, 640)
    scale = 1.0 / math.sqrt(HEAD_DIM)

    def body(q_ref, k_ref, v_ref, f_ref, qa_ref, ka_ref, vo_ref, va_ref):
        pr = pl.program_id(1)
        lane = lax.broadcasted_iota(jnp.int32, (1, CHUNK), 1)
        lo = lane < HEAD_DIM
        q2 = (q_ref[...].astype(F32) * scale).astype(BF16)
        k2 = k_ref[...]
        v2 = v_ref[...]
        zero = jnp.zeros_like(k2)
        head_id = lax.broadcasted_iota(jnp.int32, (1, heads), 1)
        ft = f_ref[...]
        for hh in range(2):
            base = HEAD_DIM if hh == 0 else 0
            neg_f = -jnp.sum(jnp.where(head_id == 2 * pr + hh, ft, 0.0), axis=1, keepdims=True)
            aug = zero
            ones = zero
            for t, part in enumerate(_split_bf16(neg_f, 3)):
                aug = jnp.where(lane == base + t, part, aug)
                ones = jnp.where(lane == base + t, jnp.ones_like(zero), ones)
            own = lo if hh == 0 else jnp.logical_not(lo)
            sl = slice(hh * CHUNK, (hh + 1) * CHUNK)
            qa_ref[:, sl] = jnp.where(own, q2, ones)
            ka_ref[:, sl] = jnp.where(own, k2, aug)
            vo_ref[:, sl] = jnp.where(own, v2, zero)
            va_ref[:, sl] = jnp.where(own, v2, jnp.where(lane == base, jnp.ones_like(zero), zero))

    pair_in = lambda c: pl.BlockSpec((tm, CHUNK), lambda i, p: (i, c * hp + p))
    pair_out = pl.BlockSpec((tm, 2 * CHUNK), lambda i, p: (i, p))
    sds = jax.ShapeDtypeStruct((rows, 2 * d), BF16)
    return pl.pallas_call(
        body, name="attn_prep",
        out_shape=(sds, sds, sds, sds),
        grid=(rows // tm, hp),
        in_specs=[pair_in(0), pair_in(1), pair_in(2), pl.BlockSpec((tm, heads), lambda i, p: (i, 0))],
        out_specs=(pair_out, pair_out, pair_out, pair_out),
        compiler_params=_params(("parallel", "parallel")),
    )(qkv, qkv, qkv, f_t)


FWD_PAIRS = 2


def _attn_fwd(q_aug, k_aug, v_aug, d):
    rows = q_aug.shape[0]
    hp = d // CHUNK
    blk = _attn_block(rows)
    nb = rows // blk
    nt = (((1,), (1,)), ((), ()))
    den_lane = (HEAD_DIM, 0)
    gp = FWD_PAIRS if hp % FWD_PAIRS == 0 else 1
    nh = 2 * gp

    def body(qi_ref, kj_ref, q_ref, k_ref, v_ref, o_ref, lse_ref, m_s, acc_s):
        i = qi_ref[pl.program_id(1)]
        j = kj_ref[pl.program_id(1)]

        @pl.when(j == 0)
        def _():
            m_s[...] = jnp.full(m_s.shape, NEG_INF, F32)
            acc_s[...] = jnp.zeros_like(acc_s)

        def step(masked):
            tiles = [slice(hh * CHUNK, (hh + 1) * CHUNK) for hh in range(nh)]
            scores = [lax.dot_general(q_ref[:, ln], k_ref[:, ln], nt, preferred_element_type=F32) for ln in tiles]
            for hh in range(nh):
                s = scores[hh]
                if masked:
                    s = _masked_logits(s, i, j, blk)
                m_prev = m_s[hh]
                m_new = jnp.maximum(m_prev, s.max(axis=-1, keepdims=True))
                p = jnp.exp(s - m_new)
                m_s[hh] = m_new
                pv = jnp.dot(p.astype(BF16), v_ref[:, tiles[hh]], preferred_element_type=F32)
                acc_s[hh] = acc_s[hh] * jnp.exp(m_prev - m_new) + pv

        edge = (j == i) | (j == 0)

        @pl.when(edge)
        def _():
            step(True)

        @pl.when(jnp.logical_not(edge))
        def _():
            step(False)

        @pl.when(j == i)
        def _():
            row = i * blk + lax.broadcasted_iota(jnp.int32, (blk, 1), 0)
            lo = _lane_lo()
            for pr in range(gp):
                acc = [acc_s[2 * pr + hh] for hh in range(2)]
                den = [acc[hh][:, den_lane[hh]:den_lane[hh] + 1] for hh in range(2)]
                o = jnp.where(lo, acc[0] * (1.0 / den[0]), acc[1] * (1.0 / den[1]))
                sl = slice(pr * CHUNK, (pr + 1) * CHUNK)
                o_ref[:, sl] = jnp.where(row >= PAD_ROWS, o, 0.0)
                lse_ref[:, sl] = jnp.where(lo, m_s[2 * pr] + jnp.log(den[0]), m_s[2 * pr + 1] + jnp.log(den[1]))

    pairs = [(i, j) for i in range(nb) for j in range(i + 1)]
    q_spec = pl.BlockSpec((blk, nh * CHUNK), lambda h, t, qi, kj: (qi[t], h))
    k_spec = pl.BlockSpec((blk, nh * CHUNK), lambda h, t, qi, kj: (kj[t], h))
    o_spec = pl.BlockSpec((blk, gp * CHUNK), lambda h, t, qi, kj: (qi[t], h))
    return pl.pallas_call(
        body, name="attn_fwd",
        out_shape=(jax.ShapeDtypeStruct((rows, d), F32), jax.ShapeDtypeStruct((rows, d), F32)),
        grid_spec=pltpu.PrefetchScalarGridSpec(
            num_scalar_prefetch=2, grid=(hp // gp, len(pairs)),
            in_specs=[q_spec, k_spec, k_spec], out_specs=(o_spec, o_spec),
            scratch_shapes=[pltpu.VMEM((nh, blk, 1), F32), pltpu.VMEM((nh, blk, CHUNK), F32)]),
        compiler_params=_params(("parallel", "arbitrary")),
    )(jnp.array([p[0] for p in pairs], jnp.int32), jnp.array([p[1] for p in pairs], jnp.int32),
      q_aug, k_aug, v_aug)


def _attn_bwd(q_aug, k_aug, v_own, do, lse, delta, d):
    rows = q_aug.shape[0]
    hp = d // CHUNK
    blk = _attn_block(rows)
    nb = rows // blk
    scale = 1.0 / math.sqrt(HEAD_DIM)
    nt = (((1,), (1,)), ((), ()))
    tn = (((0,), (0,)), ((), ()))

    def body(kj_ref, qi_ref, q_ref, k_ref, v_ref, do_ref, lse_ref, dl_ref, dq_ref, dk_ref, dv_ref, df_ref, rs_ref,
             dq_s, dk_s, dv_s, df_s, rs_s):
        j = kj_ref[pl.program_id(1)]
        i = qi_ref[pl.program_id(1)]
        lo = _lane_lo()

        @pl.when((j == 0) & (i == 0))
        def _():
            dq_s[...] = jnp.zeros_like(dq_s)
            rs_s[...] = jnp.zeros_like(rs_s)

        @pl.when(i == j)
        def _():
            dk_s[...] = jnp.zeros_like(dk_s)
            dv_s[...] = jnp.zeros_like(dv_s)
            df_s[...] = jnp.zeros_like(df_s)

        def step(masked):
            dov = do_ref[...]
            t_dq, t_dk, t_dv, row_sums = [], [], [], []
            for hh in range(2):
                lanes = slice(hh * CHUNK, (hh + 1) * CHUNK)
                qh, kh = q_ref[:, lanes], k_ref[:, lanes]
                off = hh * HEAD_DIM
                s = lax.dot_general(qh, kh, nt, preferred_element_type=F32)
                if masked:
                    s = _masked_logits(s, i, j, blk)
                p = jnp.exp(s - lse_ref[:, off:off + 1])
                dp = lax.dot_general(dov, v_ref[:, lanes], nt, preferred_element_type=F32)
                ds = p * (dp - dl_ref[:, off:off + 1])
                df_s[hh:hh + 1, :] += ds.sum(axis=0, keepdims=True)
                row_sums.append(ds.sum(axis=1, keepdims=True))
                pb = p.astype(BF16)
                dsb = ds.astype(BF16)
                t_dv.append(lax.dot_general(pb, dov, tn, preferred_element_type=F32))
                t_dk.append(lax.dot_general(dsb, qh, tn, preferred_element_type=F32))
                t_dq.append(jnp.dot(dsb, kh, preferred_element_type=F32))
            dv_s[...] += jnp.where(lo, t_dv[0], t_dv[1])
            dk_s[...] += jnp.where(lo, t_dk[0], t_dk[1])
            r0 = pl.multiple_of(i * blk, blk)
            dq_s[pl.ds(r0, blk), :] += jnp.where(lo, t_dq[0], t_dq[1])
            rs_s[pl.ds(r0, blk), :] += jnp.where(lo, row_sums[0], row_sums[1])

        edge = (j == i) | (j == 0)

        @pl.when(edge)
        def _():
            step(True)

        @pl.when(jnp.logical_not(edge))
        def _():
            step(False)

        @pl.when(i == nb - 1)
        def _():
            dk_ref[...] = dk_s[...].astype(BF16)
            dv_ref[...] = dv_s[...].astype(BF16)
            df_ref[...] = -df_s[...]

        @pl.when((i == nb - 1) & (j == nb - 1))
        def _():
            dq_ref[...] = (dq_s[...] * scale).astype(BF16)
            rs_ref[...] = rs_s[...]

    pairs = [(j, i) for j in range(nb) for i in range(j, nb)]
    q_spec = pl.BlockSpec((blk, 2 * CHUNK), lambda h, t, kj, qi: (qi[t], h))
    kv_spec = pl.BlockSpec((blk, 2 * CHUNK), lambda h, t, kj, qi: (kj[t], h))
    row_spec = pl.BlockSpec((blk, CHUNK), lambda h, t, kj, qi: (qi[t], h))
    kv_out = pl.BlockSpec((blk, CHUNK), lambda h, t, kj, qi: (kj[t], h))
    dq_out = pl.BlockSpec((rows, CHUNK), lambda h, t, kj, qi: (0, h))
    return pl.pallas_call(
        body, name="attn_bwd",
        out_shape=(jax.ShapeDtypeStruct((rows, d), BF16), jax.ShapeDtypeStruct((rows, d), BF16),
                   jax.ShapeDtypeStruct((rows, d), BF16), jax.ShapeDtypeStruct((hp, 2, rows), F32),
                   jax.ShapeDtypeStruct((rows, d), F32)),
        grid_spec=pltpu.PrefetchScalarGridSpec(
            num_scalar_prefetch=2, grid=(hp, len(pairs)),
            in_specs=[q_spec, kv_spec, kv_spec, row_spec, row_spec, row_spec],
            out_specs=(dq_out, kv_out, kv_out,
                       pl.BlockSpec((None, 2, blk), lambda h, t, kj, qi: (h, 0, kj[t])), dq_out),
            scratch_shapes=[pltpu.VMEM((rows, CHUNK), F32), pltpu.VMEM((blk, CHUNK), F32),
                            pltpu.VMEM((blk, CHUNK), F32), pltpu.VMEM((2, blk), F32),
                            pltpu.VMEM((rows, CHUNK), F32)]),
        compiler_params=_params(("parallel", "arbitrary")),
    )(jnp.array([p[0] for p in pairs], jnp.int32), jnp.array([p[1] for p in pairs], jnp.int32),
      q_aug, k_aug, v_own, do, lse, delta)


def _band(w, transposed, other):
    r = lax.broadcasted_iota(jnp.int32, (CHUNK, CHUNK), 0)
    c = lax.broadcasted_iota(jnp.int32, (CHUNK, CHUNK), 1)
    dist = (c - r) if transposed else (r - c)
    if other:
        dist = dist + CHUNK
    return jnp.where(dist >= 0, jnp.where(dist < w, 1.0, 0.0), 0.0).astype(BF16)


def _inv_count(chunk_index, w):
    row = chunk_index * CHUNK + lax.broadcasted_iota(jnp.int32, (CHUNK, 1), 0)
    cnt = jnp.clip(row - PAD_ROWS + 1, 1, w).astype(F32)
    return 1.0 / cnt


def _pool_diff(u_cur, u_prev, i, w):
    ws = _apply01(_band(w, False, False), u_cur, 3)
    ws = ws + jnp.where(i > 0, _apply01(_band(w, False, True), u_prev, 3), 0.0)
    return ws * _inv_count(i, w) - u_cur


def _pool_merge_fwd(rest, o, w_pool, scale, d):
    rows = rest.shape[0]
    n = rows // CHUNK
    cg = d // len(POOL_WINDOWS)

    def body(up_ref, uc_ref, gp_ref, ga_ref, o_ref, wp_ref, sc_ref, mg_ref, yp_ref):
        i = pl.program_id(0)
        for g, w in enumerate(POOL_WINDOWS):
            sl = slice(g * cg, (g + 1) * cg)
            diff = _pool_diff(uc_ref[:, sl], up_ref[:, sl], i, w)
            ypre = jnp.dot(diff.astype(BF16), wp_ref[g], preferred_element_type=F32)
            yp_ref[:, sl] = ypre
            merged = (jax.nn.sigmoid(gp_ref[:, sl]) * (ypre * sc_ref[:, sl])
                      + jax.nn.sigmoid(ga_ref[:, sl]) * o_ref[:, sl])
            mg_ref[:, sl] = merged.astype(BF16)

    col = lambda c: pl.BlockSpec((CHUNK, d), lambda i: (i, c))
    return pl.pallas_call(
        body, name="pool_merge_fwd",
        out_shape=(jax.ShapeDtypeStruct((rows, d), BF16), jax.ShapeDtypeStruct((rows, d), F32)),
        grid=(n,),
        in_specs=[pl.BlockSpec((CHUNK, d), lambda i: (jnp.maximum(i - 1, 0), 0)), col(0), col(1), col(2), col(0),
                  pl.BlockSpec((len(POOL_WINDOWS), cg, cg), lambda i: (0, 0, 0)),
                  pl.BlockSpec((1, d), lambda i: (0, 0))],
        out_specs=(col(0), col(0)),
        compiler_params=_params(("parallel",)),
    )(rest, rest, rest, rest, o, w_pool, scale.reshape(1, d))


def _gate_bwd(dm, rest, o, ypre, scale, d):
    rows = rest.shape[0]
    n = rows // CHUNK

    def body(dm_ref, gp_ref, ga_ref, o_ref, yp_ref, sc_ref, dgp_ref, dga_ref, do_ref, dl_ref, dy_ref, ds_ref):
        @pl.when(pl.program_id(0) == 0)
        def _():
            ds_ref[...] = jnp.zeros_like(ds_ref)

        dmv = dm_ref[...]
        sp = jax.nn.sigmoid(gp_ref[...])
        sa = jax.nn.sigmoid(ga_ref[...])
        ov = o_ref[...]
        ypre_v = yp_ref[...]
        sc = sc_ref[...]
        dgp_ref[...] = (dmv * (ypre_v * sc) * (sp * (1.0 - sp))).astype(BF16)
        dga_ref[...] = (dmv * ov * (sa * (1.0 - sa))).astype(BF16)
        t = dmv * sp
        dy_ref[...] = (t * sc).astype(BF16)
        ds_ref[...] += _fold8(t * ypre_v)
        dob = (dmv * sa).astype(BF16)
        do_ref[...] = dob
        prod = dob.astype(F32) * ov
        lo = _lane_lo()
        for pr in range(d // CHUNK):
            sl = slice(pr * CHUNK, (pr + 1) * CHUNK)
            tp = prod[:, sl]
            s_lo = jnp.where(lo, tp, 0.0).sum(axis=-1, keepdims=True)
            s_hi = jnp.where(lo, 0.0, tp).sum(axis=-1, keepdims=True)
            dl_ref[:, sl] = jnp.where(lo, s_lo, s_hi)

    col = lambda c: pl.BlockSpec((CHUNK, d), lambda i: (i, c))
    row_bf = jax.ShapeDtypeStruct((rows, d), BF16)
    outs = pl.pallas_call(
        body, name="gate_bwd",
        out_shape=(row_bf, row_bf, row_bf, jax.ShapeDtypeStruct((rows, d), F32), row_bf,
                   jax.ShapeDtypeStruct((8, d), F32)),
        grid=(n,),
        in_specs=[col(0), col(1), col(2), col(0), col(0), pl.BlockSpec((1, d), lambda i: (0, 0))],
        out_specs=(col(0), col(0), col(0), col(0), col(0), pl.BlockSpec((8, d), lambda i: (0, 0))),
        compiler_params=_params(("arbitrary",)),
    )(dm, rest, rest, o, ypre, scale.reshape(1, d))
    return outs[:5] + (outs[5].sum(axis=0),)


def _pool_bwd(dypre, rest, w_pool, d):
    rows = rest.shape[0]
    n = rows // CHUNK
    ng = len(POOL_WINDOWS)
    cg = d // ng
    nt = (((1,), (1,)), ((), ()))
    tn = (((0,), (0,)), ((), ()))

    def body(dc_ref, dn_ref, up_ref, uc_ref, wp_ref, du_ref, dw_ref):
        i = pl.program_id(0)

        @pl.when(i == 0)
        def _():
            dw_ref[...] = jnp.zeros_like(dw_ref)

        row = i * CHUNK + lax.broadcasted_iota(jnp.int32, (CHUNK, 1), 0)
        for g, w in enumerate(POOL_WINDOWS):
            sl = slice(g * cg, (g + 1) * cg)
            diff = _pool_diff(uc_ref[:, sl], up_ref[:, sl], i, w)
            dyc = dc_ref[:, sl]
            dw_ref[g] += lax.dot_general(diff.astype(BF16), dyc, tn, preferred_element_type=F32)
            wg = wp_ref[g]
            dd_cur = lax.dot_general(dyc, wg, nt, preferred_element_type=F32)
            dd_next = lax.dot_general(dn_ref[:, sl], wg, nt, preferred_element_type=F32)
            du = _apply01(_band(w, True, False), dd_cur * _inv_count(i, w), 2)
            du = du + jnp.where(i < n - 1, _apply01(_band(w, True, True), dd_next * _inv_count(i + 1, w), 2), 0.0)
            du = du - dd_cur
            du_ref[:, sl] = jnp.where(row >= PAD_ROWS, du, 0.0).astype(BF16)

    cur = pl.BlockSpec((CHUNK, d), lambda i: (i, 0))
    return pl.pallas_call(
        body, name="pool_bwd",
        out_shape=(jax.ShapeDtypeStruct((rows, d), BF16), jax.ShapeDtypeStruct((ng, cg, cg), F32)),
        grid=(n,),
        in_specs=[cur, pl.BlockSpec((CHUNK, d), lambda i: (jnp.minimum(i + 1, n - 1), 0)),
                  pl.BlockSpec((CHUNK, d), lambda i: (jnp.maximum(i - 1, 0), 0)), cur,
                  pl.BlockSpec((ng, cg, cg), lambda i: (0, 0, 0))],
        out_specs=(cur, pl.BlockSpec((ng, cg, cg), lambda i: (0, 0, 0))),
        compiler_params=_params(("arbitrary",)),
    )(dypre, dypre, rest, rest, w_pool)


def _loss_grad(h_res, target):
    rows, d = h_res.shape
    n = rows // CHUNK

    def body(h_ref, t_ref, dh_ref, acc_ref):
        i = pl.program_id(0)

        @pl.when(i == 0)
        def _():
            acc_ref[...] = jnp.zeros_like(acc_ref)
            dh_ref[...] = jnp.zeros_like(dh_ref)

        @pl.when(i > 0)
        def _():
            err = h_ref[...] - t_ref[...]
            dh_ref[...] = err * (1.0 / d)
            e2 = _fold8(err * err)
            part = e2[:, 0:CHUNK]
            for c in range(1, d // CHUNK):
                part = part + e2[:, c * CHUNK:(c + 1) * CHUNK]
            acc_ref[...] += part

    dh, acc = pl.pallas_call(
        body, name="loss_grad",
        out_shape=(jax.ShapeDtypeStruct((rows, d), F32), jax.ShapeDtypeStruct((8, CHUNK), F32)),
        grid=(n,),
        in_specs=[pl.BlockSpec((CHUNK, d), lambda i: (i, 0)),
                  pl.BlockSpec((CHUNK, d), lambda i: (jnp.maximum(i - 1, 0), 0))],
        out_specs=(pl.BlockSpec((CHUNK, d), lambda i: (i, 0)), pl.BlockSpec((8, CHUNK), lambda i: (0, 0))),
        compiler_params=_params(("arbitrary",)),
    )(h_res, target)
    return dh, (0.5 / d) * acc.sum()


def _adamw(slots, w, m, v, name):
    rows, cols = w.shape
    lanes = -(-cols // CHUNK) * CHUNK
    row_bytes = lanes * (N_DEV * slots.dtype.itemsize + 7 * 4)
    tr = max(t for t in _divisors(rows, 8) if t <= max(8, ADAM_TILE_BYTES // row_bytes))
    c1 = 1.0 - ADAM_B1 ** ADAM_STEP
    c2 = 1.0 - ADAM_B2 ** ADAM_STEP

    def body(s_ref, w_ref, m_ref, v_ref, g_ref, d_ref, mo_ref, vo_ref):
        g = s_ref[0].astype(F32)
        for k in range(1, N_DEV):
            g = g + s_ref[k].astype(F32)
        m_new = ADAM_B1 * m_ref[...] + (1.0 - ADAM_B1) * g
        v_new = ADAM_B2 * v_ref[...] + (1.0 - ADAM_B2) * (g * g)
        m_hat = m_new / c1
        v_hat = v_new / c2
        g_ref[...] = g
        d_ref[...] = -ADAM_LR * (m_hat / (jnp.sqrt(v_hat) + ADAM_EPS) + ADAM_WD * w_ref[...])
        mo_ref[...] = m_new
        vo_ref[...] = v_new

    tile = pl.BlockSpec((tr, cols), lambda i: (i, 0))
    sds = jax.ShapeDtypeStruct((rows, cols), F32)
    return pl.pallas_call(
        body, name=name,
        out_shape=(sds, sds, sds, sds),
        grid=(rows // tr,),
        in_specs=[pl.BlockSpec((N_DEV, tr, cols), lambda i: (0, i, 0)), tile, tile, tile],
        out_specs=(tile, tile, tile, tile),
        compiler_params=_params(("parallel",)),
    )(slots, w, m, v)


def _exchange(srcs, gather, name):
    n = len(srcs)
    shapes = [((N_DEV,) + s.shape) if gather else s.shape for s in srcs]

    def body(*refs):
        src_refs, out_refs = refs[:n], refs[n:2 * n]
        send_sems, recv_sems, local_sems = refs[2 * n:]
        x, y, c = lax.axis_index("x"), lax.axis_index("y"), lax.axis_index("c")
        me = 4 * x + 2 * y + c

        def payload(a, slot):
            return src_refs[a] if gather else src_refs[a].at[slot]

        own = [pltpu.make_async_copy(payload(a, me), out_refs[a].at[me], local_sems.at[a]) for a in range(n)]
        for cp in own:
            cp.start()
        sends, recvs = [], []
        for k in range(1, N_DEV):
            px = 1 - x if k & 4 else x
            py = 1 - y if k & 2 else y
            pc = 1 - c if k & 1 else c
            peer = 4 * px + 2 * py + pc
            for a in range(n):
                sems = dict(send_sem=send_sems.at[(k - 1) * n + a], recv_sem=recv_sems.at[(k - 1) * n + a],
                            device_id=(px, py, pc), device_id_type=MESH_ID)
                sends.append(pltpu.make_async_remote_copy(src_ref=payload(a, peer), dst_ref=out_refs[a].at[me], **sems))
                recvs.append(pltpu.make_async_remote_copy(src_ref=payload(a, peer), dst_ref=out_refs[a].at[peer], **sems))
        for cp in sends:
            cp.start()
        for cp in recvs:
            cp.wait_recv()
        for cp in sends:
            cp.wait_send()
        for cp in own:
            cp.wait()

    return pl.pallas_call(
        body, name=name,
        out_shape=tuple(jax.ShapeDtypeStruct(sh, s.dtype) for sh, s in zip(shapes, srcs)),
        in_specs=[pl.BlockSpec(memory_space=pl.ANY)] * n,
        out_specs=tuple([pl.BlockSpec(memory_space=pl.ANY)] * n),
        scratch_shapes=[pltpu.SemaphoreType.DMA(((N_DEV - 1) * n,)), pltpu.SemaphoreType.DMA(((N_DEV - 1) * n,)),
                        pltpu.SemaphoreType.DMA((n,))],
    )(*srcs)


HBM_SPEC = pl.BlockSpec(memory_space=pltpu.HBM)
SEM_SPEC = pl.BlockSpec(memory_space=pltpu.SEMAPHORE)
DATAFLOW = pltpu.SideEffectType.DATAFLOW_SIDE_EFFECTING


def _peer_copies(src_refs, land_refs, send_sems, recv_sems, gather):
    n = len(src_refs)
    x, y, c = lax.axis_index("x"), lax.axis_index("y"), lax.axis_index("c")
    me = 4 * x + 2 * y + c
    sends, lands = [], []
    for k in range(1, N_DEV):
        px = 1 - x if k & 4 else x
        py = 1 - y if k & 2 else y
        pc = 1 - c if k & 1 else c
        peer = 4 * px + 2 * py + pc
        for a in range(n):
            src = src_refs[a] if gather else src_refs[a].at[peer]
            sems = dict(send_sem=send_sems.at[(k - 1) * n + a], recv_sem=recv_sems.at[(k - 1) * n + a],
                        device_id=(px, py, pc), device_id_type=MESH_ID)
            sends.append(pltpu.make_async_remote_copy(src_ref=src, dst_ref=land_refs[a].at[me], **sems))
            lands.append(pltpu.make_async_remote_copy(src_ref=src, dst_ref=land_refs[a].at[peer], **sems))
    return sends, lands


def _exchange_start(srcs, gather, name):
    n = len(srcs)
    shapes = [((N_DEV,) + s.shape) if gather else s.shape for s in srcs]
    n_sem = (N_DEV - 1) * n

    def body(*refs):
        src_refs, land_refs = refs[:n], refs[n:2 * n]
        send_sems, recv_sems = refs[2 * n], refs[2 * n + 1]
        token = refs[-1]
        sends, _ = _peer_copies(src_refs, land_refs, send_sems, recv_sems, gather)
        for cp in sends:
            cp.start()
        token[...] = jnp.zeros_like(token)

    hbm = lambda arrays_shapes: [pltpu.HBM(sh, dt) for sh, dt in arrays_shapes]
    src_types = [(s.shape, s.dtype) for s in srcs]
    land_types = [(sh, s.dtype) for sh, s in zip(shapes, srcs)]
    outs = pl.pallas_call(
        body, name=name,
        out_shape=(pltpu.SemaphoreType.DMA((n_sem,)), pltpu.SemaphoreType.DMA((n_sem,)),
                   *hbm(src_types), *hbm(land_types), jax.ShapeDtypeStruct((8, CHUNK), F32)),
        in_specs=[HBM_SPEC] * (2 * n),
        out_specs=(SEM_SPEC, SEM_SPEC, *([HBM_SPEC] * (2 * n)), pl.BlockSpec(memory_space=pltpu.VMEM)),
        input_output_aliases={a: 2 + a for a in range(2 * n)},
        compiler_params=pltpu.CompilerParams(has_side_effects=DATAFLOW),
    )(*[pltpu.with_memory_space_constraint(s, pltpu.HBM) for s in srcs],
      *[pltpu.with_memory_space_constraint(lax.empty(sh, dt), pltpu.HBM) for sh, dt in land_types])
    return (outs[0], outs[1], outs[2:2 + n], outs[2 + n:2 + 2 * n]), outs[-1]


def _exchange_wait(handle, after, gather, name):
    send_sems, recv_sems, src_thru, land_thru = handle
    n = len(src_thru)

    def body(*refs):
        src_refs, land_refs = refs[:n], refs[n:2 * n]
        _, lands = _peer_copies(src_refs, land_refs, refs[2 * n], refs[2 * n + 1], gather)
        for cp in lands:
            cp.wait_send()
            cp.wait_recv()

    outs = pl.pallas_call(
        body, name=name,
        out_shape=tuple(pltpu.HBM(t.shape, t.dtype) for t in (*src_thru, *land_thru)),
        in_specs=[*([HBM_SPEC] * (2 * n)), SEM_SPEC, SEM_SPEC, pl.BlockSpec(memory_space=pl.ANY)],
        out_specs=tuple([HBM_SPEC] * (2 * n)),
        input_output_aliases={a: a for a in range(2 * n)},
        compiler_params=pltpu.CompilerParams(has_side_effects=DATAFLOW),
    )(*src_thru, *land_thru, send_sems, recv_sems, after)
    return outs[n:]


def _fill_own(land, own, me):
    return lax.dynamic_update_slice_in_dim(land, own[None].astype(land.dtype), me, axis=0)


BIG = ("w_in", "w_pool", "w_out", "w_gate", "w_up", "w_down")
SHARD_AXIS = dict(w_in=-1, w_pool=-2, w_out=-2, w_gate=-1, w_up=-1, w_down=-2)
SMALL = ("norm_mix_pre", "norm_mix_post", "norm_ffn_pre", "norm_ffn_post", "pool_scale")


def _join(g, axis):
    return jnp.concatenate([g[j] for j in range(N_DEV)], axis=axis)


def _split(full, axis):
    return jnp.stack(jnp.split(full, N_DEV, axis=axis))


def _rows2d(a):
    return a.reshape(-1, a.shape[-1])


def kernel(x, meta_tokens, norm_mix_pre, norm_mix_post, norm_ffn_pre, norm_ffn_post, w_in, b_forget, w_pool, pool_scale, w_out, w_ffn_gate, w_ffn_up, w_ffn_down, loss_target, m_meta_tokens, m_norm_mix_pre, m_norm_mix_post, m_norm_ffn_pre, m_norm_ffn_post, m_w_in, m_b_forget, m_w_pool, m_pool_scale, m_w_out, m_w_ffn_gate, m_w_ffn_up, m_w_ffn_down, v_meta_tokens, v_norm_mix_pre, v_norm_mix_post, v_norm_ffn_pre, v_norm_ffn_post, v_w_in, v_b_forget, v_w_pool, v_pool_scale, v_w_out, v_w_ffn_gate, v_w_ffn_up, v_w_ffn_down):
    x2 = x[0]
    target = loss_target[0]
    seq, d = x2.shape
    depth = w_in.shape[0]
    heads = d // HEAD_DIM
    ff = w_ffn_gate.shape[2] * N_DEV
    rows = PAD_ROWS + META_TOKENS + seq
    assert seq % CHUNK == 0 and d % (2 * CHUNK) == 0 and heads <= FORGET_PAD and depth >= 2
    me = 4 * lax.axis_index("x") + 2 * lax.axis_index("y") + lax.axis_index("c")

    big = dict(w_in=w_in, w_pool=w_pool, w_out=w_out, w_gate=w_ffn_gate, w_up=w_ffn_up, w_down=w_ffn_down)
    big_m = dict(w_in=m_w_in, w_pool=m_w_pool, w_out=m_w_out, w_gate=m_w_ffn_gate, w_up=m_w_ffn_up, w_down=m_w_ffn_down)
    big_v = dict(w_in=v_w_in, w_pool=v_w_pool, w_out=v_w_out, w_gate=v_w_ffn_gate, w_up=v_w_ffn_up, w_down=v_w_ffn_down)
    small = dict(norm_mix_pre=norm_mix_pre, norm_mix_post=norm_mix_post, norm_ffn_pre=norm_ffn_pre,
                 norm_ffn_post=norm_ffn_post, pool_scale=pool_scale)
    small_m = dict(norm_mix_pre=m_norm_mix_pre, norm_mix_post=m_norm_mix_post, norm_ffn_pre=m_norm_ffn_pre,
                   norm_ffn_post=m_norm_ffn_post, pool_scale=m_pool_scale)
    small_v = dict(norm_mix_pre=v_norm_mix_pre, norm_mix_post=v_norm_mix_post, norm_ffn_pre=v_norm_ffn_pre,
                   norm_ffn_post=v_norm_ffn_post, pool_scale=v_pool_scale)

    wire = [big[n].astype(BF16) for n in BIG]
    gathered = _exchange([w[:1] for w in wire] + [meta_tokens], True, "gather_first")
    meta_full = _join(gathered[-1], -1)
    later, started = _exchange_start([w[1:] for w in wire], True, "gather_later_start")

    def layer_weights(lands):
        full = {n: _join(g, SHARD_AXIS[n]) for n, g in zip(BIG, lands)}
        win = full["w_in"]
        fcol = 4 * d
        qkv_w = win[:, :, d:4 * d]
        rest_w = jnp.concatenate([win[:, :, :d], win[:, :, fcol + heads:], win[:, :, fcol:fcol + heads],
                                  jnp.zeros(win.shape[:2] + (FORGET_PAD - heads,), BF16)], axis=2)
        return dict(qkv=qkv_w, rest=rest_w, cat=jnp.concatenate([qkv_w, rest_w], axis=2),
                    gu=jnp.concatenate([full["w_gate"], full["w_up"]], axis=2),
                    o=full["w_out"], dn=full["w_down"], pl=full["w_pool"])

    groups = [layer_weights(gathered[:-1])]
    weight = lambda key, l: groups[min(l, 1)][key][l - min(l, 1)]
    w_qkv, w_rest, w_cat, w_gu, w_o, w_dn, w_pl = (
        [None] * depth for _ in range(7))

    h_res = jnp.concatenate([jnp.zeros((PAD_ROWS, d), F32), meta_full, x2], axis=0)
    h1 = _norm_fwd(h_res, norm_mix_pre[0] + started[0, 0])
    ones = jnp.ones((d,), F32)
    saved = []
    for l in range(depth):
        if l == 1:
            lands = _exchange_wait(later, h_res, True, "gather_later_wait")
            groups.append(layer_weights([_fill_own(g, w[1:], me) for g, w in zip(lands, wire)]))
        for table_, key in ((w_qkv, "qkv"), (w_rest, "rest"), (w_cat, "cat"), (w_gu, "gu"), (w_o, "o"),
                            (w_dn, "dn"), (w_pl, "pl")):
            table_[l] = weight(key, l)
        qkv = _matmul(h1, w_qkv[l], "nn", BF16, "proj_qkv")
        rest = _matmul(h1, w_rest[l], "nn", F32, "proj_rest")
        z = rest[:, 3 * d:3 * d + heads].T
        bias = jnp.broadcast_to(b_forget[l][:, None], (heads, CHUNK))
        q_aug, k_aug, v_own, v_aug = _attn_prep(qkv, _logf_fwd(z, bias).T, d)
        o, lse = _attn_fwd(q_aug, k_aug, v_aug, d)
        merged, ypre = _pool_merge_fwd(rest, o, w_pl[l], pool_scale[l], d)
        mix = _matmul(merged, w_o[l], "nn", F32, "mix_out")
        h_mid, h2 = _resid_norm_fwd(h_res, mix, norm_mix_post[l], norm_ffn_pre[l])
        ab = _matmul(h2, w_gu[l], "nn", F32, "ffn_in")
        act = _swiglu_fwd(ab)
        ffo = _matmul(act, w_dn[l], "nn", F32, "ffn_out")
        g_next = norm_mix_pre[l + 1] if l + 1 < depth else ones
        h_next, h1_next = _resid_norm_fwd(h_mid, ffo, norm_ffn_post[l], g_next)
        saved.append(dict(h_in=h_res, h1=h1, q_aug=q_aug, k_aug=k_aug, v_own=v_own, rest=rest, z=z,
                          bias=bias, o=o, lse=lse,
                          merged=merged, ypre=ypre, mix=mix, h_mid=h_mid, h2=h2, ab=ab, act=act, ffo=ffo))
        h_res, h1 = h_next, h1_next

    dh, loss_local = _loss_grad(h_res, target)
    loss = lax.psum(loss_local, ("x", "y", "c"))

    grads = {n: [None] * depth for n in BIG + SMALL + ("b_forget",)}

    def grad_slots(first, last):
        parts = [_split(jnp.stack(grads[n][first:last]), SHARD_AXIS[n]).astype(BF16) for n in BIG]
        return [p.reshape(N_DEV, -1, p.shape[-1]) for p in parts]

    for l in reversed(range(depth)):
        s = saved[l]
        g_post = norm_ffn_post[l]
        if l == 0:
            early_slots = grad_slots(1, depth)
            early, started = _exchange_start(early_slots, False, "scatter_early_start")
            g_post = g_post + started[0, 0]
        dffo, grads["norm_ffn_post"][l] = _norm_bwd(s["ffo"], g_post, dh, None, BF16)
        grads["w_down"][l] = _matmul(s["act"], dffo, "tn", F32, "grad_w_down")
        dact = _matmul(dffo, w_dn[l], "nt", F32, "ffn_out_dx")
        dab = _swiglu_bwd(s["ab"], dact)
        dgu = _matmul(s["h2"], dab, "tn", F32, "grad_w_gu")
        grads["w_gate"][l], grads["w_up"][l] = dgu[:, :ff], dgu[:, ff:]
        dh2 = _matmul(dab, w_gu[l], "nt", F32, "ffn_in_dx")
        dh_mid, grads["norm_ffn_pre"][l] = _norm_bwd(s["h_mid"], norm_ffn_pre[l], dh2, dh, F32)

        dmix, grads["norm_mix_post"][l] = _norm_bwd(s["mix"], norm_mix_post[l], dh_mid, None, BF16)
        grads["w_out"][l] = _matmul(s["merged"], dmix, "tn", F32, "grad_w_out")
        dm = _matmul(dmix, w_o[l], "nt", F32, "mix_out_dx")
        dgp, dga, do, delta, dypre, grads["pool_scale"][l] = _gate_bwd(dm, s["rest"], s["o"], s["ypre"], pool_scale[l], d)
        du, grads["w_pool"][l] = _pool_bwd(dypre, s["rest"], w_pl[l], d)
        dq, dk, dv, df_key, df_query = _attn_bwd(s["q_aug"], s["k_aug"], s["v_own"], do, s["lse"], delta, d)
        df = df_key.reshape(heads, rows) + df_query.reshape(rows, heads, HEAD_DIM)[:, :, 0].T
        dz, grads["b_forget"][l] = _logf_bwd(df, s["z"], s["bias"])
        dzt = jnp.pad(dz.T.astype(BF16), ((0, 0), (0, FORGET_PAD - heads)))
        dproj = jnp.concatenate([dq, dk, dv, du, dgp, dga, dzt], axis=1)
        dwc = _matmul(s["h1"], dproj, "tn", F32, "grad_w_in")
        grads["w_in"][l] = jnp.concatenate([dwc[:, 3 * d:4 * d], dwc[:, :3 * d], dwc[:, 6 * d:6 * d + heads],
                                            dwc[:, 4 * d:6 * d]], axis=1)
        dh1 = _matmul(dproj, w_cat[l], "nt", F32, "proj_dx")
        dh, grads["norm_mix_pre"][l] = _norm_bwd(s["h_in"], norm_mix_pre[l], dh1, dh_mid, F32)

    grad_x = dh[PAD_ROWS + META_TOKENS:][None]
    dmeta = dh[PAD_ROWS:PAD_ROWS + META_TOKENS]

    late_recv = _exchange(grad_slots(0, 1), False, "scatter_first")
    early_recv = _exchange_wait(early, dh, False, "scatter_early_wait")
    early_recv = [_fill_own(r, lax.dynamic_index_in_dim(s_, me, 0, keepdims=False), me)
                  for r, s_ in zip(early_recv, early_slots)]
    recv = [jnp.concatenate([a, b], axis=1) for a, b in zip(late_recv, early_recv)]
    big_out = {}
    for n, r in zip(BIG, recv):
        outs = _adamw(r, _rows2d(big[n]), _rows2d(big_m[n]), _rows2d(big_v[n]), "adamw_" + n)
        big_out[n] = [o_.reshape(big[n].shape) for o_ in outs]

    def table(parts, forget):
        t = jnp.concatenate([_rows2d(p) for p in parts] + [jnp.pad(forget, ((0, 0), (0, d - heads)))], axis=0)
        return jnp.pad(t, ((0, -t.shape[0] % 8), (0, 0)))

    g_table = table([jnp.stack(grads[n]) for n in SMALL], jnp.stack(grads["b_forget"]))
    rep_rows = g_table.shape[0]
    got = _exchange([jnp.concatenate([g_table, dmeta], axis=0)], True, "gather_small_grads")[0]
    outs = _adamw(got[:, :rep_rows], table([small[n] for n in SMALL], b_forget),
                  table([small_m[n] for n in SMALL], m_b_forget), table([small_v[n] for n in SMALL], v_b_forget),
                  "adamw_small")
    dcols = d // N_DEV
    meta_slots = lax.dynamic_slice_in_dim(got[:, rep_rows:], me * dcols, dcols, axis=2)
    meta_out = _adamw(meta_slots, meta_tokens, m_meta_tokens, v_meta_tokens, "adamw_meta")

    def ordered(k):
        t = outs[k]
        so = {n: t[a * depth:(a + 1) * depth] for a, n in enumerate(SMALL)}
        forget = t[len(SMALL) * depth:(len(SMALL) + 1) * depth, :heads]
        return (meta_out[k], so["norm_mix_pre"], so["norm_mix_post"], so["norm_ffn_pre"], so["norm_ffn_post"],
                big_out["w_in"][k], forget, big_out["w_pool"][k], so["pool_scale"], big_out["w_out"][k],
                big_out["w_gate"][k], big_out["w_up"][k], big_out["w_down"][k])

    return (loss, grad_x) + ordered(0) + ordered(1) + ordered(2) + ordered(3)
```

```python
import math

import jax
import jax.numpy as jnp
from jax import lax
from jax.experimental import pallas as pl
from jax.experimental.pallas import tpu as pltpu

F32 = jnp.float32
BF16 = jnp.bfloat16

N_DEV = 8
META_TOKENS = 16
PAD_ROWS = 112
CHUNK = 128
HEAD_DIM = 64
POOL_WINDOWS = (2, 4, 8, 16)
FORGET_PAD = 256
RMS_EPS = 1e-6
NEG_INF = -1e30
ADAM_LR, ADAM_B1, ADAM_B2, ADAM_EPS, ADAM_WD, ADAM_STEP = 0.001, 0.9, 0.999, 1e-08, 0.01, 10

VMEM_LIMIT = 56 * 1024 * 1024
VMEM_TILE_BUDGET = 36 * 1024 * 1024
ADAM_TILE_BYTES = 8 * 1024 * 1024
MESH_ID = pl.DeviceIdType.MESH


def _params(sem, vmem=VMEM_LIMIT):
    return pltpu.CompilerParams(dimension_semantics=sem, vmem_limit_bytes=vmem)


def _divisors(n, mult):
    return [d for d in range(mult, n + 1, mult) if n % d == 0]


def _row_tile(rows, cap):
    return max(d for d in _divisors(rows, CHUNK) if d <= max(cap, CHUNK))


def _fold8(x):
    r, c = x.shape
    return x.reshape(r // 8, 8, c).sum(axis=0)


def _split_bf16(x, parts):
    out = []
    for _ in range(parts - 1):
        hi = x.astype(BF16)
        out.append(hi)
        x = x - hi.astype(F32)
    out.append(x.astype(BF16))
    return out


def _apply01(mat, x, parts, left=True):
    acc = None
    for p in _split_bf16(x, parts):
        t = jnp.dot(mat, p, preferred_element_type=F32) if left else jnp.dot(p, mat, preferred_element_type=F32)
        acc = t if acc is None else acc + t
    return acc


def _matmul_tiles(m, n, k, mode, out_bytes):
    if mode == "tn":
        tk = _row_tile(k, 640)
    else:
        tk = max(d for d in _divisors(k, CHUNK) if d <= 1536)
    nk = k // tk
    best = None
    m_opts = _divisors(m, CHUNK)
    n_opts = _divisors(n, CHUNK)
    for tm in m_opts:
        for tn in n_opts:
            need = 2 * 2 * (tm * tk + tk * tn) + 2 * tm * tn * out_bytes
            if nk > 1 or mode == "tn":
                need += tm * tn * 4
            need += tm * tn * 4
            if need > VMEM_TILE_BUDGET:
                continue
            key = (tm * tn, tn)
            if best is None or key > best[0]:
                best = (key, tm, tn)
    return best[1], best[2], tk


def _matmul(a, b, mode, out_dtype, name):
    if mode == "nn":
        (m, k), (k2, n) = a.shape, b.shape
    elif mode == "nt":
        (m, k), (n, k2) = a.shape, b.shape
    else:
        (k, m), (k2, n) = a.shape, b.shape
    assert k == k2 and a.dtype == BF16 and b.dtype == BF16
    tm, tn, tk = _matmul_tiles(m, n, k, mode, jnp.dtype(out_dtype).itemsize)
    nk = k // tk
    if mode == "nn":
        a_spec = pl.BlockSpec((tm, tk), lambda i, j, r: (i, r))
        b_spec = pl.BlockSpec((tk, tn), lambda i, j, r: (r, j))
        dims = (((1,), (0,)), ((), ()))
    elif mode == "nt":
        a_spec = pl.BlockSpec((tm, tk), lambda i, j, r: (i, r))
        b_spec = pl.BlockSpec((tn, tk), lambda i, j, r: (j, r))
        dims = (((1,), (1,)), ((), ()))
    else:
        a_spec = pl.BlockSpec((tk, tm), lambda i, j, r: (r, i))
        b_spec = pl.BlockSpec((tk, tn), lambda i, j, r: (r, j))
        dims = (((0,), (0,)), ((), ()))

    def body(a_ref, b_ref, o_ref, *acc):
        part = lax.dot_general(a_ref[...], b_ref[...], dims, preferred_element_type=F32)
        if nk == 1:
            o_ref[...] = part.astype(o_ref.dtype)
        else:
            r = pl.program_id(2)

            @pl.when(r == 0)
            def _():
                acc[0][...] = part

            @pl.when(r > 0)
            def _():
                acc[0][...] += part

            @pl.when(r == nk - 1)
            def _():
                o_ref[...] = acc[0][...].astype(o_ref.dtype)

    return pl.pallas_call(
        body, name=name,
        out_shape=jax.ShapeDtypeStruct((m, n), out_dtype),
        grid=(m // tm, n // tn, nk),
        in_specs=[a_spec, b_spec],
        out_specs=pl.BlockSpec((tm, tn), lambda i, j, r: (i, j)),
        scratch_shapes=[pltpu.VMEM((tm, tn), F32)] if nk > 1 else [],
        compiler_params=_params(("parallel", "parallel", "arbitrary")),
    )(a, b)


def _rms(x, g):
    rstd = lax.rsqrt(jnp.mean(x * x, axis=-1, keepdims=True) + RMS_EPS)
    return x * rstd * g


def _norm_fwd(x, g):
    rows, d = x.shape
    tm = _row_tile(rows, 640)

    def body(x_ref, g_ref, h_ref):
        h_ref[...] = _rms(x_ref[...], g_ref[...]).astype(BF16)

    return pl.pallas_call(
        body, name="norm_fwd",
        out_shape=jax.ShapeDtypeStruct((rows, d), BF16),
        grid=(rows // tm,),
        in_specs=[pl.BlockSpec((tm, d), lambda i: (i, 0)), pl.BlockSpec((1, d), lambda i: (0, 0))],
        out_specs=pl.BlockSpec((tm, d), lambda i: (i, 0)),
        compiler_params=_params(("parallel",)),
    )(x, g.reshape(1, d))


def _resid_norm_fwd(h_res, y, g_post, g_next):
    rows, d = h_res.shape
    tm = _row_tile(rows, 640)

    def body(r_ref, y_ref, gp_ref, gn_ref, hn_ref, hx_ref):
        h_new = r_ref[...] + _rms(y_ref[...], gp_ref[...])
        hn_ref[...] = h_new
        hx_ref[...] = _rms(h_new, gn_ref[...]).astype(BF16)

    row = pl.BlockSpec((tm, d), lambda i: (i, 0))
    vec = pl.BlockSpec((1, d), lambda i: (0, 0))
    return pl.pallas_call(
        body, name="resid_norm_fwd",
        out_shape=(jax.ShapeDtypeStruct((rows, d), F32), jax.ShapeDtypeStruct((rows, d), BF16)),
        grid=(rows // tm,),
        in_specs=[row, row, vec, vec],
        out_specs=(row, row),
        compiler_params=_params(("parallel",)),
    )(h_res, y, g_post.reshape(1, d), g_next.reshape(1, d))


def _norm_bwd(x, g, dy, resid, out_dtype):
    rows, d = x.shape
    tm = _row_tile(rows, 640)
    has_resid = resid is not None

    def body(*refs):
        if has_resid:
            x_ref, g_ref, dy_ref, r_ref, dx_ref, dg_ref = refs
        else:
            x_ref, g_ref, dy_ref, dx_ref, dg_ref = refs
        xv = x_ref[...]
        dyv = dy_ref[...].astype(F32)
        rstd = lax.rsqrt(jnp.mean(xv * xv, axis=-1, keepdims=True) + RMS_EPS)
        xhat = xv * rstd
        gdy = dyv * g_ref[...]
        dx = rstd * (gdy - xhat * jnp.mean(gdy * xhat, axis=-1, keepdims=True))
        if has_resid:
            dx = dx + r_ref[...]
        dx_ref[...] = dx.astype(dx_ref.dtype)

        @pl.when(pl.program_id(0) == 0)
        def _():
            dg_ref[...] = jnp.zeros_like(dg_ref)

        dg_ref[...] += _fold8(dyv * xhat)

    row = pl.BlockSpec((tm, d), lambda i: (i, 0))
    vec = pl.BlockSpec((1, d), lambda i: (0, 0))
    args = [x, g.reshape(1, d), dy] + ([resid] if has_resid else [])
    dx, dg = pl.pallas_call(
        body, name="norm_bwd_resid" if has_resid else "norm_bwd",
        out_shape=(jax.ShapeDtypeStruct((rows, d), out_dtype), jax.ShapeDtypeStruct((8, d), F32)),
        grid=(rows // tm,),
        in_specs=[row, vec, row] + ([row] if has_resid else []),
        out_specs=(row, pl.BlockSpec((8, d), lambda i: (0, 0))),
        compiler_params=_params(("arbitrary",)),
    )(*args)
    return dx, dg.sum(axis=0)


def _swiglu_fwd(ab):
    rows, two_f = ab.shape
    f = two_f // 2
    tm = _row_tile(rows, 640)

    def body(a_ref, b_ref, o_ref):
        a = a_ref[...]
        o_ref[...] = (a * jax.nn.sigmoid(a) * b_ref[...]).astype(BF16)

    return pl.pallas_call(
        body, name="swiglu_fwd",
        out_shape=jax.ShapeDtypeStruct((rows, f), BF16),
        grid=(rows // tm,),
        in_specs=[pl.BlockSpec((tm, f), lambda i: (i, 0)), pl.BlockSpec((tm, f), lambda i: (i, 1))],
        out_specs=pl.BlockSpec((tm, f), lambda i: (i, 0)),
        compiler_params=_params(("parallel",)),
    )(ab, ab)


def _swiglu_bwd(ab, dff):
    rows, two_f = ab.shape
    f = two_f // 2
    tm = _row_tile(rows, 256)

    def body(a_ref, b_ref, d_ref, dab_ref):
        a = a_ref[...]
        d = d_ref[...]
        s = jax.nn.sigmoid(a)
        dab_ref[:, :f] = (d * b_ref[...] * (s * (1.0 + a * (1.0 - s)))).astype(BF16)
        dab_ref[:, f:] = (d * (a * s)).astype(BF16)

    lo = pl.BlockSpec((tm, f), lambda i: (i, 0))
    hi = pl.BlockSpec((tm, f), lambda i: (i, 1))
    return pl.pallas_call(
        body, name="swiglu_bwd",
        out_shape=jax.ShapeDtypeStruct((rows, two_f), BF16),
        grid=(rows // tm,),
        in_specs=[lo, hi, lo],
        out_specs=pl.BlockSpec((tm, two_f), lambda i: (i, 0)),
        compiler_params=_params(("parallel",)),
    )(ab, ab, dff)


def _tri(lower):
    r = lax.broadcasted_iota(jnp.int32, (CHUNK, CHUNK), 0)
    c = lax.broadcasted_iota(jnp.int32, (CHUNK, CHUNK), 1)
    return jnp.where((r >= c) if lower else (r <= c), 1.0, 0.0).astype(BF16)


def _logf_group(rows):
    n = rows // CHUNK
    return max(g for g in range(1, 9) if n % g == 0)


def _logf_fwd(z, b):
    h, rows = z.shape
    g = _logf_group(rows)
    n = rows // (g * CHUNK)

    def body(z_ref, b_ref, f_ref, carry):
        i = pl.program_id(0)

        @pl.when(i == 0)
        def _():
            carry[...] = jnp.zeros_like(carry)

        c = carry[...]
        for t in range(g):
            sl = slice(t * CHUNK, (t + 1) * CHUNK)
            x = z_ref[:, sl] + b_ref[...]
            lf = jnp.minimum(x, 0.0) - jnp.log(1.0 + jnp.exp(-jnp.abs(x)))
            col = (i * g + t) * CHUNK + lax.broadcasted_iota(jnp.int32, (1, CHUNK), 1)
            lf = jnp.where(col >= PAD_ROWS, lf, 0.0)
            run = _apply01(_tri(False), lf, 3, left=False) + c
            f_ref[:, sl] = run
            c = jnp.broadcast_to(run[:, CHUNK - 1:CHUNK], c.shape)
        carry[...] = c

    blk = pl.BlockSpec((h, g * CHUNK), lambda i: (0, i))
    return pl.pallas_call(
        body, name="logf_fwd",
        out_shape=jax.ShapeDtypeStruct((h, rows), F32),
        grid=(n,),
        in_specs=[blk, pl.BlockSpec((h, CHUNK), lambda i: (0, 0))],
        out_specs=blk,
        scratch_shapes=[pltpu.VMEM((h, CHUNK), F32)],
        compiler_params=_params(("arbitrary",)),
    )(z, b)


def _logf_bwd(df, z, b):
    h, rows = z.shape
    g = _logf_group(rows)
    n = rows // (g * CHUNK)

    def body(df_ref, z_ref, b_ref, dz_ref, db_ref, carry):
        i = pl.program_id(0)

        @pl.when(i == 0)
        def _():
            carry[...] = jnp.zeros_like(carry)
            db_ref[...] = jnp.zeros_like(db_ref)

        c = carry[...]
        db = db_ref[...]
        for t in reversed(range(g)):
            sl = slice(t * CHUNK, (t + 1) * CHUNK)
            run = _apply01(_tri(True), df_ref[:, sl], 3, left=False) + c
            c = jnp.broadcast_to(run[:, 0:1], c.shape)
            x = z_ref[:, sl] + b_ref[...]
            col = ((n - 1 - i) * g + t) * CHUNK + lax.broadcasted_iota(jnp.int32, (1, CHUNK), 1)
            dz = jnp.where(col >= PAD_ROWS, run * (1.0 - jax.nn.sigmoid(x)), 0.0)
            dz_ref[:, sl] = dz
            db = db + dz
        carry[...] = c
        db_ref[...] = db

    rev = pl.BlockSpec((h, g * CHUNK), lambda i: (0, n - 1 - i))
    fix = pl.BlockSpec((h, CHUNK), lambda i: (0, 0))
    dz, db = pl.pallas_call(
        body, name="logf_bwd",
        out_shape=(jax.ShapeDtypeStruct((h, rows), F32), jax.ShapeDtypeStruct((h, CHUNK), F32)),
        grid=(n,),
        in_specs=[rev, rev, fix],
        out_specs=(rev, fix),
        scratch_shapes=[pltpu.VMEM((h, CHUNK), F32)],
        compiler_params=_params(("arbitrary",)),
    )(df, z, b)
    return dz, db.sum(axis=1)


def _lane_lo():
    return lax.broadcasted_iota(jnp.int32, (1, CHUNK), 1) < HEAD_DIM


def _attn_block(rows):
    return _row_tile(rows, min(640, rows // 2))


def _masked_logits(s, i, j, blk):
    row = i * blk + lax.broadcasted_iota(jnp.int32, (blk, 1), 0)
    col = j * blk + lax.broadcasted_iota(jnp.int32, (1, blk), 1)
    return jnp.where(col <= row, jnp.where(col >= PAD_ROWS, s, NEG_INF), NEG_INF)


def _attn_prep(qkv, f_t, d):
    rows = qkv.shape[0]
    heads = d // HEAD_DIM
    hp = heads // 2
    tm = _row_tile(rows, 640)
    scale = 1.0 / math.sqrt(HEAD_DIM)

    def body(q_ref, k_ref, v_ref, f_ref, qa_ref, ka_ref, vo_ref, va_ref):
        pr = pl.program_id(1)
        lane = lax.broadcasted_iota(jnp.int32, (1, CHUNK), 1)
        lo = lane < HEAD_DIM
        q2 = (q_ref[...].astype(F32) * scale).astype(BF16)
        k2 = k_ref[...]
        v2 = v_ref[...]
        zero = jnp.zeros_like(k2)
        head_id = lax.broadcasted_iota(jnp.int32, (1, heads), 1)
        ft = f_ref[...]
        for hh in range(2):
            base = HEAD_DIM if hh == 0 else 0
            neg_f = -jnp.sum(jnp.where(head_id == 2 * pr + hh, ft, 0.0), axis=1, keepdims=True)
            aug = zero
            ones = zero
            for t, part in enumerate(_split_bf16(neg_f, 3)):
                aug = jnp.where(lane == base + t, part, aug)
                ones = jnp.where(lane == base + t, jnp.ones_like(zero), ones)
            own = lo if hh == 0 else jnp.logical_not(lo)
            sl = slice(hh * CHUNK, (hh + 1) * CHUNK)
            qa_ref[:, sl] = jnp.where(own, q2, ones)
            ka_ref[:, sl] = jnp.where(own, k2, aug)
            vo_ref[:, sl] = jnp.where(own, v2, zero)
            va_ref[:, sl] = jnp.where(own, v2, jnp.where(lane == base, jnp.ones_like(zero), zero))

    pair_in = lambda c: pl.BlockSpec((tm, CHUNK), lambda i, p: (i, c * hp + p))
    pair_out = pl.BlockSpec((tm, 2 * CHUNK), lambda i, p: (i, p))
    sds = jax.ShapeDtypeStruct((rows, 2 * d), BF16)
    return pl.pallas_call(
        body, name="attn_prep",
        out_shape=(sds, sds, sds, sds),
        grid=(rows // tm, hp),
        in_specs=[pair_in(0), pair_in(1), pair_in(2), pl.BlockSpec((tm, heads), lambda i, p: (i, 0))],
        out_specs=(pair_out, pair_out, pair_out, pair_out),
        compiler_params=_params(("parallel", "parallel")),
    )(qkv, qkv, qkv, f_t)


FWD_PAIRS = 2


def _attn_fwd(q_aug, k_aug, v_aug, d):
    rows = q_aug.shape[0]
    hp = d // CHUNK
    blk = _attn_block(rows)
    nb = rows // blk
    nt = (((1,), (1,)), ((), ()))
    den_lane = (HEAD_DIM, 0)
    gp = FWD_PAIRS if hp % FWD_PAIRS == 0 else 1
    nh = 2 * gp

    def body(qi_ref, kj_ref, q_ref, k_ref, v_ref, o_ref, lse_ref, m_s, acc_s):
        i = qi_ref[pl.program_id(1)]
        j = kj_ref[pl.program_id(1)]

        @pl.when(j == 0)
        def _():
            m_s[...] = jnp.full(m_s.shape, NEG_INF, F32)
            acc_s[...] = jnp.zeros_like(acc_s)

        def step(masked):
            tiles = [slice(hh * CHUNK, (hh + 1) * CHUNK) for hh in range(nh)]
            scores = [lax.dot_general(q_ref[:, ln], k_ref[:, ln], nt, preferred_element_type=F32) for ln in tiles]
            for hh in range(nh):
                s = scores[hh]
                if masked:
                    s = _masked_logits(s, i, j, blk)
                m_prev = m_s[hh]
                m_new = jnp.maximum(m_prev, s.max(axis=-1, keepdims=True))
                p = jnp.exp(s - m_new)
                m_s[hh] = m_new
                pv = jnp.dot(p.astype(BF16), v_ref[:, tiles[hh]], preferred_element_type=F32)
                acc_s[hh] = acc_s[hh] * jnp.exp(m_prev - m_new) + pv

        edge = (j == i) | (j == 0)

        @pl.when(edge)
        def _():
            step(True)

        @pl.when(jnp.logical_not(edge))
        def _():
            step(False)

        @pl.when(j == i)
        def _():
            row = i * blk + lax.broadcasted_iota(jnp.int32, (blk, 1), 0)
            lo = _lane_lo()
            for pr in range(gp):
                acc = [acc_s[2 * pr + hh] for hh in range(2)]
                den = [acc[hh][:, den_lane[hh]:den_lane[hh] + 1] for hh in range(2)]
                o = jnp.where(lo, acc[0] * (1.0 / den[0]), acc[1] * (1.0 / den[1]))
                sl = slice(pr * CHUNK, (pr + 1) * CHUNK)
                o_ref[:, sl] = jnp.where(row >= PAD_ROWS, o, 0.0)
                lse_ref[:, sl] = jnp.where(lo, m_s[2 * pr] + jnp.log(den[0]), m_s[2 * pr + 1] + jnp.log(den[1]))

    pairs = [(i, j) for i in range(nb) for j in range(i + 1)]
    q_spec = pl.BlockSpec((blk, nh * CHUNK), lambda h, t, qi, kj: (qi[t], h))
    k_spec = pl.BlockSpec((blk, nh * CHUNK), lambda h, t, qi, kj: (kj[t], h))
    o_spec = pl.BlockSpec((blk, gp * CHUNK), lambda h, t, qi, kj: (qi[t], h))
    return pl.pallas_call(
        body, name="attn_fwd",
        out_shape=(jax.ShapeDtypeStruct((rows, d), F32), jax.ShapeDtypeStruct((rows, d), F32)),
        grid_spec=pltpu.PrefetchScalarGridSpec(
            num_scalar_prefetch=2, grid=(hp // gp, len(pairs)),
            in_specs=[q_spec, k_spec, k_spec], out_specs=(o_spec, o_spec),
            scratch_shapes=[pltpu.VMEM((nh, blk, 1), F32), pltpu.VMEM((nh, blk, CHUNK), F32)]),
        compiler_params=_params(("parallel", "arbitrary")),
    )(jnp.array([p[0] for p in pairs], jnp.int32), jnp.array([p[1] for p in pairs], jnp.int32),
      q_aug, k_aug, v_aug)


def _attn_bwd(q_aug, k_aug, v_own, do, lse, delta, d):
    rows = q_aug.shape[0]
    hp = d // CHUNK
    blk = _attn_block(rows)
    nb = rows // blk
    scale = 1.0 / math.sqrt(HEAD_DIM)
    nt = (((1,), (1,)), ((), ()))
    tn = (((0,), (0,)), ((), ()))

    def body(kj_ref, qi_ref, q_ref, k_ref, v_ref, do_ref, lse_ref, dl_ref, dq_ref, dk_ref, dv_ref, df_ref, rs_ref,
             dq_s, dk_s, dv_s, df_s, rs_s):
        j = kj_ref[pl.program_id(1)]
        i = qi_ref[pl.program_id(1)]
        lo = _lane_lo()

        @pl.when((j == 0) & (i == 0))
        def _():
            dq_s[...] = jnp.zeros_like(dq_s)
            rs_s[...] = jnp.zeros_like(rs_s)

        @pl.when(i == j)
        def _():
            dk_s[...] = jnp.zeros_like(dk_s)
            dv_s[...] = jnp.zeros_like(dv_s)
            df_s[...] = jnp.zeros_like(df_s)

        def step(masked):
            dov = do_ref[...]
            t_dq, t_dk, t_dv, row_sums = [], [], [], []
            for hh in range(2):
                lanes = slice(hh * CHUNK, (hh + 1) * CHUNK)
                qh, kh = q_ref[:, lanes], k_ref[:, lanes]
                off = hh * HEAD_DIM
                s = lax.dot_general(qh, kh, nt, preferred_element_type=F32)
                if masked:
                    s = _masked_logits(s, i, j, blk)
                p = jnp.exp(s - lse_ref[:, off:off + 1])
                dp = lax.dot_general(dov, v_ref[:, lanes], nt, preferred_element_type=F32)
                ds = p * (dp - dl_ref[:, off:off + 1])
                df_s[hh:hh + 1, :] += ds.sum(axis=0, keepdims=True)
                row_sums.append(ds.sum(axis=1, keepdims=True))
                pb = p.astype(BF16)
                dsb = ds.astype(BF16)
                t_dv.append(lax.dot_general(pb, dov, tn, preferred_element_type=F32))
                t_dk.append(lax.dot_general(dsb, qh, tn, preferred_element_type=F32))
                t_dq.append(jnp.dot(dsb, kh, preferred_element_type=F32))
            dv_s[...] += jnp.where(lo, t_dv[0], t_dv[1])
            dk_s[...] += jnp.where(lo, t_dk[0], t_dk[1])
            r0 = pl.multiple_of(i * blk, blk)
            dq_s[pl.ds(r0, blk), :] += jnp.where(lo, t_dq[0], t_dq[1])
            rs_s[pl.ds(r0, blk), :] += jnp.where(lo, row_sums[0], row_sums[1])

        edge = (j == i) | (j == 0)

        @pl.when(edge)
        def _():
            step(True)

        @pl.when(jnp.logical_not(edge))
        def _():
            step(False)

        @pl.when(i == nb - 1)
        def _():
            dk_ref[...] = dk_s[...].astype(BF16)
            dv_ref[...] = dv_s[...].astype(BF16)
            df_ref[...] = -df_s[...]

        @pl.when((i == nb - 1) & (j == nb - 1))
        def _():
            dq_ref[...] = (dq_s[...] * scale).astype(BF16)
            rs_ref[...] = rs_s[...]

    pairs = [(j, i) for j in range(nb) for i in range(j, nb)]
    q_spec = pl.BlockSpec((blk, 2 * CHUNK), lambda h, t, kj, qi: (qi[t], h))
    kv_spec = pl.BlockSpec((blk, 2 * CHUNK), lambda h, t, kj, qi: (kj[t], h))
    row_spec = pl.BlockSpec((blk, CHUNK), lambda h, t, kj, qi: (qi[t], h))
    kv_out = pl.BlockSpec((blk, CHUNK), lambda h, t, kj, qi: (kj[t], h))
    dq_out = pl.BlockSpec((rows, CHUNK), lambda h, t, kj, qi: (0, h))
    return pl.pallas_call(
        body, name="attn_bwd",
        out_shape=(jax.ShapeDtypeStruct((rows, d), BF16), jax.ShapeDtypeStruct((rows, d), BF16),
                   jax.ShapeDtypeStruct((rows, d), BF16), jax.ShapeDtypeStruct((hp, 2, rows), F32),
                   jax.ShapeDtypeStruct((rows, d), F32)),
        grid_spec=pltpu.PrefetchScalarGridSpec(
            num_scalar_prefetch=2, grid=(hp, len(pairs)),
            in_specs=[q_spec, kv_spec, kv_spec, row_spec, row_spec, row_spec],
            out_specs=(dq_out, kv_out, kv_out,
                       pl.BlockSpec((None, 2, blk), lambda h, t, kj, qi: (h, 0, kj[t])), dq_out),
            scratch_shapes=[pltpu.VMEM((rows, CHUNK), F32), pltpu.VMEM((blk, CHUNK), F32),
                            pltpu.VMEM((blk, CHUNK), F32), pltpu.VMEM((2, blk), F32),
                            pltpu.VMEM((rows, CHUNK), F32)]),
        compiler_params=_params(("parallel", "arbitrary")),
    )(jnp.array([p[0] for p in pairs], jnp.int32), jnp.array([p[1] for p in pairs], jnp.int32),
      q_aug, k_aug, v_own, do, lse, delta)


def _band(w, transposed, other):
    r = lax.broadcasted_iota(jnp.int32, (CHUNK, CHUNK), 0)
    c = lax.broadcasted_iota(jnp.int32, (CHUNK, CHUNK), 1)
    dist = (c - r) if transposed else (r - c)
    if other:
        dist = dist + CHUNK
    return jnp.where(dist >= 0, jnp.where(dist < w, 1.0, 0.0), 0.0).astype(BF16)


def _inv_count(chunk_index, w):
    row = chunk_index * CHUNK + lax.broadcasted_iota(jnp.int32, (CHUNK, 1), 0)
    cnt = jnp.clip(row - PAD_ROWS + 1, 1, w).astype(F32)
    return 1.0 / cnt


def _pool_diff(u_cur, u_prev, i, w):
    ws = _apply01(_band(w, False, False), u_cur, 3)
    ws = ws + jnp.where(i > 0, _apply01(_band(w, False, True), u_prev, 3), 0.0)
    return ws * _inv_count(i, w) - u_cur


def _pool_merge_fwd(rest, o, w_pool, scale, d):
    rows = rest.shape[0]
    n = rows // CHUNK
    cg = d // len(POOL_WINDOWS)

    def body(up_ref, uc_ref, gp_ref, ga_ref, o_ref, wp_ref, sc_ref, mg_ref, yp_ref):
        i = pl.program_id(0)
        for g, w in enumerate(POOL_WINDOWS):
            sl = slice(g * cg, (g + 1) * cg)
            diff = _pool_diff(uc_ref[:, sl], up_ref[:, sl], i, w)
            ypre = jnp.dot(diff.astype(BF16), wp_ref[g], preferred_element_type=F32)
            yp_ref[:, sl] = ypre
            merged = (jax.nn.sigmoid(gp_ref[:, sl]) * (ypre * sc_ref[:, sl])
                      + jax.nn.sigmoid(ga_ref[:, sl]) * o_ref[:, sl])
            mg_ref[:, sl] = merged.astype(BF16)

    col = lambda c: pl.BlockSpec((CHUNK, d), lambda i: (i, c))
    return pl.pallas_call(
        body, name="pool_merge_fwd",
        out_shape=(jax.ShapeDtypeStruct((rows, d), BF16), jax.ShapeDtypeStruct((rows, d), F32)),
        grid=(n,),
        in_specs=[pl.BlockSpec((CHUNK, d), lambda i: (jnp.maximum(i - 1, 0), 0)), col(0), col(1), col(2), col(0),
                  pl.BlockSpec((len(POOL_WINDOWS), cg, cg), lambda i: (0, 0, 0)),
                  pl.BlockSpec((1, d), lambda i: (0, 0))],
        out_specs=(col(0), col(0)),
        compiler_params=_params(("parallel",)),
    )(rest, rest, rest, rest, o, w_pool, scale.reshape(1, d))


def _gate_bwd(dm, rest, o, ypre, scale, d):
    rows = rest.shape[0]
    n = rows // CHUNK

    def body(dm_ref, gp_ref, ga_ref, o_ref, yp_ref, sc_ref, dgp_ref, dga_ref, do_ref, dl_ref, dy_ref, ds_ref):
        @pl.when(pl.program_id(0) == 0)
        def _():
            ds_ref[...] = jnp.zeros_like(ds_ref)

        dmv = dm_ref[...]
        sp = jax.nn.sigmoid(gp_ref[...])
        sa = jax.nn.sigmoid(ga_ref[...])
        ov = o_ref[...]
        ypre_v = yp_ref[...]
        sc = sc_ref[...]
        dgp_ref[...] = (dmv * (ypre_v * sc) * (sp * (1.0 - sp))).astype(BF16)
        dga_ref[...] = (dmv * ov * (sa * (1.0 - sa))).astype(BF16)
        t = dmv * sp
        dy_ref[...] = (t * sc).astype(BF16)
        ds_ref[...] += _fold8(t * ypre_v)
        dob = (dmv * sa).astype(BF16)
        do_ref[...] = dob
        prod = dob.astype(F32) * ov
        lo = _lane_lo()
        for pr in range(d // CHUNK):
            sl = slice(pr * CHUNK, (pr + 1) * CHUNK)
            tp = prod[:, sl]
            s_lo = jnp.where(lo, tp, 0.0).sum(axis=-1, keepdims=True)
            s_hi = jnp.where(lo, 0.0, tp).sum(axis=-1, keepdims=True)
            dl_ref[:, sl] = jnp.where(lo, s_lo, s_hi)

    col = lambda c: pl.BlockSpec((CHUNK, d), lambda i: (i, c))
    row_bf = jax.ShapeDtypeStruct((rows, d), BF16)
    outs = pl.pallas_call(
        body, name="gate_bwd",
        out_shape=(row_bf, row_bf, row_bf, jax.ShapeDtypeStruct((rows, d), F32), row_bf,
                   jax.ShapeDtypeStruct((8, d), F32)),
        grid=(n,),
        in_specs=[col(0), col(1), col(2), col(0), col(0), pl.BlockSpec((1, d), lambda i: (0, 0))],
        out_specs=(col(0), col(0), col(0), col(0), col(0), pl.BlockSpec((8, d), lambda i: (0, 0))),
        compiler_params=_params(("arbitrary",)),
    )(dm, rest, rest, o, ypre, scale.reshape(1, d))
    return outs[:5] + (outs[5].sum(axis=0),)


def _pool_bwd(dypre, rest, w_pool, d):
    rows = rest.shape[0]
    n = rows // CHUNK
    ng = len(POOL_WINDOWS)
    cg = d // ng
    nt = (((1,), (1,)), ((), ()))
    tn = (((0,), (0,)), ((), ()))

    def body(dc_ref, dn_ref, up_ref, uc_ref, wp_ref, du_ref, dw_ref):
        i = pl.program_id(0)

        @pl.when(i == 0)
        def _():
            dw_ref[...] = jnp.zeros_like(dw_ref)

        row = i * CHUNK + lax.broadcasted_iota(jnp.int32, (CHUNK, 1), 0)
        for g, w in enumerate(POOL_WINDOWS):
            sl = slice(g * cg, (g + 1) * cg)
            diff = _pool_diff(uc_ref[:, sl], up_ref[:, sl], i, w)
            dyc = dc_ref[:, sl]
            dw_ref[g] += lax.dot_general(diff.astype(BF16), dyc, tn, preferred_element_type=F32)
            wg = wp_ref[g]
            dd_cur = lax.dot_general(dyc, wg, nt, preferred_element_type=F32)
            dd_next = lax.dot_general(dn_ref[:, sl], wg, nt, preferred_element_type=F32)
            du = _apply01(_band(w, True, False), dd_cur * _inv_count(i, w), 2)
            du = du + jnp.where(i < n - 1, _apply01(_band(w, True, True), dd_next * _inv_count(i + 1, w), 2), 0.0)
            du = du - dd_cur
            du_ref[:, sl] = jnp.where(row >= PAD_ROWS, du, 0.0).astype(BF16)

    cur = pl.BlockSpec((CHUNK, d), lambda i: (i, 0))
    return pl.pallas_call(
        body, name="pool_bwd",
        out_shape=(jax.ShapeDtypeStruct((rows, d), BF16), jax.ShapeDtypeStruct((ng, cg, cg), F32)),
        grid=(n,),
        in_specs=[cur, pl.BlockSpec((CHUNK, d), lambda i: (jnp.minimum(i + 1, n - 1), 0)),
                  pl.BlockSpec((CHUNK, d), lambda i: (jnp.maximum(i - 1, 0), 0)), cur,
                  pl.BlockSpec((ng, cg, cg), lambda i: (0, 0, 0))],
        out_specs=(cur, pl.BlockSpec((ng, cg, cg), lambda i: (0, 0, 0))),
        compiler_params=_params(("arbitrary",)),
    )(dypre, dypre, rest, rest, w_pool)


def _loss_grad(h_res, target):
    rows, d = h_res.shape
    n = rows // CHUNK

    def body(h_ref, t_ref, dh_ref, acc_ref):
        i = pl.program_id(0)

        @pl.when(i == 0)
        def _():
            acc_ref[...] = jnp.zeros_like(acc_ref)
            dh_ref[...] = jnp.zeros_like(dh_ref)

        @pl.when(i > 0)
        def _():
            err = h_ref[...] - t_ref[...]
            dh_ref[...] = err * (1.0 / d)
            e2 = _fold8(err * err)
            part = e2[:, 0:CHUNK]
            for c in range(1, d // CHUNK):
                part = part + e2[:, c * CHUNK:(c + 1) * CHUNK]
            acc_ref[...] += part

    dh, acc = pl.pallas_call(
        body, name="loss_grad",
        out_shape=(jax.ShapeDtypeStruct((rows, d), F32), jax.ShapeDtypeStruct((8, CHUNK), F32)),
        grid=(n,),
        in_specs=[pl.BlockSpec((CHUNK, d), lambda i: (i, 0)),
                  pl.BlockSpec((CHUNK, d), lambda i: (jnp.maximum(i - 1, 0), 0))],
        out_specs=(pl.BlockSpec((CHUNK, d), lambda i: (i, 0)), pl.BlockSpec((8, CHUNK), lambda i: (0, 0))),
        compiler_params=_params(("arbitrary",)),
    )(h_res, target)
    return dh, (0.5 / d) * acc.sum()


def _adamw(slots, w, m, v, name):
    rows, cols = w.shape
    lanes = -(-cols // CHUNK) * CHUNK
    row_bytes = lanes * (N_DEV * slots.dtype.itemsize + 7 * 4)
    tr = max(t for t in _divisors(rows, 8) if t <= max(8, ADAM_TILE_BYTES // row_bytes))
    c1 = 1.0 - ADAM_B1 ** ADAM_STEP
    c2 = 1.0 - ADAM_B2 ** ADAM_STEP

    def body(s_ref, w_ref, m_ref, v_ref, g_ref, d_ref, mo_ref, vo_ref):
        g = s_ref[0].astype(F32)
        for k in range(1, N_DEV):
            g = g + s_ref[k].astype(F32)
        m_new = ADAM_B1 * m_ref[...] + (1.0 - ADAM_B1) * g
        v_new = ADAM_B2 * v_ref[...] + (1.0 - ADAM_B2) * (g * g)
        m_hat = m_new / c1
        v_hat = v_new / c2
        g_ref[...] = g
        d_ref[...] = -ADAM_LR * (m_hat / (jnp.sqrt(v_hat) + ADAM_EPS) + ADAM_WD * w_ref[...])
        mo_ref[...] = m_new
        vo_ref[...] = v_new

    tile = pl.BlockSpec((tr, cols), lambda i: (i, 0))
    sds = jax.ShapeDtypeStruct((rows, cols), F32)
    return pl.pallas_call(
        body, name=name,
        out_shape=(sds, sds, sds, sds),
        grid=(rows // tr,),
        in_specs=[pl.BlockSpec((N_DEV, tr, cols), lambda i: (0, i, 0)), tile, tile, tile],
        out_specs=(tile, tile, tile, tile),
        compiler_params=_params(("parallel",)),
    )(slots, w, m, v)


def _exchange(srcs, gather, name):
    n = len(srcs)
    shapes = [((N_DEV,) + s.shape) if gather else s.shape for s in srcs]

    def body(*refs):
        src_refs, out_refs = refs[:n], refs[n:2 * n]
        send_sems, recv_sems, local_sems = refs[2 * n:]
        x, y, c = lax.axis_index("x"), lax.axis_index("y"), lax.axis_index("c")
        me = 4 * x + 2 * y + c

        def payload(a, slot):
            return src_refs[a] if gather else src_refs[a].at[slot]

        own = [pltpu.make_async_copy(payload(a, me), out_refs[a].at[me], local_sems.at[a]) for a in range(n)]
        for cp in own:
            cp.start()
        sends, recvs = [], []
        for k in range(1, N_DEV):
            px = 1 - x if k & 4 else x
            py = 1 - y if k & 2 else y
            pc = 1 - c if k & 1 else c
            peer = 4 * px + 2 * py + pc
            for a in range(n):
                sems = dict(send_sem=send_sems.at[(k - 1) * n + a], recv_sem=recv_sems.at[(k - 1) * n + a],
                            device_id=(px, py, pc), device_id_type=MESH_ID)
                sends.append(pltpu.make_async_remote_copy(src_ref=payload(a, peer), dst_ref=out_refs[a].at[me], **sems))
                recvs.append(pltpu.make_async_remote_copy(src_ref=payload(a, peer), dst_ref=out_refs[a].at[peer], **sems))
        for cp in sends:
            cp.start()
        for cp in recvs:
            cp.wait_recv()
        for cp in sends:
            cp.wait_send()
        for cp in own:
            cp.wait()

    return pl.pallas_call(
        body, name=name,
        out_shape=tuple(jax.ShapeDtypeStruct(sh, s.dtype) for sh, s in zip(shapes, srcs)),
        in_specs=[pl.BlockSpec(memory_space=pl.ANY)] * n,
        out_specs=tuple([pl.BlockSpec(memory_space=pl.ANY)] * n),
        scratch_shapes=[pltpu.SemaphoreType.DMA(((N_DEV - 1) * n,)), pltpu.SemaphoreType.DMA(((N_DEV - 1) * n,)),
                        pltpu.SemaphoreType.DMA((n,))],
    )(*srcs)


HBM_SPEC = pl.BlockSpec(memory_space=pltpu.HBM)
SEM_SPEC = pl.BlockSpec(memory_space=pltpu.SEMAPHORE)
DATAFLOW = pltpu.SideEffectType.DATAFLOW_SIDE_EFFECTING


def _peer_copies(src_refs, land_refs, send_sems, recv_sems, gather):
    n = len(src_refs)
    x, y, c = lax.axis_index("x"), lax.axis_index("y"), lax.axis_index("c")
    me = 4 * x + 2 * y + c
    sends, lands = [], []
    for k in range(1, N_DEV):
        px = 1 - x if k & 4 else x
        py = 1 - y if k & 2 else y
        pc = 1 - c if k & 1 else c
        peer = 4 * px + 2 * py + pc
        for a in range(n):
            src = src_refs[a] if gather else src_refs[a].at[peer]
            sems = dict(send_sem=send_sems.at[(k - 1) * n + a], recv_sem=recv_sems.at[(k - 1) * n + a],
                        device_id=(px, py, pc), device_id_type=MESH_ID)
            sends.append(pltpu.make_async_remote_copy(src_ref=src, dst_ref=land_refs[a].at[me], **sems))
            lands.append(pltpu.make_async_remote_copy(src_ref=src, dst_ref=land_refs[a].at[peer], **sems))
    return sends, lands


def _exchange_start(srcs, gather, name):
    n = len(srcs)
    shapes = [((N_DEV,) + s.shape) if gather else s.shape for s in srcs]
    n_sem = (N_DEV - 1) * n

    def body(*refs):
        src_refs, land_refs = refs[:n], refs[n:2 * n]
        send_sems, recv_sems = refs[2 * n], refs[2 * n + 1]
        token = refs[-1]
        sends, _ = _peer_copies(src_refs, land_refs, send_sems, recv_sems, gather)
        for cp in sends:
            cp.start()
        token[...] = jnp.zeros_like(token)

    hbm = lambda arrays_shapes: [pltpu.HBM(sh, dt) for sh, dt in arrays_shapes]
    src_types = [(s.shape, s.dtype) for s in srcs]
    land_types = [(sh, s.dtype) for sh, s in zip(shapes, srcs)]
    outs = pl.pallas_call(
        body, name=name,
        out_shape=(pltpu.SemaphoreType.DMA((n_sem,)), pltpu.SemaphoreType.DMA((n_sem,)),
                   *hbm(src_types), *hbm(land_types), jax.ShapeDtypeStruct((8, CHUNK), F32)),
        in_specs=[HBM_SPEC] * (2 * n),
        out_specs=(SEM_SPEC, SEM_SPEC, *([HBM_SPEC] * (2 * n)), pl.BlockSpec(memory_space=pltpu.VMEM)),
        input_output_aliases={a: 2 + a for a in range(2 * n)},
        compiler_params=pltpu.CompilerParams(has_side_effects=DATAFLOW),
    )(*[pltpu.with_memory_space_constraint(s, pltpu.HBM) for s in srcs],
      *[pltpu.with_memory_space_constraint(lax.empty(sh, dt), pltpu.HBM) for sh, dt in land_types])
    return (outs[0], outs[1], outs[2:2 + n], outs[2 + n:2 + 2 * n]), outs[-1]


def _exchange_wait(handle, after, gather, name):
    send_sems, recv_sems, src_thru, land_thru = handle
    n = len(src_thru)

    def body(*refs):
        src_refs, land_refs = refs[:n], refs[n:2 * n]
        _, lands = _peer_copies(src_refs, land_refs, refs[2 * n], refs[2 * n + 1], gather)
        for cp in lands:
            cp.wait_send()
            cp.wait_recv()

    outs = pl.pallas_call(
        body, name=name,
        out_shape=tuple(pltpu.HBM(t.shape, t.dtype) for t in (*src_thru, *land_thru)),
        in_specs=[*([HBM_SPEC] * (2 * n)), SEM_SPEC, SEM_SPEC, pl.BlockSpec(memory_space=pl.ANY)],
        out_specs=tuple([HBM_SPEC] * (2 * n)),
        input_output_aliases={a: a for a in range(2 * n)},
        compiler_params=pltpu.CompilerParams(has_side_effects=DATAFLOW),
    )(*src_thru, *land_thru, send_sems, recv_sems, after)
    return outs[n:]


def _fill_own(land, own, me):
    return lax.dynamic_update_slice_in_dim(land, own[None].astype(land.dtype), me, axis=0)


BIG = ("w_in", "w_pool", "w_out", "w_gate", "w_up", "w_down")
SHARD_AXIS = dict(w_in=-1, w_pool=-2, w_out=-2, w_gate=-1, w_up=-1, w_down=-2)
SMALL = ("norm_mix_pre", "norm_mix_post", "norm_ffn_pre", "norm_ffn_post", "pool_scale")


def _join(g, axis):
    return jnp.concatenate([g[j] for j in range(N_DEV)], axis=axis)


def _split(full, axis):
    return jnp.stack(jnp.split(full, N_DEV, axis=axis))


def _rows2d(a):
    return a.reshape(-1, a.shape[-1])


def kernel(x, meta_tokens, norm_mix_pre, norm_mix_post, norm_ffn_pre, norm_ffn_post, w_in, b_forget, w_pool, pool_scale, w_out, w_ffn_gate, w_ffn_up, w_ffn_down, loss_target, m_meta_tokens, m_norm_mix_pre, m_norm_mix_post, m_norm_ffn_pre, m_norm_ffn_post, m_w_in, m_b_forget, m_w_pool, m_pool_scale, m_w_out, m_w_ffn_gate, m_w_ffn_up, m_w_ffn_down, v_meta_tokens, v_norm_mix_pre, v_norm_mix_post, v_norm_ffn_pre, v_norm_ffn_post, v_w_in, v_b_forget, v_w_pool, v_pool_scale, v_w_out, v_w_ffn_gate, v_w_ffn_up, v_w_ffn_down):
    x2 = x[0]
    target = loss_target[0]
    seq, d = x2.shape
    depth = w_in.shape[0]
    heads = d // HEAD_DIM
    ff = w_ffn_gate.shape[2] * N_DEV
    rows = PAD_ROWS + META_TOKENS + seq
    assert seq % CHUNK == 0 and d % (2 * CHUNK) == 0 and heads <= FORGET_PAD and depth >= 2
    me = 4 * lax.axis_index("x") + 2 * lax.axis_index("y") + lax.axis_index("c")

    big = dict(w_in=w_in, w_pool=w_pool, w_out=w_out, w_gate=w_ffn_gate, w_up=w_ffn_up, w_down=w_ffn_down)
    big_m = dict(w_in=m_w_in, w_pool=m_w_pool, w_out=m_w_out, w_gate=m_w_ffn_gate, w_up=m_w_ffn_up, w_down=m_w_ffn_down)
    big_v = dict(w_in=v_w_in, w_pool=v_w_pool, w_out=v_w_out, w_gate=v_w_ffn_gate, w_up=v_w_ffn_up, w_down=v_w_ffn_down)
    small = dict(norm_mix_pre=norm_mix_pre, norm_mix_post=norm_mix_post, norm_ffn_pre=norm_ffn_pre,
                 norm_ffn_post=norm_ffn_post, pool_scale=pool_scale)
    small_m = dict(norm_mix_pre=m_norm_mix_pre, norm_mix_post=m_norm_mix_post, norm_ffn_pre=m_norm_ffn_pre,
                   norm_ffn_post=m_norm_ffn_post, pool_scale=m_pool_scale)
    small_v = dict(norm_mix_pre=v_norm_mix_pre, norm_mix_post=v_norm_mix_post, norm_ffn_pre=v_norm_ffn_pre,
                   norm_ffn_post=v_norm_ffn_post, pool_scale=v_pool_scale)

    wire = {n: big[n].astype(BF16) for n in BIG}
    others = [n for n in BIG if n != "w_in"]
    gathered = _exchange([wire["w_in"][:1], meta_tokens], True, "gather_first")
    meta_full = _join(gathered[1], -1)
    later_srcs = [wire["w_in"][1:]] + [wire[n] for n in others]
    later, started = _exchange_start(later_srcs, True, "gather_later_start")

    def in_weights(lands):
        win = _join(lands, -1)
        fcol = 4 * d
        qkv_w = win[:, :, d:4 * d]
        rest_w = jnp.concatenate([win[:, :, :d], win[:, :, fcol + heads:], win[:, :, fcol:fcol + heads],
                                  jnp.zeros(win.shape[:2] + (FORGET_PAD - heads,), BF16)], axis=2)
        return qkv_w, rest_w, jnp.concatenate([qkv_w, rest_w], axis=2)

    first_in = in_weights(gathered[0])
    w_qkv, w_rest, w_cat, w_gu, w_o, w_dn, w_pl = (
        [None] * depth for _ in range(7))

    h_res = jnp.concatenate([jnp.zeros((PAD_ROWS, d), F32), meta_full, x2], axis=0)
    h1 = _norm_fwd(h_res, norm_mix_pre[0] + started[0, 0])
    ones = jnp.ones((d,), F32)
    saved = []
    for l in range(depth):
        w_qkv[l], w_rest[l], w_cat[l] = (t[0] for t in first_in) if l == 0 else (t[l - 1] for t in later_in)
        qkv = _matmul(h1, w_qkv[l], "nn", BF16, "proj_qkv")
        rest = _matmul(h1, w_rest[l], "nn", F32, "proj_rest")
        z = rest[:, 3 * d:3 * d + heads].T
        bias = jnp.broadcast_to(b_forget[l][:, None], (heads, CHUNK))
        q_aug, k_aug, v_own, v_aug = _attn_prep(qkv, _logf_fwd(z, bias).T, d)
        o, lse = _attn_fwd(q_aug, k_aug, v_aug, d)
        if l == 0:
            lands = _exchange_wait(later, o, True, "gather_later_wait")
            lands = [_fill_own(g, src, me) for g, src in zip(lands, later_srcs)]
            later_in = in_weights(lands[0])
            full = {n: _join(g, SHARD_AXIS[n]) for n, g in zip(others, lands[1:])}
            gate_up = jnp.concatenate([full["w_gate"], full["w_up"]], axis=2)
        w_gu[l], w_o[l], w_dn[l], w_pl[l] = gate_up[l], full["w_out"][l], full["w_down"][l], full["w_pool"][l]
        merged, ypre = _pool_merge_fwd(rest, o, w_pl[l], pool_scale[l], d)
        mix = _matmul(merged, w_o[l], "nn", F32, "mix_out")
        h_mid, h2 = _resid_norm_fwd(h_res, mix, norm_mix_post[l], norm_ffn_pre[l])
        ab = _matmul(h2, w_gu[l], "nn", F32, "ffn_in")
        act = _swiglu_fwd(ab)
        ffo = _matmul(act, w_dn[l], "nn", F32, "ffn_out")
        g_next = norm_mix_pre[l + 1] if l + 1 < depth else ones
        h_next, h1_next = _resid_norm_fwd(h_mid, ffo, norm_ffn_post[l], g_next)
        saved.append(dict(h_in=h_res, h1=h1, q_aug=q_aug, k_aug=k_aug, v_own=v_own, rest=rest, z=z,
                          bias=bias, o=o, lse=lse,
                          merged=merged, ypre=ypre, mix=mix, h_mid=h_mid, h2=h2, ab=ab, act=act, ffo=ffo))
        h_res, h1 = h_next, h1_next

    dh, loss_local = _loss_grad(h_res, target)
    loss = lax.psum(loss_local, ("x", "y", "c"))

    grads = {n: [None] * depth for n in BIG + SMALL + ("b_forget",)}

    def grad_slots(n, first, last):
        part = _split(jnp.stack(grads[n][first:last]), SHARD_AXIS[n]).astype(BF16)
        return part.reshape(N_DEV, -1, part.shape[-1])

    last_made = ("w_in", "w_pool")

    for l in reversed(range(depth)):
        s = saved[l]
        dffo, grads["norm_ffn_post"][l] = _norm_bwd(s["ffo"], norm_ffn_post[l], dh, None, BF16)
        grads["w_down"][l] = _matmul(s["act"], dffo, "tn", F32, "grad_w_down")
        dact = _matmul(dffo, w_dn[l], "nt", F32, "ffn_out_dx")
        dab = _swiglu_bwd(s["ab"], dact)
        dgu = _matmul(s["h2"], dab, "tn", F32, "grad_w_gu")
        grads["w_gate"][l], grads["w_up"][l] = dgu[:, :ff], dgu[:, ff:]
        dh2 = _matmul(dab, w_gu[l], "nt", F32, "ffn_in_dx")
        dh_mid, grads["norm_ffn_pre"][l] = _norm_bwd(s["h_mid"], norm_ffn_pre[l], dh2, dh, F32)

        dmix, grads["norm_mix_post"][l] = _norm_bwd(s["mix"], norm_mix_post[l], dh_mid, None, BF16)
        grads["w_out"][l] = _matmul(s["merged"], dmix, "tn", F32, "grad_w_out")
        dm = _matmul(dmix, w_o[l], "nt", F32, "mix_out_dx")
        scale_l = pool_scale[l]
        if l == 0:
            early_slots = [grad_slots(n, 1 if n in last_made else 0, depth) for n in BIG]
            early, started = _exchange_start(early_slots, False, "scatter_early_start")
            scale_l = scale_l + started[0, 0]
        dgp, dga, do, delta, dypre, grads["pool_scale"][l] = _gate_bwd(dm, s["rest"], s["o"], s["ypre"], scale_l, d)
        du, grads["w_pool"][l] = _pool_bwd(dypre, s["rest"], w_pl[l], d)
        dq, dk, dv, df_key, df_query = _attn_bwd(s["q_aug"], s["k_aug"], s["v_own"], do, s["lse"], delta, d)
        df = df_key.reshape(heads, rows) + df_query.reshape(rows, heads, HEAD_DIM)[:, :, 0].T
        dz, grads["b_forget"][l] = _logf_bwd(df, s["z"], s["bias"])
        dzt = jnp.pad(dz.T.astype(BF16), ((0, 0), (0, FORGET_PAD - heads)))
        dproj = jnp.concatenate([dq, dk, dv, du, dgp, dga, dzt], axis=1)
        dwc = _matmul(s["h1"], dproj, "tn", F32, "grad_w_in")
        grads["w_in"][l] = jnp.concatenate([dwc[:, 3 * d:4 * d], dwc[:, :3 * d], dwc[:, 6 * d:6 * d + heads],
                                            dwc[:, 4 * d:6 * d]], axis=1)
        dh1 = _matmul(dproj, w_cat[l], "nt", F32, "proj_dx")
        dh, grads["norm_mix_pre"][l] = _norm_bwd(s["h_in"], norm_mix_pre[l], dh1, dh_mid, F32)

    grad_x = dh[PAD_ROWS + META_TOKENS:][None]
    dmeta = dh[PAD_ROWS:PAD_ROWS + META_TOKENS]

    late_recv = dict(zip(last_made, _exchange([grad_slots(n, 0, 1) for n in last_made], False, "scatter_last")))
    early_recv = _exchange_wait(early, dh, False, "scatter_early_wait")
    early_recv = [_fill_own(r, lax.dynamic_index_in_dim(s_, me, 0, keepdims=False), me)
                  for r, s_ in zip(early_recv, early_slots)]
    recv = [jnp.concatenate([late_recv[n], r], axis=1) if n in last_made else r for n, r in zip(BIG, early_recv)]
    big_out = {}
    for n, r in zip(BIG, recv):
        outs = _adamw(r, _rows2d(big[n]), _rows2d(big_m[n]), _rows2d(big_v[n]), "adamw_" + n)
        big_out[n] = [o_.reshape(big[n].shape) for o_ in outs]

    def table(parts, forget):
        t = jnp.concatenate([_rows2d(p) for p in parts] + [jnp.pad(forget, ((0, 0), (0, d - heads)))], axis=0)
        return jnp.pad(t, ((0, -t.shape[0] % 8), (0, 0)))

    g_table = table([jnp.stack(grads[n]) for n in SMALL], jnp.stack(grads["b_forget"]))
    rep_rows = g_table.shape[0]
    got = _exchange([jnp.concatenate([g_table, dmeta], axis=0)], True, "gather_small_grads")[0]
    outs = _adamw(got[:, :rep_rows], table([small[n] for n in SMALL], b_forget),
                  table([small_m[n] for n in SMALL], m_b_forget), table([small_v[n] for n in SMALL], v_b_forget),
                  "adamw_small")
    dcols = d // N_DEV
    meta_slots = lax.dynamic_slice_in_dim(got[:, rep_rows:], me * dcols, dcols, axis=2)
    meta_out = _adamw(meta_slots, meta_tokens, m_meta_tokens, v_meta_tokens, "adamw_meta")

    def ordered(k):
        t = outs[k]
        so = {n: t[a * depth:(a + 1) * depth] for a, n in enumerate(SMALL)}
        forget = t[len(SMALL) * depth:(len(SMALL) + 1) * depth, :heads]
        return (meta_out[k], so["norm_mix_pre"], so["norm_mix_post"], so["norm_ffn_pre"], so["norm_ffn_post"],
                big_out["w_in"][k], forget, big_out["w_pool"][k], so["pool_scale"], big_out["w_out"][k],
                big_out["w_gate"][k], big_out["w_up"][k], big_out["w_down"][k])

    return (loss, grad_x) + ordered(0) + ordered(1) + ordered(2) + ordered(3)
```

```python
import math

import jax
import jax.numpy as jnp
from jax import lax
from jax.experimental import pallas as pl
from jax.experimental.pallas import tpu as pltpu

F32 = jnp.float32
BF16 = jnp.bfloat16

N_DEV = 8
META_TOKENS = 16
PAD_ROWS = 112
CHUNK = 128
HEAD_DIM = 64
POOL_WINDOWS = (2, 4, 8, 16)
FORGET_PAD = 256
RMS_EPS = 1e-6
NEG_INF = -1e30
ADAM_LR, ADAM_B1, ADAM_B2, ADAM_EPS, ADAM_WD, ADAM_STEP = 0.001, 0.9, 0.999, 1e-08, 0.01, 10

VMEM_LIMIT = 56 * 1024 * 1024
VMEM_TILE_BUDGET = 36 * 1024 * 1024
ADAM_TILE_BYTES = 8 * 1024 * 1024
MESH_ID = pl.DeviceIdType.MESH


def _params(sem, vmem=VMEM_LIMIT):
    return pltpu.CompilerParams(dimension_semantics=sem, vmem_limit_bytes=vmem)


def _divisors(n, mult):
    return [d for d in range(mult, n + 1, mult) if n % d == 0]


def _row_tile(rows, cap):
    return max(d for d in _divisors(rows, CHUNK) if d <= max(cap, CHUNK))


def _fold8(x):
    r, c = x.shape
    return x.reshape(r // 8, 8, c).sum(axis=0)


def _split_bf16(x, parts):
    out = []
    for _ in range(parts - 1):
        hi = x.astype(BF16)
        out.append(hi)
        x = x - hi.astype(F32)
    out.append(x.astype(BF16))
    return out


def _apply01(mat, x, parts, left=True):
    acc = None
    for p in _split_bf16(x, parts):
        t = jnp.dot(mat, p, preferred_element_type=F32) if left else jnp.dot(p, mat, preferred_element_type=F32)
        acc = t if acc is None else acc + t
    return acc


def _matmul_tiles(m, n, k, mode, out_bytes):
    if mode == "tn":
        tk = _row_tile(k, 640)
    else:
        tk = max(d for d in _divisors(k, CHUNK) if d <= 1536)
    nk = k // tk
    best = None
    m_opts = _divisors(m, CHUNK)
    n_opts = _divisors(n, CHUNK)
    for tm in m_opts:
        for tn in n_opts:
            need = 2 * 2 * (tm * tk + tk * tn) + 2 * tm * tn * out_bytes
            if nk > 1 or mode == "tn":
                need += tm * tn * 4
            need += tm * tn * 4
            if need > VMEM_TILE_BUDGET:
                continue
            key = (tm * tn, tn)
            if best is None or key > best[0]:
                best = (key, tm, tn)
    return best[1], best[2], tk


def _matmul(a, b, mode, out_dtype, name):
    if mode == "nn":
        (m, k), (k2, n) = a.shape, b.shape
    elif mode == "nt":
        (m, k), (n, k2) = a.shape, b.shape
    else:
        (k, m), (k2, n) = a.shape, b.shape
    assert k == k2 and a.dtype == BF16 and b.dtype == BF16
    tm, tn, tk = _matmul_tiles(m, n, k, mode, jnp.dtype(out_dtype).itemsize)
    nk = k // tk
    if mode == "nn":
        a_spec = pl.BlockSpec((tm, tk), lambda i, j, r: (i, r))
        b_spec = pl.BlockSpec((tk, tn), lambda i, j, r: (r, j))
        dims = (((1,), (0,)), ((), ()))
    elif mode == "nt":
        a_spec = pl.BlockSpec((tm, tk), lambda i, j, r: (i, r))
        b_spec = pl.BlockSpec((tn, tk), lambda i, j, r: (j, r))
        dims = (((1,), (1,)), ((), ()))
    else:
        a_spec = pl.BlockSpec((tk, tm), lambda i, j, r: (r, i))
        b_spec = pl.BlockSpec((tk, tn), lambda i, j, r: (r, j))
        dims = (((0,), (0,)), ((), ()))

    def body(a_ref, b_ref, o_ref, *acc):
        part = lax.dot_general(a_ref[...], b_ref[...], dims, preferred_element_type=F32)
        if nk == 1:
            o_ref[...] = part.astype(o_ref.dtype)
        else:
            r = pl.program_id(2)

            @pl.when(r == 0)
            def _():
                acc[0][...] = part

            @pl.when(r > 0)
            def _():
                acc[0][...] += part

            @pl.when(r == nk - 1)
            def _():
                o_ref[...] = acc[0][...].astype(o_ref.dtype)

    return pl.pallas_call(
        body, name=name,
        out_shape=jax.ShapeDtypeStruct((m, n), out_dtype),
        grid=(m // tm, n // tn, nk),
        in_specs=[a_spec, b_spec],
        out_specs=pl.BlockSpec((tm, tn), lambda i, j, r: (i, j)),
        scratch_shapes=[pltpu.VMEM((tm, tn), F32)] if nk > 1 else [],
        compiler_params=_params(("parallel", "parallel", "arbitrary")),
    )(a, b)


def _rms(x, g):
    rstd = lax.rsqrt(jnp.mean(x * x, axis=-1, keepdims=True) + RMS_EPS)
    return x * rstd * g


def _norm_fwd(x, g):
    rows, d = x.shape
    tm = _row_tile(rows, 640)

    def body(x_ref, g_ref, h_ref):
        h_ref[...] = _rms(x_ref[...], g_ref[...]).astype(BF16)

    return pl.pallas_call(
        body, name="norm_fwd",
        out_shape=jax.ShapeDtypeStruct((rows, d), BF16),
        grid=(rows // tm,),
        in_specs=[pl.BlockSpec((tm, d), lambda i: (i, 0)), pl.BlockSpec((1, d), lambda i: (0, 0))],
        out_specs=pl.BlockSpec((tm, d), lambda i: (i, 0)),
        compiler_params=_params(("parallel",)),
    )(x, g.reshape(1, d))


def _resid_norm_fwd(h_res, y, g_post, g_next):
    rows, d = h_res.shape
    tm = _row_tile(rows, 640)

    def body(r_ref, y_ref, gp_ref, gn_ref, hn_ref, hx_ref):
        h_new = r_ref[...] + _rms(y_ref[...], gp_ref[...])
        hn_ref[...] = h_new
        hx_ref[...] = _rms(h_new, gn_ref[...]).astype(BF16)

    row = pl.BlockSpec((tm, d), lambda i: (i, 0))
    vec = pl.BlockSpec((1, d), lambda i: (0, 0))
    return pl.pallas_call(
        body, name="resid_norm_fwd",
        out_shape=(jax.ShapeDtypeStruct((rows, d), F32), jax.ShapeDtypeStruct((rows, d), BF16)),
        grid=(rows // tm,),
        in_specs=[row, row, vec, vec],
        out_specs=(row, row),
        compiler_params=_params(("parallel",)),
    )(h_res, y, g_post.reshape(1, d), g_next.reshape(1, d))


def _norm_bwd(x, g, dy, resid, out_dtype):
    rows, d = x.shape
    tm = _row_tile(rows, 640)
    has_resid = resid is not None

    def body(*refs):
        if has_resid:
            x_ref, g_ref, dy_ref, r_ref, dx_ref, dg_ref = refs
        else:
            x_ref, g_ref, dy_ref, dx_ref, dg_ref = refs
        xv = x_ref[...]
        dyv = dy_ref[...].astype(F32)
        rstd = lax.rsqrt(jnp.mean(xv * xv, axis=-1, keepdims=True) + RMS_EPS)
        xhat = xv * rstd
        gdy = dyv * g_ref[...]
        dx = rstd * (gdy - xhat * jnp.mean(gdy * xhat, axis=-1, keepdims=True))
        if has_resid:
            dx = dx + r_ref[...]
        dx_ref[...] = dx.astype(dx_ref.dtype)

        @pl.when(pl.program_id(0) == 0)
        def _():
            dg_ref[...] = jnp.zeros_like(dg_ref)

        dg_ref[...] += _fold8(dyv * xhat)

    row = pl.BlockSpec((tm, d), lambda i: (i, 0))
    vec = pl.BlockSpec((1, d), lambda i: (0, 0))
    args = [x, g.reshape(1, d), dy] + ([resid] if has_resid else [])
    dx, dg = pl.pallas_call(
        body, name="norm_bwd_resid" if has_resid else "norm_bwd",
        out_shape=(jax.ShapeDtypeStruct((rows, d), out_dtype), jax.ShapeDtypeStruct((8, d), F32)),
        grid=(rows // tm,),
        in_specs=[row, vec, row] + ([row] if has_resid else []),
        out_specs=(row, pl.BlockSpec((8, d), lambda i: (0, 0))),
        compiler_params=_params(("arbitrary",)),
    )(*args)
    return dx, dg.sum(axis=0)


def _swiglu_fwd(ab):
    rows, two_f = ab.shape
    f = two_f // 2
    tm = _row_tile(rows, 640)

    def body(a_ref, b_ref, o_ref):
        a = a_ref[...]
        o_ref[...] = (a * jax.nn.sigmoid(a) * b_ref[...]).astype(BF16)

    return pl.pallas_call(
        body, name="swiglu_fwd",
        out_shape=jax.ShapeDtypeStruct((rows, f), BF16),
        grid=(rows // tm,),
        in_specs=[pl.BlockSpec((tm, f), lambda i: (i, 0)), pl.BlockSpec((tm, f), lambda i: (i, 1))],
        out_specs=pl.BlockSpec((tm, f), lambda i: (i, 0)),
        compiler_params=_params(("parallel",)),
    )(ab, ab)


def _swiglu_bwd(ab, dff):
    rows, two_f = ab.shape
    f = two_f // 2
    tm = _row_tile(rows, 256)

    def body(a_ref, b_ref, d_ref, dab_ref):
        a = a_ref[...]
        d = d_ref[...]
        s = jax.nn.sigmoid(a)
        dab_ref[:, :f] = (d * b_ref[...] * (s * (1.0 + a * (1.0 - s)))).astype(BF16)
        dab_ref[:, f:] = (d * (a * s)).astype(BF16)

    lo = pl.BlockSpec((tm, f), lambda i: (i, 0))
    hi = pl.BlockSpec((tm, f), lambda i: (i, 1))
    return pl.pallas_call(
        body, name="swiglu_bwd",
        out_shape=jax.ShapeDtypeStruct((rows, two_f), BF16),
        grid=(rows // tm,),
        in_specs=[lo, hi, lo],
        out_specs=pl.BlockSpec((tm, two_f), lambda i: (i, 0)),
        compiler_params=_params(("parallel",)),
    )(ab, ab, dff)


def _tri(lower):
    r = lax.broadcasted_iota(jnp.int32, (CHUNK, CHUNK), 0)
    c = lax.broadcasted_iota(jnp.int32, (CHUNK, CHUNK), 1)
    return jnp.where((r >= c) if lower else (r <= c), 1.0, 0.0).astype(BF16)


def _logf_group(rows):
    n = rows // CHUNK
    return max(g for g in range(1, 9) if n % g == 0)


def _logf_fwd(z, b):
    h, rows = z.shape
    g = _logf_group(rows)
    n = rows // (g * CHUNK)

    def body(z_ref, b_ref, f_ref, carry):
        i = pl.program_id(0)

        @pl.when(i == 0)
        def _():
            carry[...] = jnp.zeros_like(carry)

        c = carry[...]
        for t in range(g):
            sl = slice(t * CHUNK, (t + 1) * CHUNK)
            x = z_ref[:, sl] + b_ref[...]
            lf = jnp.minimum(x, 0.0) - jnp.log(1.0 + jnp.exp(-jnp.abs(x)))
            col = (i * g + t) * CHUNK + lax.broadcasted_iota(jnp.int32, (1, CHUNK), 1)
            lf = jnp.where(col >= PAD_ROWS, lf, 0.0)
            run = _apply01(_tri(False), lf, 3, left=False) + c
            f_ref[:, sl] = run
            c = jnp.broadcast_to(run[:, CHUNK - 1:CHUNK], c.shape)
        carry[...] = c

    blk = pl.BlockSpec((h, g * CHUNK), lambda i: (0, i))
    return pl.pallas_call(
        body, name="logf_fwd",
        out_shape=jax.ShapeDtypeStruct((h, rows), F32),
        grid=(n,),
        in_specs=[blk, pl.BlockSpec((h, CHUNK), lambda i: (0, 0))],
        out_specs=blk,
        scratch_shapes=[pltpu.VMEM((h, CHUNK), F32)],
        compiler_params=_params(("arbitrary",)),
    )(z, b)


def _logf_bwd(df, z, b):
    h, rows = z.shape
    g = _logf_group(rows)
    n = rows // (g * CHUNK)

    def body(df_ref, z_ref, b_ref, dz_ref, db_ref, carry):
        i = pl.program_id(0)

        @pl.when(i == 0)
        def _():
            carry[...] = jnp.zeros_like(carry)
            db_ref[...] = jnp.zeros_like(db_ref)

        c = carry[...]
        db = db_ref[...]
        for t in reversed(range(g)):
            sl = slice(t * CHUNK, (t + 1) * CHUNK)
            run = _apply01(_tri(True), df_ref[:, sl], 3, left=False) + c
            c = jnp.broadcast_to(run[:, 0:1], c.shape)
            x = z_ref[:, sl] + b_ref[...]
            col = ((n - 1 - i) * g + t) * CHUNK + lax.broadcasted_iota(jnp.int32, (1, CHUNK), 1)
            dz = jnp.where(col >= PAD_ROWS, run * (1.0 - jax.nn.sigmoid(x)), 0.0)
            dz_ref[:, sl] = dz
            db = db + dz
        carry[...] = c
        db_ref[...] = db

    rev = pl.BlockSpec((h, g * CHUNK), lambda i: (0, n - 1 - i))
    fix = pl.BlockSpec((h, CHUNK), lambda i: (0, 0))
    dz, db = pl.pallas_call(
        body, name="logf_bwd",
        out_shape=(jax.ShapeDtypeStruct((h, rows), F32), jax.ShapeDtypeStruct((h, CHUNK), F32)),
        grid=(n,),
        in_specs=[rev, rev, fix],
        out_specs=(rev, fix),
        scratch_shapes=[pltpu.VMEM((h, CHUNK), F32)],
        compiler_params=_params(("arbitrary",)),
    )(df, z, b)
    return dz, db.sum(axis=1)


def _lane_lo():
    return lax.broadcasted_iota(jnp.int32, (1, CHUNK), 1) < HEAD_DIM


def _attn_block(rows):
    return _row_tile(rows, min(640, rows // 2))


def _masked_logits(s, i, j, blk):
    row = i * blk + lax.broadcasted_iota(jnp.int32, (blk, 1), 0)
    col = j * blk + lax.broadcasted_iota(jnp.int32, (1, blk), 1)
    return jnp.where(col <= row, s, NEG_INF)


def _attn_prep(qkv, f_t, d):
    rows = qkv.shape[0]
    heads = d // HEAD_DIM
    hp = heads // 2
    tm = _row_tile(rows, 640)
    scale = 1.0 / math.sqrt(HEAD_DIM)

    def body(q_ref, k_ref, v_ref, f_ref, qa_ref, ka_ref, vo_ref, va_ref):
        pr = pl.program_id(1)
        lane = lax.broadcasted_iota(jnp.int32, (1, CHUNK), 1)
        lo = lane < HEAD_DIM
        q2 = (q_ref[...].astype(F32) * scale).astype(BF16)
        k2 = k_ref[...]
        v2 = v_ref[...]
        zero = jnp.zeros_like(k2)
        head_id = lax.broadcasted_iota(jnp.int32, (1, heads), 1)
        ft = f_ref[...]
        is_pad = pl.program_id(0) * tm + lax.broadcasted_iota(jnp.int32, (tm, 1), 0) < PAD_ROWS
        for hh in range(2):
            base = HEAD_DIM if hh == 0 else 0
            neg_f = -jnp.sum(jnp.where(head_id == 2 * pr + hh, ft, 0.0), axis=1, keepdims=True)
            neg_f = jnp.where(is_pad, NEG_INF, neg_f)
            aug = zero
            ones = zero
            for t, part in enumerate(_split_bf16(neg_f, 3)):
                aug = jnp.where(lane == base + t, part, aug)
                ones = jnp.where(lane == base + t, jnp.ones_like(zero), ones)
            own = lo if hh == 0 else jnp.logical_not(lo)
            sl = slice(hh * CHUNK, (hh + 1) * CHUNK)
            qa_ref[:, sl] = jnp.where(own, q2, ones)
            ka_ref[:, sl] = jnp.where(own, k2, aug)
            vo_ref[:, sl] = jnp.where(own, v2, zero)
            va_ref[:, sl] = jnp.where(own, v2, jnp.where(lane == base, jnp.ones_like(zero), zero))

    pair_in = lambda c: pl.BlockSpec((tm, CHUNK), lambda i, p: (i, c * hp + p))
    pair_out = pl.BlockSpec((tm, 2 * CHUNK), lambda i, p: (i, p))
    sds = jax.ShapeDtypeStruct((rows, 2 * d), BF16)
    return pl.pallas_call(
        body, name="attn_prep",
        out_shape=(sds, sds, sds, sds),
        grid=(rows // tm, hp),
        in_specs=[pair_in(0), pair_in(1), pair_in(2), pl.BlockSpec((tm, heads), lambda i, p: (i, 0))],
        out_specs=(pair_out, pair_out, pair_out, pair_out),
        compiler_params=_params(("parallel", "parallel")),
    )(qkv, qkv, qkv, f_t)


FWD_PAIRS = 2


def _attn_fwd(q_aug, k_aug, v_aug, d):
    rows = q_aug.shape[0]
    hp = d // CHUNK
    blk = _attn_block(rows)
    nb = rows // blk
    nt = (((1,), (1,)), ((), ()))
    den_lane = (HEAD_DIM, 0)
    gp = FWD_PAIRS if hp % FWD_PAIRS == 0 else 1
    nh = 2 * gp

    def body(qi_ref, kj_ref, q_ref, k_ref, v_ref, o_ref, lse_ref, m_s, acc_s):
        i = qi_ref[pl.program_id(1)]
        j = kj_ref[pl.program_id(1)]

        @pl.when(j == 0)
        def _():
            m_s[...] = jnp.full(m_s.shape, NEG_INF, F32)
            acc_s[...] = jnp.zeros_like(acc_s)

        def step(masked):
            tiles = [slice(hh * CHUNK, (hh + 1) * CHUNK) for hh in range(nh)]
            scores = [lax.dot_general(q_ref[:, ln], k_ref[:, ln], nt, preferred_element_type=F32) for ln in tiles]
            for hh in range(nh):
                s = scores[hh]
                if masked:
                    s = _masked_logits(s, i, j, blk)
                m_prev = m_s[hh]
                m_new = jnp.maximum(m_prev, s.max(axis=-1, keepdims=True))
                p = jnp.exp(s - m_new)
                m_s[hh] = m_new
                pv = jnp.dot(p.astype(BF16), v_ref[:, tiles[hh]], preferred_element_type=F32)
                acc_s[hh] = acc_s[hh] * jnp.exp(m_prev - m_new) + pv

        edge = j == i

        @pl.when(edge)
        def _():
            step(True)

        @pl.when(jnp.logical_not(edge))
        def _():
            step(False)

        @pl.when(j == i)
        def _():
            row = i * blk + lax.broadcasted_iota(jnp.int32, (blk, 1), 0)
            lo = _lane_lo()
            for pr in range(gp):
                acc = [acc_s[2 * pr + hh] for hh in range(2)]
                den = [acc[hh][:, den_lane[hh]:den_lane[hh] + 1] for hh in range(2)]
                o = jnp.where(lo, acc[0] * (1.0 / den[0]), acc[1] * (1.0 / den[1]))
                sl = slice(pr * CHUNK, (pr + 1) * CHUNK)
                o_ref[:, sl] = jnp.where(row >= PAD_ROWS, o, 0.0)
                lse_ref[:, sl] = jnp.where(lo, m_s[2 * pr] + jnp.log(den[0]), m_s[2 * pr + 1] + jnp.log(den[1]))

    pairs = [(i, j) for i in range(nb) for j in range(i + 1)]
    q_spec = pl.BlockSpec((blk, nh * CHUNK), lambda h, t, qi, kj: (qi[t], h))
    k_spec = pl.BlockSpec((blk, nh * CHUNK), lambda h, t, qi, kj: (kj[t], h))
    o_spec = pl.BlockSpec((blk, gp * CHUNK), lambda h, t, qi, kj: (qi[t], h))
    return pl.pallas_call(
        body, name="attn_fwd",
        out_shape=(jax.ShapeDtypeStruct((rows, d), F32), jax.ShapeDtypeStruct((rows, d), F32)),
        grid_spec=pltpu.PrefetchScalarGridSpec(
            num_scalar_prefetch=2, grid=(hp // gp, len(pairs)),
            in_specs=[q_spec, k_spec, k_spec], out_specs=(o_spec, o_spec),
            scratch_shapes=[pltpu.VMEM((nh, blk, 1), F32), pltpu.VMEM((nh, blk, CHUNK), F32)]),
        compiler_params=_params(("parallel", "arbitrary")),
    )(jnp.array([p[0] for p in pairs], jnp.int32), jnp.array([p[1] for p in pairs], jnp.int32),
      q_aug, k_aug, v_aug)


def _attn_bwd(q_aug, k_aug, v_own, do, lse, delta, d):
    rows = q_aug.shape[0]
    hp = d // CHUNK
    blk = _attn_block(rows)
    nb = rows // blk
    scale = 1.0 / math.sqrt(HEAD_DIM)
    nt = (((1,), (1,)), ((), ()))
    tn = (((0,), (0,)), ((), ()))

    def body(kj_ref, qi_ref, q_ref, k_ref, v_ref, do_ref, lse_ref, dl_ref, dq_ref, dk_ref, dv_ref, df_ref, rs_ref,
             dq_s, dk_s, dv_s, df_s, rs_s):
        j = kj_ref[pl.program_id(1)]
        i = qi_ref[pl.program_id(1)]
        lo = _lane_lo()

        @pl.when((j == 0) & (i == 0))
        def _():
            dq_s[...] = jnp.zeros_like(dq_s)
            rs_s[...] = jnp.zeros_like(rs_s)

        @pl.when(i == j)
        def _():
            dk_s[...] = jnp.zeros_like(dk_s)
            dv_s[...] = jnp.zeros_like(dv_s)
            df_s[...] = jnp.zeros_like(df_s)

        def step(masked):
            dov = do_ref[...]
            t_dq, t_dk, t_dv, row_sums = [], [], [], []
            for hh in range(2):
                lanes = slice(hh * CHUNK, (hh + 1) * CHUNK)
                qh, kh = q_ref[:, lanes], k_ref[:, lanes]
                off = hh * HEAD_DIM
                s = lax.dot_general(qh, kh, nt, preferred_element_type=F32)
                if masked:
                    s = _masked_logits(s, i, j, blk)
                p = jnp.exp(s - lse_ref[:, off:off + 1])
                dp = lax.dot_general(dov, v_ref[:, lanes], nt, preferred_element_type=F32)
                ds = p * (dp - dl_ref[:, off:off + 1])
                df_s[hh:hh + 1, :] += ds.sum(axis=0, keepdims=True)
                row_sums.append(ds.sum(axis=1, keepdims=True))
                pb = p.astype(BF16)
                dsb = ds.astype(BF16)
                t_dv.append(lax.dot_general(pb, dov, tn, preferred_element_type=F32))
                t_dk.append(lax.dot_general(dsb, qh, tn, preferred_element_type=F32))
                t_dq.append(jnp.dot(dsb, kh, preferred_element_type=F32))
            dv_s[...] += jnp.where(lo, t_dv[0], t_dv[1])
            dk_s[...] += jnp.where(lo, t_dk[0], t_dk[1])
            r0 = pl.multiple_of(i * blk, blk)
            dq_s[pl.ds(r0, blk), :] += jnp.where(lo, t_dq[0], t_dq[1])
            rs_s[pl.ds(r0, blk), :] += jnp.where(lo, row_sums[0], row_sums[1])

        edge = j == i

        @pl.when(edge)
        def _():
            step(True)

        @pl.when(jnp.logical_not(edge))
        def _():
            step(False)

        @pl.when(i == nb - 1)
        def _():
            dk_ref[...] = dk_s[...].astype(BF16)
            dv_ref[...] = dv_s[...].astype(BF16)
            df_ref[...] = -df_s[...]

        @pl.when((i == nb - 1) & (j == nb - 1))
        def _():
            dq_ref[...] = (dq_s[...] * scale).astype(BF16)
            rs_ref[...] = rs_s[...]

    pairs = [(j, i) for j in range(nb) for i in range(j, nb)]
    q_spec = pl.BlockSpec((blk, 2 * CHUNK), lambda h, t, kj, qi: (qi[t], h))
    kv_spec = pl.BlockSpec((blk, 2 * CHUNK), lambda h, t, kj, qi: (kj[t], h))
    row_spec = pl.BlockSpec((blk, CHUNK), lambda h, t, kj, qi: (qi[t], h))
    kv_out = pl.BlockSpec((blk, CHUNK), lambda h, t, kj, qi: (kj[t], h))
    dq_out = pl.BlockSpec((rows, CHUNK), lambda h, t, kj, qi: (0, h))
    return pl.pallas_call(
        body, name="attn_bwd",
        out_shape=(jax.ShapeDtypeStruct((rows, d), BF16), jax.ShapeDtypeStruct((rows, d), BF16),
                   jax.ShapeDtypeStruct((rows, d), BF16), jax.ShapeDtypeStruct((hp, 2, rows), F32),
                   jax.ShapeDtypeStruct((rows, d), F32)),
        grid_spec=pltpu.PrefetchScalarGridSpec(
            num_scalar_prefetch=2, grid=(hp, len(pairs)),
            in_specs=[q_spec, kv_spec, kv_spec, row_spec, row_spec, row_spec],
            out_specs=(dq_out, kv_out, kv_out,
                       pl.BlockSpec((None, 2, blk), lambda h, t, kj, qi: (h, 0, kj[t])), dq_out),
            scratch_shapes=[pltpu.VMEM((rows, CHUNK), F32), pltpu.VMEM((blk, CHUNK), F32),
                            pltpu.VMEM((blk, CHUNK), F32), pltpu.VMEM((2, blk), F32),
                            pltpu.VMEM((rows, CHUNK), F32)]),
        compiler_params=_params(("parallel", "arbitrary")),
    )(jnp.array([p[0] for p in pairs], jnp.int32), jnp.array([p[1] for p in pairs], jnp.int32),
      q_aug, k_aug, v_own, do, lse, delta)


def _band(w, transposed, other):
    r = lax.broadcasted_iota(jnp.int32, (CHUNK, CHUNK), 0)
    c = lax.broadcasted_iota(jnp.int32, (CHUNK, CHUNK), 1)
    dist = (c - r) if transposed else (r - c)
    if other:
        dist = dist + CHUNK
    return jnp.where(dist >= 0, jnp.where(dist < w, 1.0, 0.0), 0.0).astype(BF16)


def _inv_count(chunk_index, w):
    row = chunk_index * CHUNK + lax.broadcasted_iota(jnp.int32, (CHUNK, 1), 0)
    cnt = jnp.clip(row - PAD_ROWS + 1, 1, w).astype(F32)
    return 1.0 / cnt


def _pool_diff(u_cur, u_prev, i, w):
    ws = _apply01(_band(w, False, False), u_cur, 3)
    ws = ws + jnp.where(i > 0, _apply01(_band(w, False, True), u_prev, 3), 0.0)
    return ws * _inv_count(i, w) - u_cur


def _pool_merge_fwd(rest, o, w_pool, scale, d):
    rows = rest.shape[0]
    n = rows // CHUNK
    cg = d // len(POOL_WINDOWS)

    def body(up_ref, uc_ref, gp_ref, ga_ref, o_ref, wp_ref, sc_ref, mg_ref, yp_ref):
        i = pl.program_id(0)
        for g, w in enumerate(POOL_WINDOWS):
            sl = slice(g * cg, (g + 1) * cg)
            diff = _pool_diff(uc_ref[:, sl], up_ref[:, sl], i, w)
            ypre = jnp.dot(diff.astype(BF16), wp_ref[g], preferred_element_type=F32)
            yp_ref[:, sl] = ypre
            merged = (jax.nn.sigmoid(gp_ref[:, sl]) * (ypre * sc_ref[:, sl])
                      + jax.nn.sigmoid(ga_ref[:, sl]) * o_ref[:, sl])
            mg_ref[:, sl] = merged.astype(BF16)

    col = lambda c: pl.BlockSpec((CHUNK, d), lambda i: (i, c))
    return pl.pallas_call(
        body, name="pool_merge_fwd",
        out_shape=(jax.ShapeDtypeStruct((rows, d), BF16), jax.ShapeDtypeStruct((rows, d), F32)),
        grid=(n,),
        in_specs=[pl.BlockSpec((CHUNK, d), lambda i: (jnp.maximum(i - 1, 0), 0)), col(0), col(1), col(2), col(0),
                  pl.BlockSpec((len(POOL_WINDOWS), cg, cg), lambda i: (0, 0, 0)),
                  pl.BlockSpec((1, d), lambda i: (0, 0))],
        out_specs=(col(0), col(0)),
        compiler_params=_params(("parallel",)),
    )(rest, rest, rest, rest, o, w_pool, scale.reshape(1, d))


def _gate_bwd(dm, rest, o, ypre, scale, d):
    rows = rest.shape[0]
    n = rows // CHUNK

    def body(dm_ref, gp_ref, ga_ref, o_ref, yp_ref, sc_ref, dgp_ref, dga_ref, do_ref, dl_ref, dy_ref, ds_ref):
        @pl.when(pl.program_id(0) == 0)
        def _():
            ds_ref[...] = jnp.zeros_like(ds_ref)

        dmv = dm_ref[...]
        sp = jax.nn.sigmoid(gp_ref[...])
        sa = jax.nn.sigmoid(ga_ref[...])
        ov = o_ref[...]
        ypre_v = yp_ref[...]
        sc = sc_ref[...]
        dgp_ref[...] = (dmv * (ypre_v * sc) * (sp * (1.0 - sp))).astype(BF16)
        dga_ref[...] = (dmv * ov * (sa * (1.0 - sa))).astype(BF16)
        t = dmv * sp
        dy_ref[...] = (t * sc).astype(BF16)
        ds_ref[...] += _fold8(t * ypre_v)
        dob = (dmv * sa).astype(BF16)
        do_ref[...] = dob
        prod = dob.astype(F32) * ov
        lo = _lane_lo()
        for pr in range(d // CHUNK):
            sl = slice(pr * CHUNK, (pr + 1) * CHUNK)
            tp = prod[:, sl]
            s_lo = jnp.where(lo, tp, 0.0).sum(axis=-1, keepdims=True)
            s_hi = jnp.where(lo, 0.0, tp).sum(axis=-1, keepdims=True)
            dl_ref[:, sl] = jnp.where(lo, s_lo, s_hi)

    col = lambda c: pl.BlockSpec((CHUNK, d), lambda i: (i, c))
    row_bf = jax.ShapeDtypeStruct((rows, d), BF16)
    outs = pl.pallas_call(
        body, name="gate_bwd",
        out_shape=(row_bf, row_bf, row_bf, jax.ShapeDtypeStruct((rows, d), F32), row_bf,
                   jax.ShapeDtypeStruct((8, d), F32)),
        grid=(n,),
        in_specs=[col(0), col(1), col(2), col(0), col(0), pl.BlockSpec((1, d), lambda i: (0, 0))],
        out_specs=(col(0), col(0), col(0), col(0), col(0), pl.BlockSpec((8, d), lambda i: (0, 0))),
        compiler_params=_params(("arbitrary",)),
    )(dm, rest, rest, o, ypre, scale.reshape(1, d))
    return outs[:5] + (outs[5].sum(axis=0),)


def _pool_bwd(dypre, rest, w_pool, d):
    rows = rest.shape[0]
    n = rows // CHUNK
    ng = len(POOL_WINDOWS)
    cg = d // ng
    nt = (((1,), (1,)), ((), ()))
    tn = (((0,), (0,)), ((), ()))

    def body(dc_ref, dn_ref, up_ref, uc_ref, wp_ref, du_ref, dw_ref):
        i = pl.program_id(0)

        @pl.when(i == 0)
        def _():
            dw_ref[...] = jnp.zeros_like(dw_ref)

        row = i * CHUNK + lax.broadcasted_iota(jnp.int32, (CHUNK, 1), 0)
        for g, w in enumerate(POOL_WINDOWS):
            sl = slice(g * cg, (g + 1) * cg)
            diff = _pool_diff(uc_ref[:, sl], up_ref[:, sl], i, w)
            dyc = dc_ref[:, sl]
            dw_ref[g] += lax.dot_general(diff.astype(BF16), dyc, tn, preferred_element_type=F32)
            wg = wp_ref[g]
            dd_cur = lax.dot_general(dyc, wg, nt, preferred_element_type=F32)
            dd_next = lax.dot_general(dn_ref[:, sl], wg, nt, preferred_element_type=F32)
            du = _apply01(_band(w, True, False), dd_cur * _inv_count(i, w), 2)
            du = du + jnp.where(i < n - 1, _apply01(_band(w, True, True), dd_next * _inv_count(i + 1, w), 2), 0.0)
            du = du - dd_cur
            du_ref[:, sl] = jnp.where(row >= PAD_ROWS, du, 0.0).astype(BF16)

    cur = pl.BlockSpec((CHUNK, d), lambda i: (i, 0))
    return pl.pallas_call(
        body, name="pool_bwd",
        out_shape=(jax.ShapeDtypeStruct((rows, d), BF16), jax.ShapeDtypeStruct((ng, cg, cg), F32)),
        grid=(n,),
        in_specs=[cur, pl.BlockSpec((CHUNK, d), lambda i: (jnp.minimum(i + 1, n - 1), 0)),
                  pl.BlockSpec((CHUNK, d), lambda i: (jnp.maximum(i - 1, 0), 0)), cur,
                  pl.BlockSpec((ng, cg, cg), lambda i: (0, 0, 0))],
        out_specs=(cur, pl.BlockSpec((ng, cg, cg), lambda i: (0, 0, 0))),
        compiler_params=_params(("arbitrary",)),
    )(dypre, dypre, rest, rest, w_pool)


def _loss_grad(h_res, target):
    rows, d = h_res.shape
    n = rows // CHUNK

    def body(h_ref, t_ref, dh_ref, acc_ref):
        i = pl.program_id(0)

        @pl.when(i == 0)
        def _():
            acc_ref[...] = jnp.zeros_like(acc_ref)
            dh_ref[...] = jnp.zeros_like(dh_ref)

        @pl.when(i > 0)
        def _():
            err = h_ref[...] - t_ref[...]
            dh_ref[...] = err * (1.0 / d)
            e2 = _fold8(err * err)
            part = e2[:, 0:CHUNK]
            for c in range(1, d // CHUNK):
                part = part + e2[:, c * CHUNK:(c + 1) * CHUNK]
            acc_ref[...] += part

    dh, acc = pl.pallas_call(
        body, name="loss_grad",
        out_shape=(jax.ShapeDtypeStruct((rows, d), F32), jax.ShapeDtypeStruct((8, CHUNK), F32)),
        grid=(n,),
        in_specs=[pl.BlockSpec((CHUNK, d), lambda i: (i, 0)),
                  pl.BlockSpec((CHUNK, d), lambda i: (jnp.maximum(i - 1, 0), 0))],
        out_specs=(pl.BlockSpec((CHUNK, d), lambda i: (i, 0)), pl.BlockSpec((8, CHUNK), lambda i: (0, 0))),
        compiler_params=_params(("arbitrary",)),
    )(h_res, target)
    return dh, (0.5 / d) * acc.sum()


def _adamw(slots, w, m, v, name):
    rows, cols = w.shape
    lanes = -(-cols // CHUNK) * CHUNK
    row_bytes = lanes * (N_DEV * slots.dtype.itemsize + 7 * 4)
    tr = max(t for t in _divisors(rows, 8) if t <= max(8, ADAM_TILE_BYTES // row_bytes))
    c1 = 1.0 - ADAM_B1 ** ADAM_STEP
    c2 = 1.0 - ADAM_B2 ** ADAM_STEP

    def body(s_ref, w_ref, m_ref, v_ref, g_ref, d_ref, mo_ref, vo_ref):
        g = s_ref[0].astype(F32)
        for k in range(1, N_DEV):
            g = g + s_ref[k].astype(F32)
        m_new = ADAM_B1 * m_ref[...] + (1.0 - ADAM_B1) * g
        v_new = ADAM_B2 * v_ref[...] + (1.0 - ADAM_B2) * (g * g)
        m_hat = m_new / c1
        v_hat = v_new / c2
        g_ref[...] = g
        d_ref[...] = -ADAM_LR * (m_hat / (jnp.sqrt(v_hat) + ADAM_EPS) + ADAM_WD * w_ref[...])
        mo_ref[...] = m_new
        vo_ref[...] = v_new

    tile = pl.BlockSpec((tr, cols), lambda i: (i, 0))
    sds = jax.ShapeDtypeStruct((rows, cols), F32)
    return pl.pallas_call(
        body, name=name,
        out_shape=(sds, sds, sds, sds),
        grid=(rows // tr,),
        in_specs=[pl.BlockSpec((N_DEV, tr, cols), lambda i: (0, i, 0)), tile, tile, tile],
        out_specs=(tile, tile, tile, tile),
        compiler_params=_params(("parallel",)),
    )(slots, w, m, v)


def _exchange(srcs, gather, name):
    n = len(srcs)
    shapes = [((N_DEV,) + s.shape) if gather else s.shape for s in srcs]

    def body(*refs):
        src_refs, out_refs = refs[:n], refs[n:2 * n]
        send_sems, recv_sems, local_sems = refs[2 * n:]
        x, y, c = lax.axis_index("x"), lax.axis_index("y"), lax.axis_index("c")
        me = 4 * x + 2 * y + c

        def payload(a, slot):
            return src_refs[a] if gather else src_refs[a].at[slot]

        own = [pltpu.make_async_copy(payload(a, me), out_refs[a].at[me], local_sems.at[a]) for a in range(n)]
        for cp in own:
            cp.start()
        sends, recvs = [], []
        for k in range(1, N_DEV):
            px = 1 - x if k & 4 else x
            py = 1 - y if k & 2 else y
            pc = 1 - c if k & 1 else c
            peer = 4 * px + 2 * py + pc
            for a in range(n):
                sems = dict(send_sem=send_sems.at[(k - 1) * n + a], recv_sem=recv_sems.at[(k - 1) * n + a],
                            device_id=(px, py, pc), device_id_type=MESH_ID)
                sends.append(pltpu.make_async_remote_copy(src_ref=payload(a, peer), dst_ref=out_refs[a].at[me], **sems))
                recvs.append(pltpu.make_async_remote_copy(src_ref=payload(a, peer), dst_ref=out_refs[a].at[peer], **sems))
        for cp in sends:
            cp.start()
        for cp in recvs:
            cp.wait_recv()
        for cp in sends:
            cp.wait_send()
        for cp in own:
            cp.wait()

    return pl.pallas_call(
        body, name=name,
        out_shape=tuple(jax.ShapeDtypeStruct(sh, s.dtype) for sh, s in zip(shapes, srcs)),
        in_specs=[pl.BlockSpec(memory_space=pl.ANY)] * n,
        out_specs=tuple([pl.BlockSpec(memory_space=pl.ANY)] * n),
        scratch_shapes=[pltpu.SemaphoreType.DMA(((N_DEV - 1) * n,)), pltpu.SemaphoreType.DMA(((N_DEV - 1) * n,)),
                        pltpu.SemaphoreType.DMA((n,))],
    )(*srcs)


HBM_SPEC = pl.BlockSpec(memory_space=pltpu.HBM)
SEM_SPEC = pl.BlockSpec(memory_space=pltpu.SEMAPHORE)
DATAFLOW = pltpu.SideEffectType.DATAFLOW_SIDE_EFFECTING


def _peer_copies(src_refs, land_refs, send_sems, recv_sems, gather):
    n = len(src_refs)
    x, y, c = lax.axis_index("x"), lax.axis_index("y"), lax.axis_index("c")
    me = 4 * x + 2 * y + c
    sends, lands = [], []
    for k in range(1, N_DEV):
        px = 1 - x if k & 4 else x
        py = 1 - y if k & 2 else y
        pc = 1 - c if k & 1 else c
        peer = 4 * px + 2 * py + pc
        for a in range(n):
            src = src_refs[a] if gather else src_refs[a].at[peer]
            sems = dict(send_sem=send_sems.at[(k - 1) * n + a], recv_sem=recv_sems.at[(k - 1) * n + a],
                        device_id=(px, py, pc), device_id_type=MESH_ID)
            sends.append(pltpu.make_async_remote_copy(src_ref=src, dst_ref=land_refs[a].at[me], **sems))
            lands.append(pltpu.make_async_remote_copy(src_ref=src, dst_ref=land_refs[a].at[peer], **sems))
    return sends, lands


def _exchange_start(srcs, gather, name):
    n = len(srcs)
    shapes = [((N_DEV,) + s.shape) if gather else s.shape for s in srcs]
    n_sem = (N_DEV - 1) * n

    def body(*refs):
        src_refs, land_refs = refs[:n], refs[n:2 * n]
        send_sems, recv_sems = refs[2 * n], refs[2 * n + 1]
        token = refs[-1]
        sends, _ = _peer_copies(src_refs, land_refs, send_sems, recv_sems, gather)
        for cp in sends:
            cp.start()
        token[...] = jnp.zeros_like(token)

    hbm = lambda arrays_shapes: [pltpu.HBM(sh, dt) for sh, dt in arrays_shapes]
    src_types = [(s.shape, s.dtype) for s in srcs]
    land_types = [(sh, s.dtype) for sh, s in zip(shapes, srcs)]
    outs = pl.pallas_call(
        body, name=name,
        out_shape=(pltpu.SemaphoreType.DMA((n_sem,)), pltpu.SemaphoreType.DMA((n_sem,)),
                   *hbm(src_types), *hbm(land_types), jax.ShapeDtypeStruct((8, CHUNK), F32)),
        in_specs=[HBM_SPEC] * (2 * n),
        out_specs=(SEM_SPEC, SEM_SPEC, *([HBM_SPEC] * (2 * n)), pl.BlockSpec(memory_space=pltpu.VMEM)),
        input_output_aliases={a: 2 + a for a in range(2 * n)},
        compiler_params=pltpu.CompilerParams(has_side_effects=DATAFLOW),
    )(*[pltpu.with_memory_space_constraint(s, pltpu.HBM) for s in srcs],
      *[pltpu.with_memory_space_constraint(lax.empty(sh, dt), pltpu.HBM) for sh, dt in land_types])
    return (outs[0], outs[1], outs[2:2 + n], outs[2 + n:2 + 2 * n]), outs[-1]


def _exchange_wait(handle, after, gather, name):
    send_sems, recv_sems, src_thru, land_thru = handle
    n = len(src_thru)

    def body(*refs):
        src_refs, land_refs = refs[:n], refs[n:2 * n]
        _, lands = _peer_copies(src_refs, land_refs, refs[2 * n], refs[2 * n + 1], gather)
        for cp in lands:
            cp.wait_send()
            cp.wait_recv()

    outs = pl.pallas_call(
        body, name=name,
        out_shape=tuple(pltpu.HBM(t.shape, t.dtype) for t in (*src_thru, *land_thru)),
        in_specs=[*([HBM_SPEC] * (2 * n)), SEM_SPEC, SEM_SPEC, pl.BlockSpec(memory_space=pl.ANY)],
        out_specs=tuple([HBM_SPEC] * (2 * n)),
        input_output_aliases={a: a for a in range(2 * n)},
        compiler_params=pltpu.CompilerParams(has_side_effects=DATAFLOW),
    )(*src_thru, *land_thru, send_sems, recv_sems, after)
    return outs[n:]


def _fill_own(land, own, me):
    return lax.dynamic_update_slice_in_dim(land, own[None].astype(land.dtype), me, axis=0)


BIG = ("w_in", "w_pool", "w_out", "w_gate", "w_up", "w_down")
SHARD_AXIS = dict(w_in=-1, w_pool=-2, w_out=-2, w_gate=-1, w_up=-1, w_down=-2)
SMALL = ("norm_mix_pre", "norm_mix_post", "norm_ffn_pre", "norm_ffn_post", "pool_scale")


def _join(g, axis):
    return jnp.concatenate([g[j] for j in range(N_DEV)], axis=axis)


def _split(full, axis):
    return jnp.stack(jnp.split(full, N_DEV, axis=axis))


def _rows2d(a):
    return a.reshape(-1, a.shape[-1])


def kernel(x, meta_tokens, norm_mix_pre, norm_mix_post, norm_ffn_pre, norm_ffn_post, w_in, b_forget, w_pool, pool_scale, w_out, w_ffn_gate, w_ffn_up, w_ffn_down, loss_target, m_meta_tokens, m_norm_mix_pre, m_norm_mix_post, m_norm_ffn_pre, m_norm_ffn_post, m_w_in, m_b_forget, m_w_pool, m_pool_scale, m_w_out, m_w_ffn_gate, m_w_ffn_up, m_w_ffn_down, v_meta_tokens, v_norm_mix_pre, v_norm_mix_post, v_norm_ffn_pre, v_norm_ffn_post, v_w_in, v_b_forget, v_w_pool, v_pool_scale, v_w_out, v_w_ffn_gate, v_w_ffn_up, v_w_ffn_down):
    x2 = x[0]
    target = loss_target[0]
    seq, d = x2.shape
    depth = w_in.shape[0]
    heads = d // HEAD_DIM
    ff = w_ffn_gate.shape[2] * N_DEV
    rows = PAD_ROWS + META_TOKENS + seq
    assert seq % CHUNK == 0 and d % (2 * CHUNK) == 0 and heads <= FORGET_PAD and depth >= 2
    me = 4 * lax.axis_index("x") + 2 * lax.axis_index("y") + lax.axis_index("c")

    big = dict(w_in=w_in, w_pool=w_pool, w_out=w_out, w_gate=w_ffn_gate, w_up=w_ffn_up, w_down=w_ffn_down)
    big_m = dict(w_in=m_w_in, w_pool=m_w_pool, w_out=m_w_out, w_gate=m_w_ffn_gate, w_up=m_w_ffn_up, w_down=m_w_ffn_down)
    big_v = dict(w_in=v_w_in, w_pool=v_w_pool, w_out=v_w_out, w_gate=v_w_ffn_gate, w_up=v_w_ffn_up, w_down=v_w_ffn_down)
    small = dict(norm_mix_pre=norm_mix_pre, norm_mix_post=norm_mix_post, norm_ffn_pre=norm_ffn_pre,
                 norm_ffn_post=norm_ffn_post, pool_scale=pool_scale)
    small_m = dict(norm_mix_pre=m_norm_mix_pre, norm_mix_post=m_norm_mix_post, norm_ffn_pre=m_norm_ffn_pre,
                   norm_ffn_post=m_norm_ffn_post, pool_scale=m_pool_scale)
    small_v = dict(norm_mix_pre=v_norm_mix_pre, norm_mix_post=v_norm_mix_post, norm_ffn_pre=v_norm_ffn_pre,
                   norm_ffn_post=v_norm_ffn_post, pool_scale=v_pool_scale)

    wire = {n: big[n].astype(BF16) for n in BIG}
    others = [n for n in BIG if n != "w_in"]
    gathered = _exchange([wire["w_in"][:1], meta_tokens], True, "gather_first")
    meta_full = _join(gathered[1], -1)
    later_srcs = [wire["w_in"][1:]] + [wire[n] for n in others]
    later, started = _exchange_start(later_srcs, True, "gather_later_start")

    def in_weights(lands):
        win = _join(lands, -1)
        fcol = 4 * d
        qkv_w = win[:, :, d:4 * d]
        rest_w = jnp.concatenate([win[:, :, :d], win[:, :, fcol + heads:], win[:, :, fcol:fcol + heads],
                                  jnp.zeros(win.shape[:2] + (FORGET_PAD - heads,), BF16)], axis=2)
        return qkv_w, rest_w, jnp.concatenate([qkv_w, rest_w], axis=2)

    first_in = in_weights(gathered[0])
    w_qkv, w_rest, w_cat, w_gu, w_o, w_dn, w_pl = (
        [None] * depth for _ in range(7))

    h_res = jnp.concatenate([jnp.zeros((PAD_ROWS, d), F32), meta_full, x2], axis=0)
    h1 = _norm_fwd(h_res, norm_mix_pre[0] + started[0, 0])
    ones = jnp.ones((d,), F32)
    saved = []
    for l in range(depth):
        w_qkv[l], w_rest[l], w_cat[l] = (t[0] for t in first_in) if l == 0 else (t[l - 1] for t in later_in)
        qkv = _matmul(h1, w_qkv[l], "nn", BF16, "proj_qkv")
        rest = _matmul(h1, w_rest[l], "nn", F32, "proj_rest")
        z = rest[:, 3 * d:3 * d + heads].T
        bias = jnp.broadcast_to(b_forget[l][:, None], (heads, CHUNK))
        q_aug, k_aug, v_own, v_aug = _attn_prep(qkv, _logf_fwd(z, bias).T, d)
        o, lse = _attn_fwd(q_aug, k_aug, v_aug, d)
        if l == 0:
            lands = _exchange_wait(later, o, True, "gather_later_wait")
            lands = [_fill_own(g, src, me) for g, src in zip(lands, later_srcs)]
            later_in = in_weights(lands[0])
            full = {n: _join(g, SHARD_AXIS[n]) for n, g in zip(others, lands[1:])}
            gate_up = jnp.concatenate([full["w_gate"], full["w_up"]], axis=2)
        w_gu[l], w_o[l], w_dn[l], w_pl[l] = gate_up[l], full["w_out"][l], full["w_down"][l], full["w_pool"][l]
        merged, ypre = _pool_merge_fwd(rest, o, w_pl[l], pool_scale[l], d)
        mix = _matmul(merged, w_o[l], "nn", F32, "mix_out")
        h_mid, h2 = _resid_norm_fwd(h_res, mix, norm_mix_post[l], norm_ffn_pre[l])
        ab = _matmul(h2, w_gu[l], "nn", F32, "ffn_in")
        act = _swiglu_fwd(ab)
        ffo = _matmul(act, w_dn[l], "nn", F32, "ffn_out")
        g_next = norm_mix_pre[l + 1] if l + 1 < depth else ones
        h_next, h1_next = _resid_norm_fwd(h_mid, ffo, norm_ffn_post[l], g_next)
        saved.append(dict(h_in=h_res, h1=h1, q_aug=q_aug, k_aug=k_aug, v_own=v_own, rest=rest, z=z,
                          bias=bias, o=o, lse=lse,
                          merged=merged, ypre=ypre, mix=mix, h_mid=h_mid, h2=h2, ab=ab, act=act, ffo=ffo))
        h_res, h1 = h_next, h1_next

    dh, loss_local = _loss_grad(h_res, target)
    loss = lax.psum(loss_local, ("x", "y", "c"))

    grads = {n: [None] * depth for n in BIG + SMALL + ("b_forget",)}

    def grad_slots(n, first, last):
        part = _split(jnp.stack(grads[n][first:last]), SHARD_AXIS[n]).astype(BF16)
        return part.reshape(N_DEV, -1, part.shape[-1])

    last_made = ("w_in", "w_pool")

    for l in reversed(range(depth)):
        s = saved[l]
        dffo, grads["norm_ffn_post"][l] = _norm_bwd(s["ffo"], norm_ffn_post[l], dh, None, BF16)
        grads["w_down"][l] = _matmul(s["act"], dffo, "tn", F32, "grad_w_down")
        dact = _matmul(dffo, w_dn[l], "nt", F32, "ffn_out_dx")
        dab = _swiglu_bwd(s["ab"], dact)
        dgu = _matmul(s["h2"], dab, "tn", F32, "grad_w_gu")
        grads["w_gate"][l], grads["w_up"][l] = dgu[:, :ff], dgu[:, ff:]
        dh2 = _matmul(dab, w_gu[l], "nt", F32, "ffn_in_dx")
        dh_mid, grads["norm_ffn_pre"][l] = _norm_bwd(s["h_mid"], norm_ffn_pre[l], dh2, dh, F32)

        dmix, grads["norm_mix_post"][l] = _norm_bwd(s["mix"], norm_mix_post[l], dh_mid, None, BF16)
        grads["w_out"][l] = _matmul(s["merged"], dmix, "tn", F32, "grad_w_out")
        dm = _matmul(dmix, w_o[l], "nt", F32, "mix_out_dx")
        scale_l = pool_scale[l]
        if l == 0:
            early_slots = [grad_slots(n, 1 if n in last_made else 0, depth) for n in BIG]
            early, started = _exchange_start(early_slots, False, "scatter_early_start")
            scale_l = scale_l + started[0, 0]
        dgp, dga, do, delta, dypre, grads["pool_scale"][l] = _gate_bwd(dm, s["rest"], s["o"], s["ypre"], scale_l, d)
        du, grads["w_pool"][l] = _pool_bwd(dypre, s["rest"], w_pl[l], d)
        dq, dk, dv, df_key, df_query = _attn_bwd(s["q_aug"], s["k_aug"], s["v_own"], do, s["lse"], delta, d)
        df = df_key.reshape(heads, rows) + df_query.reshape(rows, heads, HEAD_DIM)[:, :, 0].T
        dz, grads["b_forget"][l] = _logf_bwd(df, s["z"], s["bias"])
        dzt = jnp.pad(dz.T.astype(BF16), ((0, 0), (0, FORGET_PAD - heads)))
        dproj = jnp.concatenate([dq, dk, dv, du, dgp, dga, dzt], axis=1)
        dwc = _matmul(s["h1"], dproj, "tn", F32, "grad_w_in")
        grads["w_in"][l] = jnp.concatenate([dwc[:, 3 * d:4 * d], dwc[:, :3 * d], dwc[:, 6 * d:6 * d + heads],
                                            dwc[:, 4 * d:6 * d]], axis=1)
        dh1 = _matmul(dproj, w_cat[l], "nt", F32, "proj_dx")
        dh, grads["norm_mix_pre"][l] = _norm_bwd(s["h_in"], norm_mix_pre[l], dh1, dh_mid, F32)

    grad_x = dh[PAD_ROWS + META_TOKENS:][None]
    dmeta = dh[PAD_ROWS:PAD_ROWS + META_TOKENS]

    late_recv = dict(zip(last_made, _exchange([grad_slots(n, 0, 1) for n in last_made], False, "scatter_last")))
    early_recv = _exchange_wait(early, dh, False, "scatter_early_wait")
    early_recv = [_fill_own(r, lax.dynamic_index_in_dim(s_, me, 0, keepdims=False), me)
                  for r, s_ in zip(early_recv, early_slots)]
    recv = [jnp.concatenate([late_recv[n], r], axis=1) if n in last_made else r for n, r in zip(BIG, early_recv)]
    big_out = {}
    for n, r in zip(BIG, recv):
        outs = _adamw(r, _rows2d(big[n]), _rows2d(big_m[n]), _rows2d(big_v[n]), "adamw_" + n)
        big_out[n] = [o_.reshape(big[n].shape) for o_ in outs]

    def table(parts, forget):
        t = jnp.concatenate([_rows2d(p) for p in parts] + [jnp.pad(forget, ((0, 0), (0, d - heads)))], axis=0)
        return jnp.pad(t, ((0, -t.shape[0] % 8), (0, 0)))

    g_table = table([jnp.stack(grads[n]) for n in SMALL], jnp.stack(grads["b_forget"]))
    rep_rows = g_table.shape[0]
    got = _exchange([jnp.concatenate([g_table, dmeta], axis=0)], True, "gather_small_grads")[0]
    outs = _adamw(got[:, :rep_rows], table([small[n] for n in SMALL], b_forget),
                  table([small_m[n] for n in SMALL], m_b_forget), table([small_v[n] for n in SMALL], v_b_forget),
                  "adamw_small")
    dcols = d // N_DEV
    meta_slots = lax.dynamic_slice_in_dim(got[:, rep_rows:], me * dcols, dcols, axis=2)
    meta_out = _adamw(meta_slots, meta_tokens, m_meta_tokens, v_meta_tokens, "adamw_meta")

    def ordered(k):
        t = outs[k]
        so = {n: t[a * depth:(a + 1) * depth] for a, n in enumerate(SMALL)}
        forget = t[len(SMALL) * depth:(len(SMALL) + 1) * depth, :heads]
        return (meta_out[k], so["norm_mix_pre"], so["norm_mix_post"], so["norm_ffn_pre"], so["norm_ffn_post"],
                big_out["w_in"][k], forget, big_out["w_pool"][k], so["pool_scale"], big_out["w_out"][k],
                big_out["w_gate"][k], big_out["w_up"][k], big_out["w_down"][k])

    return (loss, grad_x) + ordered(0) + ordered(1) + ordered(2) + ordered(3)
```

```python
import math

import jax
import jax.numpy as jnp
from jax import lax
from jax.experimental import pallas as pl
from jax.experimental.pallas import tpu as pltpu

F32 = jnp.float32
BF16 = jnp.bfloat16

N_DEV = 8
META_TOKENS = 16
PAD_ROWS = 112
CHUNK = 128
HEAD_DIM = 64
POOL_WINDOWS = (2, 4, 8, 16)
FORGET_PAD = 256
RMS_EPS = 1e-6
NEG_INF = -1e30
ADAM_LR, ADAM_B1, ADAM_B2, ADAM_EPS, ADAM_WD, ADAM_STEP = 0.001, 0.9, 0.999, 1e-08, 0.01, 10

VMEM_LIMIT = 56 * 1024 * 1024
VMEM_TILE_BUDGET = 36 * 1024 * 1024
ADAM_TILE_BYTES = 8 * 1024 * 1024
MESH_ID = pl.DeviceIdType.MESH


def _params(sem, vmem=VMEM_LIMIT):
    return pltpu.CompilerParams(dimension_semantics=sem, vmem_limit_bytes=vmem)


def _divisors(n, mult):
    return [d for d in range(mult, n + 1, mult) if n % d == 0]


def _row_tile(rows, cap):
    return max(d for d in _divisors(rows, CHUNK) if d <= max(cap, CHUNK))


def _fold8(x):
    r, c = x.shape
    return x.reshape(r // 8, 8, c).sum(axis=0)


def _split_bf16(x, parts):
    out = []
    for _ in range(parts - 1):
        hi = x.astype(BF16)
        out.append(hi)
        x = x - hi.astype(F32)
    out.append(x.astype(BF16))
    return out


def _apply01(mat, x, parts, left=True):
    acc = None
    for p in _split_bf16(x, parts):
        t = jnp.dot(mat, p, preferred_element_type=F32) if left else jnp.dot(p, mat, preferred_element_type=F32)
        acc = t if acc is None else acc + t
    return acc


def _matmul_tiles(m, n, k, mode, out_bytes):
    if mode == "tn":
        tk = _row_tile(k, 640)
    else:
        tk = max(d for d in _divisors(k, CHUNK) if d <= 1536)
    nk = k // tk
    best = None
    m_opts = _divisors(m, CHUNK)
    n_opts = _divisors(n, CHUNK)
    for tm in m_opts:
        for tn in n_opts:
            need = 2 * 2 * (tm * tk + tk * tn) + 2 * tm * tn * out_bytes
            if nk > 1 or mode == "tn":
                need += tm * tn * 4
            need += tm * tn * 4
            if need > VMEM_TILE_BUDGET:
                continue
            key = (tm * tn, tn)
            if best is None or key > best[0]:
                best = (key, tm, tn)
    return best[1], best[2], tk


def _matmul(a, b, mode, out_dtype, name):
    if mode == "nn":
        (m, k), (k2, n) = a.shape, b.shape
    elif mode == "nt":
        (m, k), (n, k2) = a.shape, b.shape
    else:
        (k, m), (k2, n) = a.shape, b.shape
    assert k == k2 and a.dtype == BF16 and b.dtype == BF16
    tm, tn, tk = _matmul_tiles(m, n, k, mode, jnp.dtype(out_dtype).itemsize)
    nk = k // tk
    if mode == "nn":
        a_spec = pl.BlockSpec((tm, tk), lambda i, j, r: (i, r))
        b_spec = pl.BlockSpec((tk, tn), lambda i, j, r: (r, j))
        dims = (((1,), (0,)), ((), ()))
    elif mode == "nt":
        a_spec = pl.BlockSpec((tm, tk), lambda i, j, r: (i, r))
        b_spec = pl.BlockSpec((tn, tk), lambda i, j, r: (j, r))
        dims = (((1,), (1,)), ((), ()))
    else:
        a_spec = pl.BlockSpec((tk, tm), lambda i, j, r: (r, i))
        b_spec = pl.BlockSpec((tk, tn), lambda i, j, r: (r, j))
        dims = (((0,), (0,)), ((), ()))

    def body(a_ref, b_ref, o_ref, *acc):
        part = lax.dot_general(a_ref[...], b_ref[...], dims, preferred_element_type=F32)
        if nk == 1:
            o_ref[...] = part.astype(o_ref.dtype)
        else:
            r = pl.program_id(2)

            @pl.when(r == 0)
            def _():
                acc[0][...] = part

            @pl.when(r > 0)
            def _():
                acc[0][...] += part

            @pl.when(r == nk - 1)
            def _():
                o_ref[...] = acc[0][...].astype(o_ref.dtype)

    return pl.pallas_call(
        body, name=name,
        out_shape=jax.ShapeDtypeStruct((m, n), out_dtype),
        grid=(m // tm, n // tn, nk),
        in_specs=[a_spec, b_spec],
        out_specs=pl.BlockSpec((tm, tn), lambda i, j, r: (i, j)),
        scratch_shapes=[pltpu.VMEM((tm, tn), F32)] if nk > 1 else [],
        compiler_params=_params(("parallel", "parallel", "arbitrary")),
    )(a, b)


def _rms(x, g):
    rstd = lax.rsqrt(jnp.mean(x * x, axis=-1, keepdims=True) + RMS_EPS)
    return x * rstd * g


def _norm_fwd(x, g):
    rows, d = x.shape
    tm = _row_tile(rows, 640)

    def body(x_ref, g_ref, h_ref):
        h_ref[...] = _rms(x_ref[...], g_ref[...]).astype(BF16)

    return pl.pallas_call(
        body, name="norm_fwd",
        out_shape=jax.ShapeDtypeStruct((rows, d), BF16),
        grid=(rows // tm,),
        in_specs=[pl.BlockSpec((tm, d), lambda i: (i, 0)), pl.BlockSpec((1, d), lambda i: (0, 0))],
        out_specs=pl.BlockSpec((tm, d), lambda i: (i, 0)),
        compiler_params=_params(("parallel",)),
    )(x, g.reshape(1, d))


def _resid_norm_fwd(h_res, y, g_post, g_next):
    rows, d = h_res.shape
    tm = _row_tile(rows, 640)

    def body(r_ref, y_ref, gp_ref, gn_ref, hn_ref, hx_ref):
        h_new = r_ref[...] + _rms(y_ref[...], gp_ref[...])
        hn_ref[...] = h_new
        hx_ref[...] = _rms(h_new, gn_ref[...]).astype(BF16)

    row = pl.BlockSpec((tm, d), lambda i: (i, 0))
    vec = pl.BlockSpec((1, d), lambda i: (0, 0))
    return pl.pallas_call(
        body, name="resid_norm_fwd",
        out_shape=(jax.ShapeDtypeStruct((rows, d), F32), jax.ShapeDtypeStruct((rows, d), BF16)),
        grid=(rows // tm,),
        in_specs=[row, row, vec, vec],
        out_specs=(row, row),
        compiler_params=_params(("parallel",)),
    )(h_res, y, g_post.reshape(1, d), g_next.reshape(1, d))


def _norm_bwd(x, g, dy, resid, out_dtype):
    rows, d = x.shape
    tm = _row_tile(rows, 640)
    has_resid = resid is not None

    def body(*refs):
        if has_resid:
            x_ref, g_ref, dy_ref, r_ref, dx_ref, dg_ref = refs
        else:
            x_ref, g_ref, dy_ref, dx_ref, dg_ref = refs
        xv = x_ref[...]
        dyv = dy_ref[...].astype(F32)
        rstd = lax.rsqrt(jnp.mean(xv * xv, axis=-1, keepdims=True) + RMS_EPS)
        xhat = xv * rstd
        gdy = dyv * g_ref[...]
        dx = rstd * (gdy - xhat * jnp.mean(gdy * xhat, axis=-1, keepdims=True))
        if has_resid:
            dx = dx + r_ref[...]
        dx_ref[...] = dx.astype(dx_ref.dtype)

        @pl.when(pl.program_id(0) == 0)
        def _():
            dg_ref[...] = jnp.zeros_like(dg_ref)

        dg_ref[...] += _fold8(dyv * xhat)

    row = pl.BlockSpec((tm, d), lambda i: (i, 0))
    vec = pl.BlockSpec((1, d), lambda i: (0, 0))
    args = [x, g.reshape(1, d), dy] + ([resid] if has_resid else [])
    dx, dg = pl.pallas_call(
        body, name="norm_bwd_resid" if has_resid else "norm_bwd",
        out_shape=(jax.ShapeDtypeStruct((rows, d), out_dtype), jax.ShapeDtypeStruct((8, d), F32)),
        grid=(rows // tm,),
        in_specs=[row, vec, row] + ([row] if has_resid else []),
        out_specs=(row, pl.BlockSpec((8, d), lambda i: (0, 0))),
        compiler_params=_params(("arbitrary",)),
    )(*args)
    return dx, dg.sum(axis=0)


def _swiglu_fwd(ab):
    rows, two_f = ab.shape
    f = two_f // 2
    tm = _row_tile(rows, 640)

    def body(a_ref, b_ref, o_ref):
        a = a_ref[...]
        o_ref[...] = (a * jax.nn.sigmoid(a) * b_ref[...]).astype(BF16)

    return pl.pallas_call(
        body, name="swiglu_fwd",
        out_shape=jax.ShapeDtypeStruct((rows, f), BF16),
        grid=(rows // tm,),
        in_specs=[pl.BlockSpec((tm, f), lambda i: (i, 0)), pl.BlockSpec((tm, f), lambda i: (i, 1))],
        out_specs=pl.BlockSpec((tm, f), lambda i: (i, 0)),
        compiler_params=_params(("parallel",)),
    )(ab, ab)


def _swiglu_bwd(ab, dff):
    rows, two_f = ab.shape
    f = two_f // 2
    tm = _row_tile(rows, 256)

    def body(a_ref, b_ref, d_ref, dab_ref):
        a = a_ref[...]
        d = d_ref[...]
        s = jax.nn.sigmoid(a)
        dab_ref[:, :f] = (d * b_ref[...] * (s * (1.0 + a * (1.0 - s)))).astype(BF16)
        dab_ref[:, f:] = (d * (a * s)).astype(BF16)

    lo = pl.BlockSpec((tm, f), lambda i: (i, 0))
    hi = pl.BlockSpec((tm, f), lambda i: (i, 1))
    return pl.pallas_call(
        body, name="swiglu_bwd",
        out_shape=jax.ShapeDtypeStruct((rows, two_f), BF16),
        grid=(rows // tm,),
        in_specs=[lo, hi, lo],
        out_specs=pl.BlockSpec((tm, two_f), lambda i: (i, 0)),
        compiler_params=_params(("parallel",)),
    )(ab, ab, dff)


def _tri(lower):
    r = lax.broadcasted_iota(jnp.int32, (CHUNK, CHUNK), 0)
    c = lax.broadcasted_iota(jnp.int32, (CHUNK, CHUNK), 1)
    return jnp.where((r >= c) if lower else (r <= c), 1.0, 0.0).astype(BF16)


def _logf_group(rows):
    n = rows // CHUNK
    return max(g for g in range(1, 9) if n % g == 0)


def _logf_fwd(z, b):
    h, rows = z.shape
    g = _logf_group(rows)
    n = rows // (g * CHUNK)

    def body(z_ref, b_ref, f_ref, carry):
        i = pl.program_id(0)

        @pl.when(i == 0)
        def _():
            carry[...] = jnp.zeros_like(carry)

        c = carry[...]
        for t in range(g):
            sl = slice(t * CHUNK, (t + 1) * CHUNK)
            x = z_ref[:, sl] + b_ref[...]
            lf = jnp.minimum(x, 0.0) - jnp.log(1.0 + jnp.exp(-jnp.abs(x)))
            col = (i * g + t) * CHUNK + lax.broadcasted_iota(jnp.int32, (1, CHUNK), 1)
            lf = jnp.where(col >= PAD_ROWS, lf, 0.0)
            run = _apply01(_tri(False), lf, 3, left=False) + c
            f_ref[:, sl] = run
            c = jnp.broadcast_to(run[:, CHUNK - 1:CHUNK], c.shape)
        carry[...] = c

    blk = pl.BlockSpec((h, g * CHUNK), lambda i: (0, i))
    return pl.pallas_call(
        body, name="logf_fwd",
        out_shape=jax.ShapeDtypeStruct((h, rows), F32),
        grid=(n,),
        in_specs=[blk, pl.BlockSpec((h, CHUNK), lambda i: (0, 0))],
        out_specs=blk,
        scratch_shapes=[pltpu.VMEM((h, CHUNK), F32)],
        compiler_params=_params(("arbitrary",)),
    )(z, b)


def _logf_bwd(df, z, b):
    h, rows = z.shape
    g = _logf_group(rows)
    n = rows // (g * CHUNK)

    def body(df_ref, z_ref, b_ref, dz_ref, db_ref, carry):
        i = pl.program_id(0)

        @pl.when(i == 0)
        def _():
            carry[...] = jnp.zeros_like(carry)
            db_ref[...] = jnp.zeros_like(db_ref)

        c = carry[...]
        db = db_ref[...]
        for t in reversed(range(g)):
            sl = slice(t * CHUNK, (t + 1) * CHUNK)
            run = _apply01(_tri(True), df_ref[:, sl], 3, left=False) + c
            c = jnp.broadcast_to(run[:, 0:1], c.shape)
            x = z_ref[:, sl] + b_ref[...]
            col = ((n - 1 - i) * g + t) * CHUNK + lax.broadcasted_iota(jnp.int32, (1, CHUNK), 1)
            dz = jnp.where(col >= PAD_ROWS, run * (1.0 - jax.nn.sigmoid(x)), 0.0)
            dz_ref[:, sl] = dz
            db = db + dz
        carry[...] = c
        db_ref[...] = db

    rev = pl.BlockSpec((h, g * CHUNK), lambda i: (0, n - 1 - i))
    fix = pl.BlockSpec((h, CHUNK), lambda i: (0, 0))
    dz, db = pl.pallas_call(
        body, name="logf_bwd",
        out_shape=(jax.ShapeDtypeStruct((h, rows), F32), jax.ShapeDtypeStruct((h, CHUNK), F32)),
        grid=(n,),
        in_specs=[rev, rev, fix],
        out_specs=(rev, fix),
        scratch_shapes=[pltpu.VMEM((h, CHUNK), F32)],
        compiler_params=_params(("arbitrary",)),
    )(df, z, b)
    return dz, db.sum(axis=1)


def _lane_lo():
    return lax.broadcasted_iota(jnp.int32, (1, CHUNK), 1) < HEAD_DIM


def _attn_block(rows):
    return _row_tile(rows, min(640, rows // 2))


def _masked_logits(s, i, j, blk):
    row = i * blk + lax.broadcasted_iota(jnp.int32, (blk, 1), 0)
    col = j * blk + lax.broadcasted_iota(jnp.int32, (1, blk), 1)
    return jnp.where(col <= row, s, NEG_INF)


def _attn_prep(qkv, f_t, d):
    rows = qkv.shape[0]
    heads = d // HEAD_DIM
    hp = heads // 2
    tm = _row_tile(rows, 640)
    scale = 1.0 / math.sqrt(HEAD_DIM)

    def body(q_ref, k_ref, v_ref, f_ref, qa_ref, ka_ref, vo_ref, va_ref):
        pr = pl.program_id(1)
        lane = lax.broadcasted_iota(jnp.int32, (1, CHUNK), 1)
        lo = lane < HEAD_DIM
        q2 = (q_ref[...].astype(F32) * scale).astype(BF16)
        k2 = k_ref[...]
        v2 = v_ref[...]
        zero = jnp.zeros_like(k2)
        head_id = lax.broadcasted_iota(jnp.int32, (1, heads), 1)
        ft = f_ref[...]
        is_pad = pl.program_id(0) * tm + lax.broadcasted_iota(jnp.int32, (tm, 1), 0) < PAD_ROWS
        for hh in range(2):
            base = HEAD_DIM if hh == 0 else 0
            neg_f = -jnp.sum(jnp.where(head_id == 2 * pr + hh, ft, 0.0), axis=1, keepdims=True)
            neg_f = jnp.where(is_pad, NEG_INF, neg_f)
            aug = zero
            ones = zero
            for t, part in enumerate(_split_bf16(neg_f, 3)):
                aug = jnp.where(lane == base + t, part, aug)
                ones = jnp.where(lane == base + t, jnp.ones_like(zero), ones)
            own = lo if hh == 0 else jnp.logical_not(lo)
            sl = slice(hh * CHUNK, (hh + 1) * CHUNK)
            qa_ref[:, sl] = jnp.where(own, q2, ones)
            ka_ref[:, sl] = jnp.where(own, k2, aug)
            vo_ref[:, sl] = jnp.where(own, v2, zero)
            va_ref[:, sl] = jnp.where(own, v2, jnp.where(lane == base, jnp.ones_like(zero), zero))

    pair_in = lambda c: pl.BlockSpec((tm, CHUNK), lambda i, p: (i, c * hp + p))
    pair_out = pl.BlockSpec((tm, 2 * CHUNK), lambda i, p: (i, p))
    sds = jax.ShapeDtypeStruct((rows, 2 * d), BF16)
    return pl.pallas_call(
        body, name="attn_prep",
        out_shape=(sds, sds, sds, sds),
        grid=(rows // tm, hp),
        in_specs=[pair_in(0), pair_in(1), pair_in(2), pl.BlockSpec((tm, heads), lambda i, p: (i, 0))],
        out_specs=(pair_out, pair_out, pair_out, pair_out),
        compiler_params=_params(("parallel", "parallel")),
    )(qkv, qkv, qkv, f_t)


FWD_PAIRS = 4


def _attn_fwd(q_aug, k_aug, v_aug, d):
    rows = q_aug.shape[0]
    hp = d // CHUNK
    blk = _attn_block(rows)
    nb = rows // blk
    nt = (((1,), (1,)), ((), ()))
    den_lane = (HEAD_DIM, 0)
    gp = FWD_PAIRS if hp % FWD_PAIRS == 0 else 1
    nh = 2 * gp

    def body(qi_ref, kj_ref, q_ref, k_ref, v_ref, o_ref, lse_ref, m_s, acc_s):
        i = qi_ref[pl.program_id(1)]
        j = kj_ref[pl.program_id(1)]

        @pl.when(j == 0)
        def _():
            m_s[...] = jnp.full(m_s.shape, NEG_INF, F32)
            acc_s[...] = jnp.zeros_like(acc_s)

        def step(masked):
            tiles = [slice(hh * CHUNK, (hh + 1) * CHUNK) for hh in range(nh)]
            scores = [lax.dot_general(q_ref[:, ln], k_ref[:, ln], nt, preferred_element_type=F32) for ln in tiles]
            for hh in range(nh):
                s = scores[hh]
                if masked:
                    s = _masked_logits(s, i, j, blk)
                m_prev = m_s[hh]
                m_new = jnp.maximum(m_prev, s.max(axis=-1, keepdims=True))
                p = jnp.exp(s - m_new)
                m_s[hh] = m_new
                pv = jnp.dot(p.astype(BF16), v_ref[:, tiles[hh]], preferred_element_type=F32)
                acc_s[hh] = acc_s[hh] * jnp.exp(m_prev - m_new) + pv

        edge = j == i

        @pl.when(edge)
        def _():
            step(True)

        @pl.when(jnp.logical_not(edge))
        def _():
            step(False)

        @pl.when(j == i)
        def _():
            row = i * blk + lax.broadcasted_iota(jnp.int32, (blk, 1), 0)
            lo = _lane_lo()
            for pr in range(gp):
                acc = [acc_s[2 * pr + hh] for hh in range(2)]
                den = [acc[hh][:, den_lane[hh]:den_lane[hh] + 1] for hh in range(2)]
                o = jnp.where(lo, acc[0] * (1.0 / den[0]), acc[1] * (1.0 / den[1]))
                sl = slice(pr * CHUNK, (pr + 1) * CHUNK)
                o_ref[:, sl] = jnp.where(row >= PAD_ROWS, o, 0.0)
                lse_ref[:, sl] = jnp.where(lo, m_s[2 * pr] + jnp.log(den[0]), m_s[2 * pr + 1] + jnp.log(den[1]))

    pairs = [(i, j) for i in range(nb) for j in range(i + 1)]
    q_spec = pl.BlockSpec((blk, nh * CHUNK), lambda h, t, qi, kj: (qi[t], h))
    k_spec = pl.BlockSpec((blk, nh * CHUNK), lambda h, t, qi, kj: (kj[t], h))
    o_spec = pl.BlockSpec((blk, gp * CHUNK), lambda h, t, qi, kj: (qi[t], h))
    return pl.pallas_call(
        body, name="attn_fwd",
        out_shape=(jax.ShapeDtypeStruct((rows, d), F32), jax.ShapeDtypeStruct((rows, d), F32)),
        grid_spec=pltpu.PrefetchScalarGridSpec(
            num_scalar_prefetch=2, grid=(hp // gp, len(pairs)),
            in_specs=[q_spec, k_spec, k_spec], out_specs=(o_spec, o_spec),
            scratch_shapes=[pltpu.VMEM((nh, blk, 1), F32), pltpu.VMEM((nh, blk, CHUNK), F32)]),
        compiler_params=_params(("parallel", "arbitrary")),
    )(jnp.array([p[0] for p in pairs], jnp.int32), jnp.array([p[1] for p in pairs], jnp.int32),
      q_aug, k_aug, v_aug)


def _attn_bwd(q_aug, k_aug, v_own, do, lse, delta, d):
    rows = q_aug.shape[0]
    hp = d // CHUNK
    blk = _attn_block(rows)
    nb = rows // blk
    scale = 1.0 / math.sqrt(HEAD_DIM)
    nt = (((1,), (1,)), ((), ()))
    tn = (((0,), (0,)), ((), ()))

    def body(kj_ref, qi_ref, q_ref, k_ref, v_ref, do_ref, lse_ref, dl_ref, dq_ref, dk_ref, dv_ref, df_ref, rs_ref,
             dq_s, dk_s, dv_s, df_s, rs_s):
        j = kj_ref[pl.program_id(1)]
        i = qi_ref[pl.program_id(1)]
        lo = _lane_lo()

        @pl.when((j == 0) & (i == 0))
        def _():
            dq_s[...] = jnp.zeros_like(dq_s)
            rs_s[...] = jnp.zeros_like(rs_s)

        @pl.when(i == j)
        def _():
            dk_s[...] = jnp.zeros_like(dk_s)
            dv_s[...] = jnp.zeros_like(dv_s)
            df_s[...] = jnp.zeros_like(df_s)

        def step(masked):
            dov = do_ref[...]
            t_dq, t_dk, t_dv, row_sums = [], [], [], []
            for hh in range(2):
                lanes = slice(hh * CHUNK, (hh + 1) * CHUNK)
                qh, kh = q_ref[:, lanes], k_ref[:, lanes]
                off = hh * HEAD_DIM
                s = lax.dot_general(qh, kh, nt, preferred_element_type=F32)
                if masked:
                    s = _masked_logits(s, i, j, blk)
                p = jnp.exp(s - lse_ref[:, off:off + 1])
                dp = lax.dot_general(dov, v_ref[:, lanes], nt, preferred_element_type=F32)
                ds = p * (dp - dl_ref[:, off:off + 1])
                df_s[hh:hh + 1, :] += ds.sum(axis=0, keepdims=True)
                row_sums.append(ds.sum(axis=1, keepdims=True))
                pb = p.astype(BF16)
                dsb = ds.astype(BF16)
                t_dv.append(lax.dot_general(pb, dov, tn, preferred_element_type=F32))
                t_dk.append(lax.dot_general(dsb, qh, tn, preferred_element_type=F32))
                t_dq.append(jnp.dot(dsb, kh, preferred_element_type=F32))
            dv_s[...] += jnp.where(lo, t_dv[0], t_dv[1])
            dk_s[...] += jnp.where(lo, t_dk[0], t_dk[1])
            r0 = pl.multiple_of(i * blk, blk)
            dq_s[pl.ds(r0, blk), :] += jnp.where(lo, t_dq[0], t_dq[1])
            rs_s[pl.ds(r0, blk), :] += jnp.where(lo, row_sums[0], row_sums[1])

        edge = j == i

        @pl.when(edge)
        def _():
            step(True)

        @pl.when(jnp.logical_not(edge))
        def _():
            step(False)

        @pl.when(i == nb - 1)
        def _():
            dk_ref[...] = dk_s[...].astype(BF16)
            dv_ref[...] = dv_s[...].astype(BF16)
            df_ref[...] = -df_s[...]

        @pl.when((i == nb - 1) & (j == nb - 1))
        def _():
            dq_ref[...] = (dq_s[...] * scale).astype(BF16)
            rs_ref[...] = rs_s[...]

    pairs = [(j, i) for j in range(nb) for i in range(j, nb)]
    q_spec = pl.BlockSpec((blk, 2 * CHUNK), lambda h, t, kj, qi: (qi[t], h))
    kv_spec = pl.BlockSpec((blk, 2 * CHUNK), lambda h, t, kj, qi: (kj[t], h))
    row_spec = pl.BlockSpec((blk, CHUNK), lambda h, t, kj, qi: (qi[t], h))
    kv_out = pl.BlockSpec((blk, CHUNK), lambda h, t, kj, qi: (kj[t], h))
    dq_out = pl.BlockSpec((rows, CHUNK), lambda h, t, kj, qi: (0, h))
    return pl.pallas_call(
        body, name="attn_bwd",
        out_shape=(jax.ShapeDtypeStruct((rows, d), BF16), jax.ShapeDtypeStruct((rows, d), BF16),
                   jax.ShapeDtypeStruct((rows, d), BF16), jax.ShapeDtypeStruct((hp, 2, rows), F32),
                   jax.ShapeDtypeStruct((rows, d), F32)),
        grid_spec=pltpu.PrefetchScalarGridSpec(
            num_scalar_prefetch=2, grid=(hp, len(pairs)),
            in_specs=[q_spec, kv_spec, kv_spec, row_spec, row_spec, row_spec],
            out_specs=(dq_out, kv_out, kv_out,
                       pl.BlockSpec((None, 2, blk), lambda h, t, kj, qi: (h, 0, kj[t])), dq_out),
            scratch_shapes=[pltpu.VMEM((rows, CHUNK), F32), pltpu.VMEM((blk, CHUNK), F32),
                            pltpu.VMEM((blk, CHUNK), F32), pltpu.VMEM((2, blk), F32),
                            pltpu.VMEM((rows, CHUNK), F32)]),
        compiler_params=_params(("parallel", "arbitrary")),
    )(jnp.array([p[0] for p in pairs], jnp.int32), jnp.array([p[1] for p in pairs], jnp.int32),
      q_aug, k_aug, v_own, do, lse, delta)


def _band(w, transposed, other):
    r = lax.broadcasted_iota(jnp.int32, (CHUNK, CHUNK), 0)
    c = lax.broadcasted_iota(jnp.int32, (CHUNK, CHUNK), 1)
    dist = (c - r) if transposed else (r - c)
    if other:
        dist = dist + CHUNK
    return jnp.where(dist >= 0, jnp.where(dist < w, 1.0, 0.0), 0.0).astype(BF16)


def _inv_count(chunk_index, w):
    row = chunk_index * CHUNK + lax.broadcasted_iota(jnp.int32, (CHUNK, 1), 0)
    cnt = jnp.clip(row - PAD_ROWS + 1, 1, w).astype(F32)
    return 1.0 / cnt


def _pool_diff(u_cur, u_prev, i, w):
    ws = _apply01(_band(w, False, False), u_cur, 3)
    ws = ws + jnp.where(i > 0, _apply01(_band(w, False, True), u_prev, 3), 0.0)
    return ws * _inv_count(i, w) - u_cur


def _pool_merge_fwd(rest, o, w_pool, scale, d):
    rows = rest.shape[0]
    n = rows // CHUNK
    cg = d // len(POOL_WINDOWS)

    def body(up_ref, uc_ref, gp_ref, ga_ref, o_ref, wp_ref, sc_ref, mg_ref, yp_ref):
        i = pl.program_id(0)
        for g, w in enumerate(POOL_WINDOWS):
            sl = slice(g * cg, (g + 1) * cg)
            diff = _pool_diff(uc_ref[:, sl], up_ref[:, sl], i, w)
            ypre = jnp.dot(diff.astype(BF16), wp_ref[g], preferred_element_type=F32)
            yp_ref[:, sl] = ypre
            merged = (jax.nn.sigmoid(gp_ref[:, sl]) * (ypre * sc_ref[:, sl])
                      + jax.nn.sigmoid(ga_ref[:, sl]) * o_ref[:, sl])
            mg_ref[:, sl] = merged.astype(BF16)

    col = lambda c: pl.BlockSpec((CHUNK, d), lambda i: (i, c))
    return pl.pallas_call(
        body, name="pool_merge_fwd",
        out_shape=(jax.ShapeDtypeStruct((rows, d), BF16), jax.ShapeDtypeStruct((rows, d), F32)),
        grid=(n,),
        in_specs=[pl.BlockSpec((CHUNK, d), lambda i: (jnp.maximum(i - 1, 0), 0)), col(0), col(1), col(2), col(0),
                  pl.BlockSpec((len(POOL_WINDOWS), cg, cg), lambda i: (0, 0, 0)),
                  pl.BlockSpec((1, d), lambda i: (0, 0))],
        out_specs=(col(0), col(0)),
        compiler_params=_params(("parallel",)),
    )(rest, rest, rest, rest, o, w_pool, scale.reshape(1, d))


def _gate_bwd(dm, rest, o, ypre, scale, d):
    rows = rest.shape[0]
    n = rows // CHUNK

    def body(dm_ref, gp_ref, ga_ref, o_ref, yp_ref, sc_ref, dgp_ref, dga_ref, do_ref, dl_ref, dy_ref, ds_ref):
        @pl.when(pl.program_id(0) == 0)
        def _():
            ds_ref[...] = jnp.zeros_like(ds_ref)

        dmv = dm_ref[...]
        sp = jax.nn.sigmoid(gp_ref[...])
        sa = jax.nn.sigmoid(ga_ref[...])
        ov = o_ref[...]
        ypre_v = yp_ref[...]
        sc = sc_ref[...]
        dgp_ref[...] = (dmv * (ypre_v * sc) * (sp * (1.0 - sp))).astype(BF16)
        dga_ref[...] = (dmv * ov * (sa * (1.0 - sa))).astype(BF16)
        t = dmv * sp
        dy_ref[...] = (t * sc).astype(BF16)
        ds_ref[...] += _fold8(t * ypre_v)
        dob = (dmv * sa).astype(BF16)
        do_ref[...] = dob
        prod = dob.astype(F32) * ov
        lo = _lane_lo()
        for pr in range(d // CHUNK):
            sl = slice(pr * CHUNK, (pr + 1) * CHUNK)
            tp = prod[:, sl]
            s_lo = jnp.where(lo, tp, 0.0).sum(axis=-1, keepdims=True)
            s_hi = jnp.where(lo, 0.0, tp).sum(axis=-1, keepdims=True)
            dl_ref[:, sl] = jnp.where(lo, s_lo, s_hi)

    col = lambda c: pl.BlockSpec((CHUNK, d), lambda i: (i, c))
    row_bf = jax.ShapeDtypeStruct((rows, d), BF16)
    outs = pl.pallas_call(
        body, name="gate_bwd",
        out_shape=(row_bf, row_bf, row_bf, jax.ShapeDtypeStruct((rows, d), F32), row_bf,
                   jax.ShapeDtypeStruct((8, d), F32)),
        grid=(n,),
        in_specs=[col(0), col(1), col(2), col(0), col(0), pl.BlockSpec((1, d), lambda i: (0, 0))],
        out_specs=(col(0), col(0), col(0), col(0), col(0), pl.BlockSpec((8, d), lambda i: (0, 0))),
        compiler_params=_params(("arbitrary",)),
    )(dm, rest, rest, o, ypre, scale.reshape(1, d))
    return outs[:5] + (outs[5].sum(axis=0),)


def _pool_bwd(dypre, rest, w_pool, d):
    rows = rest.shape[0]
    n = rows // CHUNK
    ng = len(POOL_WINDOWS)
    cg = d // ng
    nt = (((1,), (1,)), ((), ()))
    tn = (((0,), (0,)), ((), ()))

    def body(dc_ref, dn_ref, up_ref, uc_ref, wp_ref, du_ref, dw_ref):
        i = pl.program_id(0)

        @pl.when(i == 0)
        def _():
            dw_ref[...] = jnp.zeros_like(dw_ref)

        row = i * CHUNK + lax.broadcasted_iota(jnp.int32, (CHUNK, 1), 0)
        for g, w in enumerate(POOL_WINDOWS):
            sl = slice(g * cg, (g + 1) * cg)
            diff = _pool_diff(uc_ref[:, sl], up_ref[:, sl], i, w)
            dyc = dc_ref[:, sl]
            dw_ref[g] += lax.dot_general(diff.astype(BF16), dyc, tn, preferred_element_type=F32)
            wg = wp_ref[g]
            dd_cur = lax.dot_general(dyc, wg, nt, preferred_element_type=F32)
            dd_next = lax.dot_general(dn_ref[:, sl], wg, nt, preferred_element_type=F32)
            du = _apply01(_band(w, True, False), dd_cur * _inv_count(i, w), 2)
            du = du + jnp.where(i < n - 1, _apply01(_band(w, True, True), dd_next * _inv_count(i + 1, w), 2), 0.0)
            du = du - dd_cur
            du_ref[:, sl] = jnp.where(row >= PAD_ROWS, du, 0.0).astype(BF16)

    cur = pl.BlockSpec((CHUNK, d), lambda i: (i, 0))
    return pl.pallas_call(
        body, name="pool_bwd",
        out_shape=(jax.ShapeDtypeStruct((rows, d), BF16), jax.ShapeDtypeStruct((ng, cg, cg), F32)),
        grid=(n,),
        in_specs=[cur, pl.BlockSpec((CHUNK, d), lambda i: (jnp.minimum(i + 1, n - 1), 0)),
                  pl.BlockSpec((CHUNK, d), lambda i: (jnp.maximum(i - 1, 0), 0)), cur,
                  pl.BlockSpec((ng, cg, cg), lambda i: (0, 0, 0))],
        out_specs=(cur, pl.BlockSpec((ng, cg, cg), lambda i: (0, 0, 0))),
        compiler_params=_params(("arbitrary",)),
    )(dypre, dypre, rest, rest, w_pool)


def _loss_grad(h_res, target):
    rows, d = h_res.shape
    n = rows // CHUNK

    def body(h_ref, t_ref, dh_ref, acc_ref):
        i = pl.program_id(0)

        @pl.when(i == 0)
        def _():
            acc_ref[...] = jnp.zeros_like(acc_ref)
            dh_ref[...] = jnp.zeros_like(dh_ref)

        @pl.when(i > 0)
        def _():
            err = h_ref[...] - t_ref[...]
            dh_ref[...] = err * (1.0 / d)
            e2 = _fold8(err * err)
            part = e2[:, 0:CHUNK]
            for c in range(1, d // CHUNK):
                part = part + e2[:, c * CHUNK:(c + 1) * CHUNK]
            acc_ref[...] += part

    dh, acc = pl.pallas_call(
        body, name="loss_grad",
        out_shape=(jax.ShapeDtypeStruct((rows, d), F32), jax.ShapeDtypeStruct((8, CHUNK), F32)),
        grid=(n,),
        in_specs=[pl.BlockSpec((CHUNK, d), lambda i: (i, 0)),
                  pl.BlockSpec((CHUNK, d), lambda i: (jnp.maximum(i - 1, 0), 0))],
        out_specs=(pl.BlockSpec((CHUNK, d), lambda i: (i, 0)), pl.BlockSpec((8, CHUNK), lambda i: (0, 0))),
        compiler_params=_params(("arbitrary",)),
    )(h_res, target)
    return dh, (0.5 / d) * acc.sum()


def _adamw(slots, w, m, v, name):
    rows, cols = w.shape
    lanes = -(-cols // CHUNK) * CHUNK
    row_bytes = lanes * (N_DEV * slots.dtype.itemsize + 7 * 4)
    tr = max(t for t in _divisors(rows, 8) if t <= max(8, ADAM_TILE_BYTES // row_bytes))
    c1 = 1.0 - ADAM_B1 ** ADAM_STEP
    c2 = 1.0 - ADAM_B2 ** ADAM_STEP

    def body(s_ref, w_ref, m_ref, v_ref, g_ref, d_ref, mo_ref, vo_ref):
        g = s_ref[0].astype(F32)
        for k in range(1, N_DEV):
            g = g + s_ref[k].astype(F32)
        m_new = ADAM_B1 * m_ref[...] + (1.0 - ADAM_B1) * g
        v_new = ADAM_B2 * v_ref[...] + (1.0 - ADAM_B2) * (g * g)
        m_hat = m_new / c1
        v_hat = v_new / c2
        g_ref[...] = g
        d_ref[...] = -ADAM_LR * (m_hat / (jnp.sqrt(v_hat) + ADAM_EPS) + ADAM_WD * w_ref[...])
        mo_ref[...] = m_new
        vo_ref[...] = v_new

    tile = pl.BlockSpec((tr, cols), lambda i: (i, 0))
    sds = jax.ShapeDtypeStruct((rows, cols), F32)
    return pl.pallas_call(
        body, name=name,
        out_shape=(sds, sds, sds, sds),
        grid=(rows // tr,),
        in_specs=[pl.BlockSpec((N_DEV, tr, cols), lambda i: (0, i, 0)), tile, tile, tile],
        out_specs=(tile, tile, tile, tile),
        compiler_params=_params(("parallel",)),
    )(slots, w, m, v)


def _exchange(srcs, gather, name):
    n = len(srcs)
    shapes = [((N_DEV,) + s.shape) if gather else s.shape for s in srcs]

    def body(*refs):
        src_refs, out_refs = refs[:n], refs[n:2 * n]
        send_sems, recv_sems, local_sems = refs[2 * n:]
        x, y, c = lax.axis_index("x"), lax.axis_index("y"), lax.axis_index("c")
        me = 4 * x + 2 * y + c

        def payload(a, slot):
            return src_refs[a] if gather else src_refs[a].at[slot]

        own = [pltpu.make_async_copy(payload(a, me), out_refs[a].at[me], local_sems.at[a]) for a in range(n)]
        for cp in own:
            cp.start()
        sends, recvs = [], []
        for k in range(1, N_DEV):
            px = 1 - x if k & 4 else x
            py = 1 - y if k & 2 else y
            pc = 1 - c if k & 1 else c
            peer = 4 * px + 2 * py + pc
            for a in range(n):
                sems = dict(send_sem=send_sems.at[(k - 1) * n + a], recv_sem=recv_sems.at[(k - 1) * n + a],
                            device_id=(px, py, pc), device_id_type=MESH_ID)
                sends.append(pltpu.make_async_remote_copy(src_ref=payload(a, peer), dst_ref=out_refs[a].at[me], **sems))
                recvs.append(pltpu.make_async_remote_copy(src_ref=payload(a, peer), dst_ref=out_refs[a].at[peer], **sems))
        for cp in sends:
            cp.start()
        for cp in recvs:
            cp.wait_recv()
        for cp in sends:
            cp.wait_send()
        for cp in own:
            cp.wait()

    return pl.pallas_call(
        body, name=name,
        out_shape=tuple(jax.ShapeDtypeStruct(sh, s.dtype) for sh, s in zip(shapes, srcs)),
        in_specs=[pl.BlockSpec(memory_space=pl.ANY)] * n,
        out_specs=tuple([pl.BlockSpec(memory_space=pl.ANY)] * n),
        scratch_shapes=[pltpu.SemaphoreType.DMA(((N_DEV - 1) * n,)), pltpu.SemaphoreType.DMA(((N_DEV - 1) * n,)),
                        pltpu.SemaphoreType.DMA((n,))],
    )(*srcs)


HBM_SPEC = pl.BlockSpec(memory_space=pltpu.HBM)
SEM_SPEC = pl.BlockSpec(memory_space=pltpu.SEMAPHORE)
DATAFLOW = pltpu.SideEffectType.DATAFLOW_SIDE_EFFECTING


def _peer_copies(src_refs, land_refs, send_sems, recv_sems, gather):
    n = len(src_refs)
    x, y, c = lax.axis_index("x"), lax.axis_index("y"), lax.axis_index("c")
    me = 4 * x + 2 * y + c
    sends, lands = [], []
    for k in range(1, N_DEV):
        px = 1 - x if k & 4 else x
        py = 1 - y if k & 2 else y
        pc = 1 - c if k & 1 else c
        peer = 4 * px + 2 * py + pc
        for a in range(n):
            src = src_refs[a] if gather else src_refs[a].at[peer]
            sems = dict(send_sem=send_sems.at[(k - 1) * n + a], recv_sem=recv_sems.at[(k - 1) * n + a],
                        device_id=(px, py, pc), device_id_type=MESH_ID)
            sends.append(pltpu.make_async_remote_copy(src_ref=src, dst_ref=land_refs[a].at[me], **sems))
            lands.append(pltpu.make_async_remote_copy(src_ref=src, dst_ref=land_refs[a].at[peer], **sems))
    return sends, lands


def _exchange_start(srcs, gather, name):
    n = len(srcs)
    shapes = [((N_DEV,) + s.shape) if gather else s.shape for s in srcs]
    n_sem = (N_DEV - 1) * n

    def body(*refs):
        src_refs, land_refs = refs[:n], refs[n:2 * n]
        send_sems, recv_sems = refs[2 * n], refs[2 * n + 1]
        token = refs[-1]
        sends, _ = _peer_copies(src_refs, land_refs, send_sems, recv_sems, gather)
        for cp in sends:
            cp.start()
        token[...] = jnp.zeros_like(token)

    hbm = lambda arrays_shapes: [pltpu.HBM(sh, dt) for sh, dt in arrays_shapes]
    src_types = [(s.shape, s.dtype) for s in srcs]
    land_types = [(sh, s.dtype) for sh, s in zip(shapes, srcs)]
    outs = pl.pallas_call(
        body, name=name,
        out_shape=(pltpu.SemaphoreType.DMA((n_sem,)), pltpu.SemaphoreType.DMA((n_sem,)),
                   *hbm(src_types), *hbm(land_types), jax.ShapeDtypeStruct((8, CHUNK), F32)),
        in_specs=[HBM_SPEC] * (2 * n),
        out_specs=(SEM_SPEC, SEM_SPEC, *([HBM_SPEC] * (2 * n)), pl.BlockSpec(memory_space=pltpu.VMEM)),
        input_output_aliases={a: 2 + a for a in range(2 * n)},
        compiler_params=pltpu.CompilerParams(has_side_effects=DATAFLOW),
    )(*[pltpu.with_memory_space_constraint(s, pltpu.HBM) for s in srcs],
      *[pltpu.with_memory_space_constraint(lax.empty(sh, dt), pltpu.HBM) for sh, dt in land_types])
    return (outs[0], outs[1], outs[2:2 + n], outs[2 + n:2 + 2 * n]), outs[-1]


def _exchange_wait(handle, after, gather, name):
    send_sems, recv_sems, src_thru, land_thru = handle
    n = len(src_thru)

    def body(*refs):
        src_refs, land_refs = refs[:n], refs[n:2 * n]
        _, lands = _peer_copies(src_refs, land_refs, refs[2 * n], refs[2 * n + 1], gather)
        for cp in lands:
            cp.wait_send()
            cp.wait_recv()

    outs = pl.pallas_call(
        body, name=name,
        out_shape=tuple(pltpu.HBM(t.shape, t.dtype) for t in (*src_thru, *land_thru)),
        in_specs=[*([HBM_SPEC] * (2 * n)), SEM_SPEC, SEM_SPEC, pl.BlockSpec(memory_space=pl.ANY)],
        out_specs=tuple([HBM_SPEC] * (2 * n)),
        input_output_aliases={a: a for a in range(2 * n)},
        compiler_params=pltpu.CompilerParams(has_side_effects=DATAFLOW),
    )(*src_thru, *land_thru, send_sems, recv_sems, after)
    return outs[n:]


def _fill_own(land, own, me):
    return lax.dynamic_update_slice_in_dim(land, own[None].astype(land.dtype), me, axis=0)


BIG = ("w_in", "w_pool", "w_out", "w_gate", "w_up", "w_down")
SHARD_AXIS = dict(w_in=-1, w_pool=-2, w_out=-2, w_gate=-1, w_up=-1, w_down=-2)
SMALL = ("norm_mix_pre", "norm_mix_post", "norm_ffn_pre", "norm_ffn_post", "pool_scale")


def _join(g, axis):
    return jnp.concatenate([g[j] for j in range(N_DEV)], axis=axis)


def _split(full, axis):
    return jnp.stack(jnp.split(full, N_DEV, axis=axis))


def _rows2d(a):
    return a.reshape(-1, a.shape[-1])


def kernel(x, meta_tokens, norm_mix_pre, norm_mix_post, norm_ffn_pre, norm_ffn_post, w_in, b_forget, w_pool, pool_scale, w_out, w_ffn_gate, w_ffn_up, w_ffn_down, loss_target, m_meta_tokens, m_norm_mix_pre, m_norm_mix_post, m_norm_ffn_pre, m_norm_ffn_post, m_w_in, m_b_forget, m_w_pool, m_pool_scale, m_w_out, m_w_ffn_gate, m_w_ffn_up, m_w_ffn_down, v_meta_tokens, v_norm_mix_pre, v_norm_mix_post, v_norm_ffn_pre, v_norm_ffn_post, v_w_in, v_b_forget, v_w_pool, v_pool_scale, v_w_out, v_w_ffn_gate, v_w_ffn_up, v_w_ffn_down):
    x2 = x[0]
    target = loss_target[0]
    seq, d = x2.shape
    depth = w_in.shape[0]
    heads = d // HEAD_DIM
    ff = w_ffn_gate.shape[2] * N_DEV
    rows = PAD_ROWS + META_TOKENS + seq
    assert seq % CHUNK == 0 and d % (2 * CHUNK) == 0 and heads <= FORGET_PAD and depth >= 2
    me = 4 * lax.axis_index("x") + 2 * lax.axis_index("y") + lax.axis_index("c")

    big = dict(w_in=w_in, w_pool=w_pool, w_out=w_out, w_gate=w_ffn_gate, w_up=w_ffn_up, w_down=w_ffn_down)
    big_m = dict(w_in=m_w_in, w_pool=m_w_pool, w_out=m_w_out, w_gate=m_w_ffn_gate, w_up=m_w_ffn_up, w_down=m_w_ffn_down)
    big_v = dict(w_in=v_w_in, w_pool=v_w_pool, w_out=v_w_out, w_gate=v_w_ffn_gate, w_up=v_w_ffn_up, w_down=v_w_ffn_down)
    small = dict(norm_mix_pre=norm_mix_pre, norm_mix_post=norm_mix_post, norm_ffn_pre=norm_ffn_pre,
                 norm_ffn_post=norm_ffn_post, pool_scale=pool_scale)
    small_m = dict(norm_mix_pre=m_norm_mix_pre, norm_mix_post=m_norm_mix_post, norm_ffn_pre=m_norm_ffn_pre,
                   norm_ffn_post=m_norm_ffn_post, pool_scale=m_pool_scale)
    small_v = dict(norm_mix_pre=v_norm_mix_pre, norm_mix_post=v_norm_mix_post, norm_ffn_pre=v_norm_ffn_pre,
                   norm_ffn_post=v_norm_ffn_post, pool_scale=v_pool_scale)

    wire = {n: big[n].astype(BF16) for n in BIG}
    others = [n for n in BIG if n != "w_in"]
    gathered = _exchange([wire["w_in"][:1], meta_tokens], True, "gather_first")
    meta_full = _join(gathered[1], -1)
    later_srcs = [wire["w_in"][1:]] + [wire[n] for n in others]
    later, started = _exchange_start(later_srcs, True, "gather_later_start")

    def in_weights(lands):
        win = _join(lands, -1)
        fcol = 4 * d
        qkv_w = win[:, :, d:4 * d]
        rest_w = jnp.concatenate([win[:, :, :d], win[:, :, fcol + heads:], win[:, :, fcol:fcol + heads],
                                  jnp.zeros(win.shape[:2] + (FORGET_PAD - heads,), BF16)], axis=2)
        return qkv_w, rest_w, jnp.concatenate([qkv_w, rest_w], axis=2)

    first_in = in_weights(gathered[0])
    w_qkv, w_rest, w_cat, w_gu, w_o, w_dn, w_pl = (
        [None] * depth for _ in range(7))

    h_res = jnp.concatenate([jnp.zeros((PAD_ROWS, d), F32), meta_full, x2], axis=0)
    h1 = _norm_fwd(h_res, norm_mix_pre[0] + started[0, 0])
    ones = jnp.ones((d,), F32)
    saved = []
    for l in range(depth):
        w_qkv[l], w_rest[l], w_cat[l] = (t[0] for t in first_in) if l == 0 else (t[l - 1] for t in later_in)
        qkv = _matmul(h1, w_qkv[l], "nn", BF16, "proj_qkv")
        rest = _matmul(h1, w_rest[l], "nn", F32, "proj_rest")
        z = rest[:, 3 * d:3 * d + heads].T
        bias = jnp.broadcast_to(b_forget[l][:, None], (heads, CHUNK))
        q_aug, k_aug, v_own, v_aug = _attn_prep(qkv, _logf_fwd(z, bias).T, d)
        o, lse = _attn_fwd(q_aug, k_aug, v_aug, d)
        if l == 0:
            lands = _exchange_wait(later, o, True, "gather_later_wait")
            lands = [_fill_own(g, src, me) for g, src in zip(lands, later_srcs)]
            later_in = in_weights(lands[0])
            full = {n: _join(g, SHARD_AXIS[n]) for n, g in zip(others, lands[1:])}
            gate_up = jnp.concatenate([full["w_gate"], full["w_up"]], axis=2)
        w_gu[l], w_o[l], w_dn[l], w_pl[l] = gate_up[l], full["w_out"][l], full["w_down"][l], full["w_pool"][l]
        merged, ypre = _pool_merge_fwd(rest, o, w_pl[l], pool_scale[l], d)
        mix = _matmul(merged, w_o[l], "nn", F32, "mix_out")
        h_mid, h2 = _resid_norm_fwd(h_res, mix, norm_mix_post[l], norm_ffn_pre[l])
        ab = _matmul(h2, w_gu[l], "nn", F32, "ffn_in")
        act = _swiglu_fwd(ab)
        ffo = _matmul(act, w_dn[l], "nn", F32, "ffn_out")
        g_next = norm_mix_pre[l + 1] if l + 1 < depth else ones
        h_next, h1_next = _resid_norm_fwd(h_mid, ffo, norm_ffn_post[l], g_next)
        saved.append(dict(h_in=h_res, h1=h1, q_aug=q_aug, k_aug=k_aug, v_own=v_own, rest=rest, z=z,
                          bias=bias, o=o, lse=lse,
                          merged=merged, ypre=ypre, mix=mix, h_mid=h_mid, h2=h2, ab=ab, act=act, ffo=ffo))
        h_res, h1 = h_next, h1_next

    dh, loss_local = _loss_grad(h_res, target)
    loss = lax.psum(loss_local, ("x", "y", "c"))

    grads = {n: [None] * depth for n in BIG + SMALL + ("b_forget",)}

    def grad_slots(n, first, last):
        part = _split(jnp.stack(grads[n][first:last]), SHARD_AXIS[n]).astype(BF16)
        return part.reshape(N_DEV, -1, part.shape[-1])

    last_made = ("w_in", "w_pool")

    for l in reversed(range(depth)):
        s = saved[l]
        dffo, grads["norm_ffn_post"][l] = _norm_bwd(s["ffo"], norm_ffn_post[l], dh, None, BF16)
        grads["w_down"][l] = _matmul(s["act"], dffo, "tn", F32, "grad_w_down")
        dact = _matmul(dffo, w_dn[l], "nt", F32, "ffn_out_dx")
        dab = _swiglu_bwd(s["ab"], dact)
        dgu = _matmul(s["h2"], dab, "tn", F32, "grad_w_gu")
        grads["w_gate"][l], grads["w_up"][l] = dgu[:, :ff], dgu[:, ff:]
        dh2 = _matmul(dab, w_gu[l], "nt", F32, "ffn_in_dx")
        dh_mid, grads["norm_ffn_pre"][l] = _norm_bwd(s["h_mid"], norm_ffn_pre[l], dh2, dh, F32)

        dmix, grads["norm_mix_post"][l] = _norm_bwd(s["mix"], norm_mix_post[l], dh_mid, None, BF16)
        grads["w_out"][l] = _matmul(s["merged"], dmix, "tn", F32, "grad_w_out")
        dm = _matmul(dmix, w_o[l], "nt", F32, "mix_out_dx")
        scale_l = pool_scale[l]
        if l == 0:
            early_slots = [grad_slots(n, 1 if n in last_made else 0, depth) for n in BIG]
            early, started = _exchange_start(early_slots, False, "scatter_early_start")
            scale_l = scale_l + started[0, 0]
        dgp, dga, do, delta, dypre, grads["pool_scale"][l] = _gate_bwd(dm, s["rest"], s["o"], s["ypre"], scale_l, d)
        du, grads["w_pool"][l] = _pool_bwd(dypre, s["rest"], w_pl[l], d)
        dq, dk, dv, df_key, df_query = _attn_bwd(s["q_aug"], s["k_aug"], s["v_own"], do, s["lse"], delta, d)
        df = df_key.reshape(heads, rows) + df_query.reshape(rows, heads, HEAD_DIM)[:, :, 0].T
        dz, grads["b_forget"][l] = _logf_bwd(df, s["z"], s["bias"])
        dzt = jnp.pad(dz.T.astype(BF16), ((0, 0), (0, FORGET_PAD - heads)))
        dproj = jnp.concatenate([dq, dk, dv, du, dgp, dga, dzt], axis=1)
        dwc = _matmul(s["h1"], dproj, "tn", F32, "grad_w_in")
        grads["w_in"][l] = jnp.concatenate([dwc[:, 3 * d:4 * d], dwc[:, :3 * d], dwc[:, 6 * d:6 * d + heads],
                                            dwc[:, 4 * d:6 * d]], axis=1)
        dh1 = _matmul(dproj, w_cat[l], "nt", F32, "proj_dx")
        dh, grads["norm_mix_pre"][l] = _norm_bwd(s["h_in"], norm_mix_pre[l], dh1, dh_mid, F32)

    grad_x = dh[PAD_ROWS + META_TOKENS:][None]
    dmeta = dh[PAD_ROWS:PAD_ROWS + META_TOKENS]

    late_recv = dict(zip(last_made, _exchange([grad_slots(n, 0, 1) for n in last_made], False, "scatter_last")))
    early_recv = _exchange_wait(early, dh, False, "scatter_early_wait")
    early_recv = [_fill_own(r, lax.dynamic_index_in_dim(s_, me, 0, keepdims=False), me)
                  for r, s_ in zip(early_recv, early_slots)]
    recv = [jnp.concatenate([late_recv[n], r], axis=1) if n in last_made else r for n, r in zip(BIG, early_recv)]
    big_out = {}
    for n, r in zip(BIG, recv):
        outs = _adamw(r, _rows2d(big[n]), _rows2d(big_m[n]), _rows2d(big_v[n]), "adamw_" + n)
        big_out[n] = [o_.reshape(big[n].shape) for o_ in outs]

    def table(parts, forget):
        t = jnp.concatenate([_rows2d(p) for p in parts] + [jnp.pad(forget, ((0, 0), (0, d - heads)))], axis=0)
        return jnp.pad(t, ((0, -t.shape[0] % 8), (0, 0)))

    g_table = table([jnp.stack(grads[n]) for n in SMALL], jnp.stack(grads["b_forget"]))
    rep_rows = g_table.shape[0]
    got = _exchange([jnp.concatenate([g_table, dmeta], axis=0)], True, "gather_small_grads")[0]
    outs = _adamw(got[:, :rep_rows], table([small[n] for n in SMALL], b_forget),
                  table([small_m[n] for n in SMALL], m_b_forget), table([small_v[n] for n in SMALL], v_b_forget),
                  "adamw_small")
    dcols = d // N_DEV
    meta_slots = lax.dynamic_slice_in_dim(got[:, rep_rows:], me * dcols, dcols, axis=2)
    meta_out = _adamw(meta_slots, meta_tokens, m_meta_tokens, v_meta_tokens, "adamw_meta")

    def ordered(k):
        t = outs[k]
        so = {n: t[a * depth:(a + 1) * depth] for a, n in enumerate(SMALL)}
        forget = t[len(SMALL) * depth:(len(SMALL) + 1) * depth, :heads]
        return (meta_out[k], so["norm_mix_pre"], so["norm_mix_post"], so["norm_ffn_pre"], so["norm_ffn_post"],
                big_out["w_in"][k], forget, big_out["w_pool"][k], so["pool_scale"], big_out["w_out"][k],
                big_out["w_gate"][k], big_out["w_up"][k], big_out["w_down"][k])

    return (loss, grad_x) + ordered(0) + ordered(1) + ordered(2) + ordered(3)
```

```python
import math

import jax
import jax.numpy as jnp
from jax import lax
from jax.experimental import pallas as pl
from jax.experimental.pallas import tpu as pltpu

F32 = jnp.float32
BF16 = jnp.bfloat16

N_DEV = 8
META_TOKENS = 16
PAD_ROWS = 112
CHUNK = 128
HEAD_DIM = 64
POOL_WINDOWS = (2, 4, 8, 16)
FORGET_PAD = 256
RMS_EPS = 1e-6
NEG_INF = -1e30
ADAM_LR, ADAM_B1, ADAM_B2, ADAM_EPS, ADAM_WD, ADAM_STEP = 0.001, 0.9, 0.999, 1e-08, 0.01, 10

VMEM_LIMIT = 56 * 1024 * 1024
VMEM_TILE_BUDGET = 36 * 1024 * 1024
ADAM_TILE_BYTES = 8 * 1024 * 1024
MESH_ID = pl.DeviceIdType.MESH


def _params(sem, vmem=VMEM_LIMIT):
    return pltpu.CompilerParams(dimension_semantics=sem, vmem_limit_bytes=vmem)


def _divisors(n, mult):
    return [d for d in range(mult, n + 1, mult) if n % d == 0]


def _row_tile(rows, cap):
    return max(d for d in _divisors(rows, CHUNK) if d <= max(cap, CHUNK))


def _fold8(x):
    r, c = x.shape
    return x.reshape(r // 8, 8, c).sum(axis=0)


def _split_bf16(x, parts):
    out = []
    for _ in range(parts - 1):
        hi = x.astype(BF16)
        out.append(hi)
        x = x - hi.astype(F32)
    out.append(x.astype(BF16))
    return out


def _apply01(mat, x, parts, left=True):
    acc = None
    for p in _split_bf16(x, parts):
        t = jnp.dot(mat, p, preferred_element_type=F32) if left else jnp.dot(p, mat, preferred_element_type=F32)
        acc = t if acc is None else acc + t
    return acc


def _matmul_tiles(m, n, k, mode, out_bytes):
    if mode == "tn":
        tk = _row_tile(k, 640)
    else:
        tk = max(d for d in _divisors(k, CHUNK) if d <= 1536)
    nk = k // tk
    best = None
    m_opts = _divisors(m, CHUNK)
    n_opts = _divisors(n, CHUNK)
    for tm in m_opts:
        for tn in n_opts:
            need = 2 * 2 * (tm * tk + tk * tn) + 2 * tm * tn * out_bytes
            if nk > 1 or mode == "tn":
                need += tm * tn * 4
            need += tm * tn * 4
            if need > VMEM_TILE_BUDGET:
                continue
            key = (tm * tn, tn)
            if best is None or key > best[0]:
                best = (key, tm, tn)
    return best[1], best[2], tk


def _matmul(a, b, mode, out_dtype, name):
    if mode == "nn":
        (m, k), (k2, n) = a.shape, b.shape
    elif mode == "nt":
        (m, k), (n, k2) = a.shape, b.shape
    else:
        (k, m), (k2, n) = a.shape, b.shape
    assert k == k2 and a.dtype == BF16 and b.dtype == BF16
    tm, tn, tk = _matmul_tiles(m, n, k, mode, jnp.dtype(out_dtype).itemsize)
    nk = k // tk
    if mode == "nn":
        a_spec = pl.BlockSpec((tm, tk), lambda i, j, r: (i, r))
        b_spec = pl.BlockSpec((tk, tn), lambda i, j, r: (r, j))
        dims = (((1,), (0,)), ((), ()))
    elif mode == "nt":
        a_spec = pl.BlockSpec((tm, tk), lambda i, j, r: (i, r))
        b_spec = pl.BlockSpec((tn, tk), lambda i, j, r: (j, r))
        dims = (((1,), (1,)), ((), ()))
    else:
        a_spec = pl.BlockSpec((tk, tm), lambda i, j, r: (r, i))
        b_spec = pl.BlockSpec((tk, tn), lambda i, j, r: (r, j))
        dims = (((0,), (0,)), ((), ()))

    def body(a_ref, b_ref, o_ref, *acc):
        part = lax.dot_general(a_ref[...], b_ref[...], dims, preferred_element_type=F32)
        if nk == 1:
            o_ref[...] = part.astype(o_ref.dtype)
        else:
            r = pl.program_id(2)

            @pl.when(r == 0)
            def _():
                acc[0][...] = part

            @pl.when(r > 0)
            def _():
                acc[0][...] += part

            @pl.when(r == nk - 1)
            def _():
                o_ref[...] = acc[0][...].astype(o_ref.dtype)

    return pl.pallas_call(
        body, name=name,
        out_shape=jax.ShapeDtypeStruct((m, n), out_dtype),
        grid=(m // tm, n // tn, nk),
        in_specs=[a_spec, b_spec],
        out_specs=pl.BlockSpec((tm, tn), lambda i, j, r: (i, j)),
        scratch_shapes=[pltpu.VMEM((tm, tn), F32)] if nk > 1 else [],
        compiler_params=_params(("parallel", "parallel", "arbitrary")),
    )(a, b)


def _rms(x, g):
    rstd = lax.rsqrt(jnp.mean(x * x, axis=-1, keepdims=True) + RMS_EPS)
    return x * rstd * g


def _norm_fwd(x, g):
    rows, d = x.shape
    tm = _row_tile(rows, 640)

    def body(x_ref, g_ref, h_ref):
        h_ref[...] = _rms(x_ref[...], g_ref[...]).astype(BF16)

    return pl.pallas_call(
        body, name="norm_fwd",
        out_shape=jax.ShapeDtypeStruct((rows, d), BF16),
        grid=(rows // tm,),
        in_specs=[pl.BlockSpec((tm, d), lambda i: (i, 0)), pl.BlockSpec((1, d), lambda i: (0, 0))],
        out_specs=pl.BlockSpec((tm, d), lambda i: (i, 0)),
        compiler_params=_params(("parallel",)),
    )(x, g.reshape(1, d))


def _resid_norm_fwd(h_res, y, g_post, g_next):
    rows, d = h_res.shape
    tm = _row_tile(rows, 640)

    def body(r_ref, y_ref, gp_ref, gn_ref, hn_ref, hx_ref):
        h_new = r_ref[...] + _rms(y_ref[...], gp_ref[...])
        hn_ref[...] = h_new
        hx_ref[...] = _rms(h_new, gn_ref[...]).astype(BF16)

    row = pl.BlockSpec((tm, d), lambda i: (i, 0))
    vec = pl.BlockSpec((1, d), lambda i: (0, 0))
    return pl.pallas_call(
        body, name="resid_norm_fwd",
        out_shape=(jax.ShapeDtypeStruct((rows, d), F32), jax.ShapeDtypeStruct((rows, d), BF16)),
        grid=(rows // tm,),
        in_specs=[row, row, vec, vec],
        out_specs=(row, row),
        compiler_params=_params(("parallel",)),
    )(h_res, y, g_post.reshape(1, d), g_next.reshape(1, d))


def _norm_bwd(x, g, dy, resid, out_dtype):
    rows, d = x.shape
    tm = _row_tile(rows, 640)
    has_resid = resid is not None

    def body(*refs):
        if has_resid:
            x_ref, g_ref, dy_ref, r_ref, dx_ref, dg_ref = refs
        else:
            x_ref, g_ref, dy_ref, dx_ref, dg_ref = refs
        xv = x_ref[...]
        dyv = dy_ref[...].astype(F32)
        rstd = lax.rsqrt(jnp.mean(xv * xv, axis=-1, keepdims=True) + RMS_EPS)
        xhat = xv * rstd
        gdy = dyv * g_ref[...]
        dx = rstd * (gdy - xhat * jnp.mean(gdy * xhat, axis=-1, keepdims=True))
        if has_resid:
            dx = dx + r_ref[...]
        dx_ref[...] = dx.astype(dx_ref.dtype)

        @pl.when(pl.program_id(0) == 0)
        def _():
            dg_ref[...] = jnp.zeros_like(dg_ref)

        dg_ref[...] += _fold8(dyv * xhat)

    row = pl.BlockSpec((tm, d), lambda i: (i, 0))
    vec = pl.BlockSpec((1, d), lambda i: (0, 0))
    args = [x, g.reshape(1, d), dy] + ([resid] if has_resid else [])
    dx, dg = pl.pallas_call(
        body, name="norm_bwd_resid" if has_resid else "norm_bwd",
        out_shape=(jax.ShapeDtypeStruct((rows, d), out_dtype), jax.ShapeDtypeStruct((8, d), F32)),
        grid=(rows // tm,),
        in_specs=[row, vec, row] + ([row] if has_resid else []),
        out_specs=(row, pl.BlockSpec((8, d), lambda i: (0, 0))),
        compiler_params=_params(("arbitrary",)),
    )(*args)
    return dx, dg.sum(axis=0)


def _swiglu_fwd(ab):
    rows, two_f = ab.shape
    f = two_f // 2
    tm = _row_tile(rows, 640)

    def body(a_ref, b_ref, o_ref):
        a = a_ref[...]
        o_ref[...] = (a * jax.nn.sigmoid(a) * b_ref[...]).astype(BF16)

    return pl.pallas_call(
        body, name="swiglu_fwd",
        out_shape=jax.ShapeDtypeStruct((rows, f), BF16),
        grid=(rows // tm,),
        in_specs=[pl.BlockSpec((tm, f), lambda i: (i, 0)), pl.BlockSpec((tm, f), lambda i: (i, 1))],
        out_specs=pl.BlockSpec((tm, f), lambda i: (i, 0)),
        compiler_params=_params(("parallel",)),
    )(ab, ab)


def _swiglu_bwd(ab, dff):
    rows, two_f = ab.shape
    f = two_f // 2
    tm = _row_tile(rows, 256)

    def body(a_ref, b_ref, d_ref, dab_ref):
        a = a_ref[...]
        d = d_ref[...]
        s = jax.nn.sigmoid(a)
        dab_ref[:, :f] = (d * b_ref[...] * (s * (1.0 + a * (1.0 - s)))).astype(BF16)
        dab_ref[:, f:] = (d * (a * s)).astype(BF16)

    lo = pl.BlockSpec((tm, f), lambda i: (i, 0))
    hi = pl.BlockSpec((tm, f), lambda i: (i, 1))
    return pl.pallas_call(
        body, name="swiglu_bwd",
        out_shape=jax.ShapeDtypeStruct((rows, two_f), BF16),
        grid=(rows // tm,),
        in_specs=[lo, hi, lo],
        out_specs=pl.BlockSpec((tm, two_f), lambda i: (i, 0)),
        compiler_params=_params(("parallel",)),
    )(ab, ab, dff)


def _tri(lower):
    r = lax.broadcasted_iota(jnp.int32, (CHUNK, CHUNK), 0)
    c = lax.broadcasted_iota(jnp.int32, (CHUNK, CHUNK), 1)
    return jnp.where((r >= c) if lower else (r <= c), 1.0, 0.0).astype(BF16)


def _logf_group(rows):
    n = rows // CHUNK
    return max(g for g in range(1, 9) if n % g == 0)


def _logf_fwd(z, b):
    h, rows = z.shape
    g = _logf_group(rows)
    n = rows // (g * CHUNK)

    def body(z_ref, b_ref, f_ref, carry):
        i = pl.program_id(0)

        @pl.when(i == 0)
        def _():
            carry[...] = jnp.zeros_like(carry)

        c = carry[...]
        for t in range(g):
            sl = slice(t * CHUNK, (t + 1) * CHUNK)
            x = z_ref[:, sl] + b_ref[...]
            lf = jnp.minimum(x, 0.0) - jnp.log(1.0 + jnp.exp(-jnp.abs(x)))
            col = (i * g + t) * CHUNK + lax.broadcasted_iota(jnp.int32, (1, CHUNK), 1)
            lf = jnp.where(col >= PAD_ROWS, lf, 0.0)
            run = _apply01(_tri(False), lf, 3, left=False) + c
            f_ref[:, sl] = run
            c = jnp.broadcast_to(run[:, CHUNK - 1:CHUNK], c.shape)
        carry[...] = c

    blk = pl.BlockSpec((h, g * CHUNK), lambda i: (0, i))
    return pl.pallas_call(
        body, name="logf_fwd",
        out_shape=jax.ShapeDtypeStruct((h, rows), F32),
        grid=(n,),
        in_specs=[blk, pl.BlockSpec((h, CHUNK), lambda i: (0, 0))],
        out_specs=blk,
        scratch_shapes=[pltpu.VMEM((h, CHUNK), F32)],
        compiler_params=_params(("arbitrary",)),
    )(z, b)


def _logf_bwd(df, z, b):
    h, rows = z.shape
    g = _logf_group(rows)
    n = rows // (g * CHUNK)

    def body(df_ref, z_ref, b_ref, dz_ref, db_ref, carry):
        i = pl.program_id(0)

        @pl.when(i == 0)
        def _():
            carry[...] = jnp.zeros_like(carry)
            db_ref[...] = jnp.zeros_like(db_ref)

        c = carry[...]
        db = db_ref[...]
        for t in reversed(range(g)):
            sl = slice(t * CHUNK, (t + 1) * CHUNK)
            run = _apply01(_tri(True), df_ref[:, sl], 3, left=False) + c
            c = jnp.broadcast_to(run[:, 0:1], c.shape)
            x = z_ref[:, sl] + b_ref[...]
            col = ((n - 1 - i) * g + t) * CHUNK + lax.broadcasted_iota(jnp.int32, (1, CHUNK), 1)
            dz = jnp.where(col >= PAD_ROWS, run * (1.0 - jax.nn.sigmoid(x)), 0.0)
            dz_ref[:, sl] = dz
            db = db + dz
        carry[...] = c
        db_ref[...] = db

    rev = pl.BlockSpec((h, g * CHUNK), lambda i: (0, n - 1 - i))
    fix = pl.BlockSpec((h, CHUNK), lambda i: (0, 0))
    dz, db = pl.pallas_call(
        body, name="logf_bwd",
        out_shape=(jax.ShapeDtypeStruct((h, rows), F32), jax.ShapeDtypeStruct((h, CHUNK), F32)),
        grid=(n,),
        in_specs=[rev, rev, fix],
        out_specs=(rev, fix),
        scratch_shapes=[pltpu.VMEM((h, CHUNK), F32)],
        compiler_params=_params(("arbitrary",)),
    )(df, z, b)
    return dz, db.sum(axis=1)


def _lane_lo():
    return lax.broadcasted_iota(jnp.int32, (1, CHUNK), 1) < HEAD_DIM


def _attn_block(rows):
    return _row_tile(rows, min(640, rows // 2))


def _masked_logits(s, i, j, blk):
    row = i * blk + lax.broadcasted_iota(jnp.int32, (blk, 1), 0)
    col = j * blk + lax.broadcasted_iota(jnp.int32, (1, blk), 1)
    return jnp.where(col <= row, s, NEG_INF)


def _attn_prep(qkv, f_t, d):
    rows = qkv.shape[0]
    heads = d // HEAD_DIM
    hp = heads // 2
    tm = _row_tile(rows, 640)
    scale = 1.0 / math.sqrt(HEAD_DIM)

    def body(q_ref, k_ref, v_ref, f_ref, qa_ref, ka_ref, vo_ref, va_ref):
        pr = pl.program_id(1)
        lane = lax.broadcasted_iota(jnp.int32, (1, CHUNK), 1)
        lo = lane < HEAD_DIM
        q2 = (q_ref[...].astype(F32) * scale).astype(BF16)
        k2 = k_ref[...]
        v2 = v_ref[...]
        zero = jnp.zeros_like(k2)
        head_id = lax.broadcasted_iota(jnp.int32, (1, heads), 1)
        ft = f_ref[...]
        is_pad = pl.program_id(0) * tm + lax.broadcasted_iota(jnp.int32, (tm, 1), 0) < PAD_ROWS
        for hh in range(2):
            base = HEAD_DIM if hh == 0 else 0
            neg_f = -jnp.sum(jnp.where(head_id == 2 * pr + hh, ft, 0.0), axis=1, keepdims=True)
            neg_f = jnp.where(is_pad, NEG_INF, neg_f)
            aug = zero
            ones = zero
            for t, part in enumerate(_split_bf16(neg_f, 3)):
                aug = jnp.where(lane == base + t, part, aug)
                ones = jnp.where(lane == base + t, jnp.ones_like(zero), ones)
            own = lo if hh == 0 else jnp.logical_not(lo)
            sl = slice(hh * CHUNK, (hh + 1) * CHUNK)
            qa_ref[:, sl] = jnp.where(own, q2, ones)
            ka_ref[:, sl] = jnp.where(own, k2, aug)
            vo_ref[:, sl] = jnp.where(own, v2, zero)
            va_ref[:, sl] = jnp.where(own, v2, jnp.where(lane == base, jnp.ones_like(zero), zero))

    pair_in = lambda c: pl.BlockSpec((tm, CHUNK), lambda i, p: (i, c * hp + p))
    pair_out = pl.BlockSpec((tm, 2 * CHUNK), lambda i, p: (i, p))
    sds = jax.ShapeDtypeStruct((rows, 2 * d), BF16)
    return pl.pallas_call(
        body, name="attn_prep",
        out_shape=(sds, sds, sds, sds),
        grid=(rows // tm, hp),
        in_specs=[pair_in(0), pair_in(1), pair_in(2), pl.BlockSpec((tm, heads), lambda i, p: (i, 0))],
        out_specs=(pair_out, pair_out, pair_out, pair_out),
        compiler_params=_params(("parallel", "parallel")),
    )(qkv, qkv, qkv, f_t)


FWD_PAIRS = 4


def _attn_fwd(q_aug, k_aug, v_aug, d):
    rows = q_aug.shape[0]
    hp = d // CHUNK
    blk = _attn_block(rows)
    nb = rows // blk
    nt = (((1,), (1,)), ((), ()))
    den_lane = (HEAD_DIM, 0)
    gp = FWD_PAIRS if hp % FWD_PAIRS == 0 else 1
    nh = 2 * gp

    def body(qi_ref, kj_ref, q_ref, k_ref, v_ref, o_ref, lse_ref, m_s, acc_s):
        i = qi_ref[pl.program_id(1)]
        j = kj_ref[pl.program_id(1)]

        @pl.when(j == 0)
        def _():
            m_s[...] = jnp.full(m_s.shape, NEG_INF, F32)
            acc_s[...] = jnp.zeros_like(acc_s)

        def step(masked):
            tiles = [slice(hh * CHUNK, (hh + 1) * CHUNK) for hh in range(nh)]
            scores = [lax.dot_general(q_ref[:, ln], k_ref[:, ln], nt, preferred_element_type=F32) for ln in tiles]
            for hh in range(nh):
                s = scores[hh]
                if masked:
                    s = _masked_logits(s, i, j, blk)
                m_prev = m_s[hh]
                m_new = jnp.maximum(m_prev, s.max(axis=-1, keepdims=True))
                p = jnp.exp(s - m_new)
                m_s[hh] = m_new
                pv = jnp.dot(p.astype(BF16), v_ref[:, tiles[hh]], preferred_element_type=F32)
                acc_s[hh] = acc_s[hh] * jnp.exp(m_prev - m_new) + pv

        edge = j == i

        @pl.when(edge)
        def _():
            step(True)

        @pl.when(jnp.logical_not(edge))
        def _():
            step(False)

        @pl.when(j == i)
        def _():
            row = i * blk + lax.broadcasted_iota(jnp.int32, (blk, 1), 0)
            lo = _lane_lo()
            for pr in range(gp):
                acc = [acc_s[2 * pr + hh] for hh in range(2)]
                den = [acc[hh][:, den_lane[hh]:den_lane[hh] + 1] for hh in range(2)]
                o = jnp.where(lo, acc[0] * (1.0 / den[0]), acc[1] * (1.0 / den[1]))
                sl = slice(pr * CHUNK, (pr + 1) * CHUNK)
                o_ref[:, sl] = jnp.where(row >= PAD_ROWS, o, 0.0)
                lse_ref[:, sl] = jnp.where(lo, m_s[2 * pr] + jnp.log(den[0]), m_s[2 * pr + 1] + jnp.log(den[1]))

    pairs = [(i, j) for i in range(nb) for j in range(i + 1)]
    q_spec = pl.BlockSpec((blk, nh * CHUNK), lambda h, t, qi, kj: (qi[t], h))
    k_spec = pl.BlockSpec((blk, nh * CHUNK), lambda h, t, qi, kj: (kj[t], h))
    o_spec = pl.BlockSpec((blk, gp * CHUNK), lambda h, t, qi, kj: (qi[t], h))
    return pl.pallas_call(
        body, name="attn_fwd",
        out_shape=(jax.ShapeDtypeStruct((rows, d), F32), jax.ShapeDtypeStruct((rows, d), F32)),
        grid_spec=pltpu.PrefetchScalarGridSpec(
            num_scalar_prefetch=2, grid=(hp // gp, len(pairs)),
            in_specs=[q_spec, k_spec, k_spec], out_specs=(o_spec, o_spec),
            scratch_shapes=[pltpu.VMEM((nh, blk, 1), F32), pltpu.VMEM((nh, blk, CHUNK), F32)]),
        compiler_params=_params(("parallel", "arbitrary")),
    )(jnp.array([p[0] for p in pairs], jnp.int32), jnp.array([p[1] for p in pairs], jnp.int32),
      q_aug, k_aug, v_aug)


def _attn_bwd(q_aug, k_aug, v_own, do, lse, delta, d):
    rows = q_aug.shape[0]
    hp = d // CHUNK
    blk = _attn_block(rows)
    nb = rows // blk
    scale = 1.0 / math.sqrt(HEAD_DIM)
    nt = (((1,), (1,)), ((), ()))
    tn = (((0,), (0,)), ((), ()))

    def body(kj_ref, qi_ref, q_ref, k_ref, v_ref, do_ref, lse_ref, dl_ref, dq_ref, dk_ref, dv_ref, df_ref, rs_ref,
             dq_s, dk_s, dv_s, df_s, rs_s):
        j = kj_ref[pl.program_id(1)]
        i = qi_ref[pl.program_id(1)]
        lo = _lane_lo()

        @pl.when((j == 0) & (i == 0))
        def _():
            dq_s[...] = jnp.zeros_like(dq_s)
            rs_s[...] = jnp.zeros_like(rs_s)

        @pl.when(i == j)
        def _():
            dk_s[...] = jnp.zeros_like(dk_s)
            dv_s[...] = jnp.zeros_like(dv_s)
            df_s[...] = jnp.zeros_like(df_s)

        def step(masked):
            dov = do_ref[...]
            t_dq, t_dk, t_dv, row_sums = [], [], [], []
            for hh in range(2):
                lanes = slice(hh * CHUNK, (hh + 1) * CHUNK)
                qh, kh = q_ref[:, lanes], k_ref[:, lanes]
                off = hh * HEAD_DIM
                s = lax.dot_general(qh, kh, nt, preferred_element_type=F32)
                if masked:
                    s = _masked_logits(s, i, j, blk)
                p = jnp.exp(s - lse_ref[:, off:off + 1])
                dp = lax.dot_general(dov, v_ref[:, lanes], nt, preferred_element_type=F32)
                ds = p * (dp - dl_ref[:, off:off + 1])
                df_s[hh:hh + 1, :] += ds.sum(axis=0, keepdims=True)
                row_sums.append(ds.sum(axis=1, keepdims=True))
                pb = p.astype(BF16)
                dsb = ds.astype(BF16)
                t_dv.append(lax.dot_general(pb, dov, tn, preferred_element_type=F32))
                t_dk.append(lax.dot_general(dsb, qh, tn, preferred_element_type=F32))
                t_dq.append(jnp.dot(dsb, kh, preferred_element_type=F32))
            dv_s[...] += jnp.where(lo, t_dv[0], t_dv[1])
            dk_s[...] += jnp.where(lo, t_dk[0], t_dk[1])
            r0 = pl.multiple_of(i * blk, blk)
            dq_s[pl.ds(r0, blk), :] += jnp.where(lo, t_dq[0], t_dq[1])
            rs_s[pl.ds(r0, blk), :] += jnp.where(lo, row_sums[0], row_sums[1])

        edge = j == i

        @pl.when(edge)
        def _():
            step(True)

        @pl.when(jnp.logical_not(edge))
        def _():
            step(False)

        @pl.when(i == nb - 1)
        def _():
            dk_ref[...] = dk_s[...].astype(BF16)
            dv_ref[...] = dv_s[...].astype(BF16)
            df_ref[...] = -df_s[...]

        @pl.when((i == nb - 1) & (j == nb - 1))
        def _():
            dq_ref[...] = (dq_s[...] * scale).astype(BF16)
            rs_ref[...] = rs_s[...]

    pairs = [(j, i) for j in range(nb) for i in range(j, nb)]
    q_spec = pl.BlockSpec((blk, 2 * CHUNK), lambda h, t, kj, qi: (qi[t], h))
    kv_spec = pl.BlockSpec((blk, 2 * CHUNK), lambda h, t, kj, qi: (kj[t], h))
    row_spec = pl.BlockSpec((blk, CHUNK), lambda h, t, kj, qi: (qi[t], h))
    kv_out = pl.BlockSpec((blk, CHUNK), lambda h, t, kj, qi: (kj[t], h))
    dq_out = pl.BlockSpec((rows, CHUNK), lambda h, t, kj, qi: (0, h))
    return pl.pallas_call(
        body, name="attn_bwd",
        out_shape=(jax.ShapeDtypeStruct((rows, d), BF16), jax.ShapeDtypeStruct((rows, d), BF16),
                   jax.ShapeDtypeStruct((rows, d), BF16), jax.ShapeDtypeStruct((hp, 2, rows), F32),
                   jax.ShapeDtypeStruct((rows, d), F32)),
        grid_spec=pltpu.PrefetchScalarGridSpec(
            num_scalar_prefetch=2, grid=(hp, len(pairs)),
            in_specs=[q_spec, kv_spec, kv_spec, row_spec, row_spec, row_spec],
            out_specs=(dq_out, kv_out, kv_out,
                       pl.BlockSpec((None, 2, blk), lambda h, t, kj, qi: (h, 0, kj[t])), dq_out),
            scratch_shapes=[pltpu.VMEM((rows, CHUNK), F32), pltpu.VMEM((blk, CHUNK), F32),
                            pltpu.VMEM((blk, CHUNK), F32), pltpu.VMEM((2, blk), F32),
                            pltpu.VMEM((rows, CHUNK), F32)]),
        compiler_params=_params(("parallel", "arbitrary")),
    )(jnp.array([p[0] for p in pairs], jnp.int32), jnp.array([p[1] for p in pairs], jnp.int32),
      q_aug, k_aug, v_own, do, lse, delta)


def _band(w, transposed, other):
    r = lax.broadcasted_iota(jnp.int32, (CHUNK, CHUNK), 0)
    c = lax.broadcasted_iota(jnp.int32, (CHUNK, CHUNK), 1)
    dist = (c - r) if transposed else (r - c)
    if other:
        dist = dist + CHUNK
    return jnp.where(dist >= 0, jnp.where(dist < w, 1.0, 0.0), 0.0).astype(BF16)


def _inv_count(chunk_index, w):
    row = chunk_index * CHUNK + lax.broadcasted_iota(jnp.int32, (CHUNK, 1), 0)
    cnt = jnp.clip(row - PAD_ROWS + 1, 1, w).astype(F32)
    return 1.0 / cnt


def _pool_diff(u_cur, u_prev, i, w):
    ws = _apply01(_band(w, False, False), u_cur, 3)
    ws = ws + jnp.where(i > 0, _apply01(_band(w, False, True), u_prev, 3), 0.0)
    return ws * _inv_count(i, w) - u_cur


def _pool_merge_fwd(rest, o, w_pool, scale, d):
    rows = rest.shape[0]
    n = rows // CHUNK
    cg = d // len(POOL_WINDOWS)

    def body(up_ref, uc_ref, gp_ref, ga_ref, o_ref, wp_ref, sc_ref, mg_ref, yp_ref):
        i = pl.program_id(0)
        for g, w in enumerate(POOL_WINDOWS):
            sl = slice(g * cg, (g + 1) * cg)
            diff = _pool_diff(uc_ref[:, sl], up_ref[:, sl], i, w)
            ypre = jnp.dot(diff.astype(BF16), wp_ref[g], preferred_element_type=F32)
            yp_ref[:, sl] = ypre
            merged = (jax.nn.sigmoid(gp_ref[:, sl]) * (ypre * sc_ref[:, sl])
                      + jax.nn.sigmoid(ga_ref[:, sl]) * o_ref[:, sl])
            mg_ref[:, sl] = merged.astype(BF16)

    col = lambda c: pl.BlockSpec((CHUNK, d), lambda i: (i, c))
    return pl.pallas_call(
        body, name="pool_merge_fwd",
        out_shape=(jax.ShapeDtypeStruct((rows, d), BF16), jax.ShapeDtypeStruct((rows, d), F32)),
        grid=(n,),
        in_specs=[pl.BlockSpec((CHUNK, d), lambda i: (jnp.maximum(i - 1, 0), 0)), col(0), col(1), col(2), col(0),
                  pl.BlockSpec((len(POOL_WINDOWS), cg, cg), lambda i: (0, 0, 0)),
                  pl.BlockSpec((1, d), lambda i: (0, 0))],
        out_specs=(col(0), col(0)),
        compiler_params=_params(("parallel",)),
    )(rest, rest, rest, rest, o, w_pool, scale.reshape(1, d))


def _gate_bwd(dm, rest, o, ypre, scale, d):
    rows = rest.shape[0]
    n = rows // CHUNK

    def body(dm_ref, gp_ref, ga_ref, o_ref, yp_ref, sc_ref, dgp_ref, dga_ref, do_ref, dl_ref, dy_ref, ds_ref):
        @pl.when(pl.program_id(0) == 0)
        def _():
            ds_ref[...] = jnp.zeros_like(ds_ref)

        dmv = dm_ref[...]
        sp = jax.nn.sigmoid(gp_ref[...])
        sa = jax.nn.sigmoid(ga_ref[...])
        ov = o_ref[...]
        ypre_v = yp_ref[...]
        sc = sc_ref[...]
        dgp_ref[...] = (dmv * (ypre_v * sc) * (sp * (1.0 - sp))).astype(BF16)
        dga_ref[...] = (dmv * ov * (sa * (1.0 - sa))).astype(BF16)
        t = dmv * sp
        dy_ref[...] = (t * sc).astype(BF16)
        ds_ref[...] += _fold8(t * ypre_v)
        dob = (dmv * sa).astype(BF16)
        do_ref[...] = dob
        prod = dob.astype(F32) * ov
        lo = _lane_lo()
        for pr in range(d // CHUNK):
            sl = slice(pr * CHUNK, (pr + 1) * CHUNK)
            tp = prod[:, sl]
            s_lo = jnp.where(lo, tp, 0.0).sum(axis=-1, keepdims=True)
            s_hi = jnp.where(lo, 0.0, tp).sum(axis=-1, keepdims=True)
            dl_ref[:, sl] = jnp.where(lo, s_lo, s_hi)

    col = lambda c: pl.BlockSpec((CHUNK, d), lambda i: (i, c))
    row_bf = jax.ShapeDtypeStruct((rows, d), BF16)
    outs = pl.pallas_call(
        body, name="gate_bwd",
        out_shape=(row_bf, row_bf, row_bf, jax.ShapeDtypeStruct((rows, d), F32), row_bf,
                   jax.ShapeDtypeStruct((8, d), F32)),
        grid=(n,),
        in_specs=[col(0), col(1), col(2), col(0), col(0), pl.BlockSpec((1, d), lambda i: (0, 0))],
        out_specs=(col(0), col(0), col(0), col(0), col(0), pl.BlockSpec((8, d), lambda i: (0, 0))),
        compiler_params=_params(("arbitrary",)),
    )(dm, rest, rest, o, ypre, scale.reshape(1, d))
    return outs[:5] + (outs[5].sum(axis=0),)


def _pool_bwd(dypre, rest, w_pool, d):
    rows = rest.shape[0]
    n = rows // CHUNK
    ng = len(POOL_WINDOWS)
    cg = d // ng
    nt = (((1,), (1,)), ((), ()))
    tn = (((0,), (0,)), ((), ()))

    def body(dc_ref, dn_ref, up_ref, uc_ref, wp_ref, du_ref, dw_ref):
        i = pl.program_id(0)

        @pl.when(i == 0)
        def _():
            dw_ref[...] = jnp.zeros_like(dw_ref)

        row = i * CHUNK + lax.broadcasted_iota(jnp.int32, (CHUNK, 1), 0)
        for g, w in enumerate(POOL_WINDOWS):
            sl = slice(g * cg, (g + 1) * cg)
            diff = _pool_diff(uc_ref[:, sl], up_ref[:, sl], i, w)
            dyc = dc_ref[:, sl]
            dw_ref[g] += lax.dot_general(diff.astype(BF16), dyc, tn, preferred_element_type=F32)
            wg = wp_ref[g]
            dd_cur = lax.dot_general(dyc, wg, nt, preferred_element_type=F32)
            dd_next = lax.dot_general(dn_ref[:, sl], wg, nt, preferred_element_type=F32)
            du = _apply01(_band(w, True, False), dd_cur * _inv_count(i, w), 2)
            du = du + jnp.where(i < n - 1, _apply01(_band(w, True, True), dd_next * _inv_count(i + 1, w), 2), 0.0)
            du = du - dd_cur
            du_ref[:, sl] = jnp.where(row >= PAD_ROWS, du, 0.0).astype(BF16)

    cur = pl.BlockSpec((CHUNK, d), lambda i: (i, 0))
    return pl.pallas_call(
        body, name="pool_bwd",
        out_shape=(jax.ShapeDtypeStruct((rows, d), BF16), jax.ShapeDtypeStruct((ng, cg, cg), F32)),
        grid=(n,),
        in_specs=[cur, pl.BlockSpec((CHUNK, d), lambda i: (jnp.minimum(i + 1, n - 1), 0)),
                  pl.BlockSpec((CHUNK, d), lambda i: (jnp.maximum(i - 1, 0), 0)), cur,
                  pl.BlockSpec((ng, cg, cg), lambda i: (0, 0, 0))],
        out_specs=(cur, pl.BlockSpec((ng, cg, cg), lambda i: (0, 0, 0))),
        compiler_params=_params(("arbitrary",)),
    )(dypre, dypre, rest, rest, w_pool)


def _loss_grad(h_res, target):
    rows, d = h_res.shape
    n = rows // CHUNK

    def body(h_ref, t_ref, dh_ref, acc_ref):
        i = pl.program_id(0)

        @pl.when(i == 0)
        def _():
            acc_ref[...] = jnp.zeros_like(acc_ref)
            dh_ref[...] = jnp.zeros_like(dh_ref)

        @pl.when(i > 0)
        def _():
            err = h_ref[...] - t_ref[...]
            dh_ref[...] = err * (1.0 / d)
            e2 = _fold8(err * err)
            part = e2[:, 0:CHUNK]
            for c in range(1, d // CHUNK):
                part = part + e2[:, c * CHUNK:(c + 1) * CHUNK]
            acc_ref[...] += part

    dh, acc = pl.pallas_call(
        body, name="loss_grad",
        out_shape=(jax.ShapeDtypeStruct((rows, d), F32), jax.ShapeDtypeStruct((8, CHUNK), F32)),
        grid=(n,),
        in_specs=[pl.BlockSpec((CHUNK, d), lambda i: (i, 0)),
                  pl.BlockSpec((CHUNK, d), lambda i: (jnp.maximum(i - 1, 0), 0))],
        out_specs=(pl.BlockSpec((CHUNK, d), lambda i: (i, 0)), pl.BlockSpec((8, CHUNK), lambda i: (0, 0))),
        compiler_params=_params(("arbitrary",)),
    )(h_res, target)
    return dh, (0.5 / d) * acc.sum()


def _adamw(slots, w, m, v, name):
    rows, cols = w.shape
    lanes = -(-cols // CHUNK) * CHUNK
    row_bytes = lanes * (N_DEV * slots.dtype.itemsize + 7 * 4)
    tr = max(t for t in _divisors(rows, 8) if t <= max(8, ADAM_TILE_BYTES // row_bytes))
    c1 = 1.0 - ADAM_B1 ** ADAM_STEP
    c2 = 1.0 - ADAM_B2 ** ADAM_STEP

    def body(s_ref, w_ref, m_ref, v_ref, g_ref, d_ref, mo_ref, vo_ref):
        g = s_ref[0].astype(F32)
        for k in range(1, N_DEV):
            g = g + s_ref[k].astype(F32)
        m_new = ADAM_B1 * m_ref[...] + (1.0 - ADAM_B1) * g
        v_new = ADAM_B2 * v_ref[...] + (1.0 - ADAM_B2) * (g * g)
        m_hat = m_new / c1
        v_hat = v_new / c2
        g_ref[...] = g
        d_ref[...] = -ADAM_LR * (m_hat / (jnp.sqrt(v_hat) + ADAM_EPS) + ADAM_WD * w_ref[...])
        mo_ref[...] = m_new
        vo_ref[...] = v_new

    tile = pl.BlockSpec((tr, cols), lambda i: (i, 0))
    sds = jax.ShapeDtypeStruct((rows, cols), F32)
    return pl.pallas_call(
        body, name=name,
        out_shape=(sds, sds, sds, sds),
        grid=(rows // tr,),
        in_specs=[pl.BlockSpec((N_DEV, tr, cols), lambda i: (0, i, 0)), tile, tile, tile],
        out_specs=(tile, tile, tile, tile),
        compiler_params=_params(("parallel",)),
    )(slots, w, m, v)


def _exchange(srcs, gather, name):
    n = len(srcs)
    shapes = [((N_DEV,) + s.shape) if gather else s.shape for s in srcs]

    def body(*refs):
        src_refs, out_refs = refs[:n], refs[n:2 * n]
        send_sems, recv_sems, local_sems = refs[2 * n:]
        x, y, c = lax.axis_index("x"), lax.axis_index("y"), lax.axis_index("c")
        me = 4 * x + 2 * y + c

        def payload(a, slot):
            return src_refs[a] if gather else src_refs[a].at[slot]

        own = [pltpu.make_async_copy(payload(a, me), out_refs[a].at[me], local_sems.at[a]) for a in range(n)]
        for cp in own:
            cp.start()
        sends, recvs = [], []
        for k in range(1, N_DEV):
            px = 1 - x if k & 4 else x
            py = 1 - y if k & 2 else y
            pc = 1 - c if k & 1 else c
            peer = 4 * px + 2 * py + pc
            for a in range(n):
                sems = dict(send_sem=send_sems.at[(k - 1) * n + a], recv_sem=recv_sems.at[(k - 1) * n + a],
                            device_id=(px, py, pc), device_id_type=MESH_ID)
                sends.append(pltpu.make_async_remote_copy(src_ref=payload(a, peer), dst_ref=out_refs[a].at[me], **sems))
                recvs.append(pltpu.make_async_remote_copy(src_ref=payload(a, peer), dst_ref=out_refs[a].at[peer], **sems))
        for cp in sends:
            cp.start()
        for cp in recvs:
            cp.wait_recv()
        for cp in sends:
            cp.wait_send()
        for cp in own:
            cp.wait()

    return pl.pallas_call(
        body, name=name,
        out_shape=tuple(jax.ShapeDtypeStruct(sh, s.dtype) for sh, s in zip(shapes, srcs)),
        in_specs=[pl.BlockSpec(memory_space=pl.ANY)] * n,
        out_specs=tuple([pl.BlockSpec(memory_space=pl.ANY)] * n),
        scratch_shapes=[pltpu.SemaphoreType.DMA(((N_DEV - 1) * n,)), pltpu.SemaphoreType.DMA(((N_DEV - 1) * n,)),
                        pltpu.SemaphoreType.DMA((n,))],
    )(*srcs)


HBM_SPEC = pl.BlockSpec(memory_space=pltpu.HBM)
SEM_SPEC = pl.BlockSpec(memory_space=pltpu.SEMAPHORE)
DATAFLOW = pltpu.SideEffectType.DATAFLOW_SIDE_EFFECTING


def _peer_copies(src_refs, land_refs, send_sems, recv_sems, gather):
    n = len(src_refs)
    x, y, c = lax.axis_index("x"), lax.axis_index("y"), lax.axis_index("c")
    me = 4 * x + 2 * y + c
    sends, lands = [], []
    for k in range(1, N_DEV):
        px = 1 - x if k & 4 else x
        py = 1 - y if k & 2 else y
        pc = 1 - c if k & 1 else c
        peer = 4 * px + 2 * py + pc
        for a in range(n):
            src = src_refs[a] if gather else src_refs[a].at[peer]
            sems = dict(send_sem=send_sems.at[(k - 1) * n + a], recv_sem=recv_sems.at[(k - 1) * n + a],
                        device_id=(px, py, pc), device_id_type=MESH_ID)
            sends.append(pltpu.make_async_remote_copy(src_ref=src, dst_ref=land_refs[a].at[me], **sems))
            lands.append(pltpu.make_async_remote_copy(src_ref=src, dst_ref=land_refs[a].at[peer], **sems))
    return sends, lands


def _exchange_start(srcs, gather, name):
    n = len(srcs)
    shapes = [((N_DEV,) + s.shape) if gather else s.shape for s in srcs]
    n_sem = (N_DEV - 1) * n

    def body(*refs):
        src_refs, land_refs = refs[:n], refs[n:2 * n]
        send_sems, recv_sems = refs[2 * n], refs[2 * n + 1]
        token = refs[-1]
        sends, _ = _peer_copies(src_refs, land_refs, send_sems, recv_sems, gather)
        for cp in sends:
            cp.start()
        token[...] = jnp.zeros_like(token)

    hbm = lambda arrays_shapes: [pltpu.HBM(sh, dt) for sh, dt in arrays_shapes]
    src_types = [(s.shape, s.dtype) for s in srcs]
    land_types = [(sh, s.dtype) for sh, s in zip(shapes, srcs)]
    outs = pl.pallas_call(
        body, name=name,
        out_shape=(pltpu.SemaphoreType.DMA((n_sem,)), pltpu.SemaphoreType.DMA((n_sem,)),
                   *hbm(src_types), *hbm(land_types), jax.ShapeDtypeStruct((8, CHUNK), F32)),
        in_specs=[HBM_SPEC] * (2 * n),
        out_specs=(SEM_SPEC, SEM_SPEC, *([HBM_SPEC] * (2 * n)), pl.BlockSpec(memory_space=pltpu.VMEM)),
        input_output_aliases={a: 2 + a for a in range(2 * n)},
        compiler_params=pltpu.CompilerParams(has_side_effects=DATAFLOW),
    )(*[pltpu.with_memory_space_constraint(s, pltpu.HBM) for s in srcs],
      *[pltpu.with_memory_space_constraint(lax.empty(sh, dt), pltpu.HBM) for sh, dt in land_types])
    return (outs[0], outs[1], outs[2:2 + n], outs[2 + n:2 + 2 * n]), outs[-1]


def _exchange_wait(handle, after, gather, name):
    send_sems, recv_sems, src_thru, land_thru = handle
    n = len(src_thru)

    def body(*refs):
        src_refs, land_refs = refs[:n], refs[n:2 * n]
        _, lands = _peer_copies(src_refs, land_refs, refs[2 * n], refs[2 * n + 1], gather)
        for cp in lands:
            cp.wait_send()
            cp.wait_recv()

    outs = pl.pallas_call(
        body, name=name,
        out_shape=tuple(pltpu.HBM(t.shape, t.dtype) for t in (*src_thru, *land_thru)),
        in_specs=[*([HBM_SPEC] * (2 * n)), SEM_SPEC, SEM_SPEC, pl.BlockSpec(memory_space=pl.ANY)],
        out_specs=tuple([HBM_SPEC] * (2 * n)),
        input_output_aliases={a: a for a in range(2 * n)},
        compiler_params=pltpu.CompilerParams(has_side_effects=DATAFLOW),
    )(*src_thru, *land_thru, send_sems, recv_sems, after)
    return outs[n:]


def _fill_own(land, own, me):
    return lax.dynamic_update_slice_in_dim(land, own[None].astype(land.dtype), me, axis=0)


BIG = ("w_in", "w_pool", "w_out", "w_gate", "w_up", "w_down")
SHARD_AXIS = dict(w_in=-1, w_pool=-2, w_out=-2, w_gate=-1, w_up=-1, w_down=-2)
SMALL = ("norm_mix_pre", "norm_mix_post", "norm_ffn_pre", "norm_ffn_post", "pool_scale")


def _join(g, axis):
    return jnp.concatenate([g[j] for j in range(N_DEV)], axis=axis)


def _split(full, axis):
    return jnp.stack(jnp.split(full, N_DEV, axis=axis))


def _rows2d(a):
    return a.reshape(-1, a.shape[-1])


def kernel(x, meta_tokens, norm_mix_pre, norm_mix_post, norm_ffn_pre, norm_ffn_post, w_in, b_forget, w_pool, pool_scale, w_out, w_ffn_gate, w_ffn_up, w_ffn_down, loss_target, m_meta_tokens, m_norm_mix_pre, m_norm_mix_post, m_norm_ffn_pre, m_norm_ffn_post, m_w_in, m_b_forget, m_w_pool, m_pool_scale, m_w_out, m_w_ffn_gate, m_w_ffn_up, m_w_ffn_down, v_meta_tokens, v_norm_mix_pre, v_norm_mix_post, v_norm_ffn_pre, v_norm_ffn_post, v_w_in, v_b_forget, v_w_pool, v_pool_scale, v_w_out, v_w_ffn_gate, v_w_ffn_up, v_w_ffn_down):
    x2 = x[0]
    target = loss_target[0]
    seq, d = x2.shape
    depth = w_in.shape[0]
    heads = d // HEAD_DIM
    ff = w_ffn_gate.shape[2] * N_DEV
    rows = PAD_ROWS + META_TOKENS + seq
    assert seq % CHUNK == 0 and d % (2 * CHUNK) == 0 and heads <= FORGET_PAD and depth >= 2
    me = 4 * lax.axis_index("x") + 2 * lax.axis_index("y") + lax.axis_index("c")

    big = dict(w_in=w_in, w_pool=w_pool, w_out=w_out, w_gate=w_ffn_gate, w_up=w_ffn_up, w_down=w_ffn_down)
    big_m = dict(w_in=m_w_in, w_pool=m_w_pool, w_out=m_w_out, w_gate=m_w_ffn_gate, w_up=m_w_ffn_up, w_down=m_w_ffn_down)
    big_v = dict(w_in=v_w_in, w_pool=v_w_pool, w_out=v_w_out, w_gate=v_w_ffn_gate, w_up=v_w_ffn_up, w_down=v_w_ffn_down)
    small = dict(norm_mix_pre=norm_mix_pre, norm_mix_post=norm_mix_post, norm_ffn_pre=norm_ffn_pre,
                 norm_ffn_post=norm_ffn_post, pool_scale=pool_scale)
    small_m = dict(norm_mix_pre=m_norm_mix_pre, norm_mix_post=m_norm_mix_post, norm_ffn_pre=m_norm_ffn_pre,
                   norm_ffn_post=m_norm_ffn_post, pool_scale=m_pool_scale)
    small_v = dict(norm_mix_pre=v_norm_mix_pre, norm_mix_post=v_norm_mix_post, norm_ffn_pre=v_norm_ffn_pre,
                   norm_ffn_post=v_norm_ffn_post, pool_scale=v_pool_scale)

    wire = {n: big[n].astype(BF16) for n in BIG}
    others = [n for n in BIG if n != "w_in"]
    gathered = _exchange([wire["w_in"][:1], meta_tokens], True, "gather_first")
    meta_full = _join(gathered[1], -1)
    later_srcs = [wire["w_in"][1:]] + [wire[n] for n in others]
    later, started = _exchange_start(later_srcs, True, "gather_later_start")

    def in_weights(lands):
        win = _join(lands, -1)
        fcol = 4 * d
        qkv_w = win[:, :, d:4 * d]
        rest_w = jnp.concatenate([win[:, :, :d], win[:, :, fcol + heads:], win[:, :, fcol:fcol + heads],
                                  jnp.zeros(win.shape[:2] + (FORGET_PAD - heads,), BF16)], axis=2)
        return qkv_w, rest_w, jnp.concatenate([qkv_w, rest_w], axis=2)

    first_in = in_weights(gathered[0])
    w_qkv, w_rest, w_cat, w_gu, w_o, w_dn, w_pl = (
        [None] * depth for _ in range(7))

    h_res = jnp.concatenate([jnp.zeros((PAD_ROWS, d), F32), meta_full, x2], axis=0)
    h1 = _norm_fwd(h_res, norm_mix_pre[0] + started[0, 0])
    ones = jnp.ones((d,), F32)
    saved = []
    for l in range(depth):
        w_qkv[l], w_rest[l], w_cat[l] = (t[0] for t in first_in) if l == 0 else (t[l - 1] for t in later_in)
        qkv = _matmul(h1, w_qkv[l], "nn", BF16, "proj_qkv")
        rest = _matmul(h1, w_rest[l], "nn", F32, "proj_rest")
        z = rest[:, 3 * d:3 * d + heads].T
        bias = jnp.broadcast_to(b_forget[l][:, None], (heads, CHUNK))
        q_aug, k_aug, v_own, v_aug = _attn_prep(qkv, _logf_fwd(z, bias).T, d)
        o, lse = _attn_fwd(q_aug, k_aug, v_aug, d)
        if l == 0:
            lands = _exchange_wait(later, o, True, "gather_later_wait")
            lands = [_fill_own(g, src, me) for g, src in zip(lands, later_srcs)]
            later_in = in_weights(lands[0])
            full = {n: _join(g, SHARD_AXIS[n]) for n, g in zip(others, lands[1:])}
            gate_up = jnp.concatenate([full["w_gate"], full["w_up"]], axis=2)
        w_gu[l], w_o[l], w_dn[l], w_pl[l] = gate_up[l], full["w_out"][l], full["w_down"][l], full["w_pool"][l]
        merged, ypre = _pool_merge_fwd(rest, o, w_pl[l], pool_scale[l], d)
        mix = _matmul(merged, w_o[l], "nn", F32, "mix_out")
        h_mid, h2 = _resid_norm_fwd(h_res, mix, norm_mix_post[l], norm_ffn_pre[l])
        ab = _matmul(h2, w_gu[l], "nn", F32, "ffn_in")
        act = _swiglu_fwd(ab)
        ffo = _matmul(act, w_dn[l], "nn", F32, "ffn_out")
        g_next = norm_mix_pre[l + 1] if l + 1 < depth else ones
        h_next, h1_next = _resid_norm_fwd(h_mid, ffo, norm_ffn_post[l], g_next)
        saved.append(dict(h_in=h_res, h1=h1, q_aug=q_aug, k_aug=k_aug, v_own=v_own, rest=rest, z=z,
                          bias=bias, o=o, lse=lse,
                          merged=merged, ypre=ypre, mix=mix, h_mid=h_mid, h2=h2, ab=ab, act=act, ffo=ffo))
        h_res, h1 = h_next, h1_next

    dh, loss_local = _loss_grad(h_res, target)
    loss = lax.psum(loss_local, ("x", "y", "c"))

    grads = {n: [None] * depth for n in BIG + SMALL + ("b_forget",)}

    def grad_slots(n, first, last):
        part = _split(jnp.stack(grads[n][first:last]), SHARD_AXIS[n]).astype(BF16)
        return part.reshape(N_DEV, -1, part.shape[-1])

    last_made = ("w_in", "w_pool")

    for l in reversed(range(depth)):
        s = saved[l]
        dffo, grads["norm_ffn_post"][l] = _norm_bwd(s["ffo"], norm_ffn_post[l], dh, None, BF16)
        grads["w_down"][l] = _matmul(s["act"].T, dffo, "nn", F32, "grad_w_down")
        dact = _matmul(dffo, w_dn[l], "nt", F32, "ffn_out_dx")
        dab = _swiglu_bwd(s["ab"], dact)
        dgu = _matmul(s["h2"].T, dab, "nn", F32, "grad_w_gu")
        grads["w_gate"][l], grads["w_up"][l] = dgu[:, :ff], dgu[:, ff:]
        dh2 = _matmul(dab, w_gu[l], "nt", F32, "ffn_in_dx")
        dh_mid, grads["norm_ffn_pre"][l] = _norm_bwd(s["h_mid"], norm_ffn_pre[l], dh2, dh, F32)

        dmix, grads["norm_mix_post"][l] = _norm_bwd(s["mix"], norm_mix_post[l], dh_mid, None, BF16)
        grads["w_out"][l] = _matmul(s["merged"].T, dmix, "nn", F32, "grad_w_out")
        dm = _matmul(dmix, w_o[l], "nt", F32, "mix_out_dx")
        scale_l = pool_scale[l]
        if l == 0:
            early_slots = [grad_slots(n, 1 if n in last_made else 0, depth) for n in BIG]
            early, started = _exchange_start(early_slots, False, "scatter_early_start")
            scale_l = scale_l + started[0, 0]
        dgp, dga, do, delta, dypre, grads["pool_scale"][l] = _gate_bwd(dm, s["rest"], s["o"], s["ypre"], scale_l, d)
        du, grads["w_pool"][l] = _pool_bwd(dypre, s["rest"], w_pl[l], d)
        dq, dk, dv, df_key, df_query = _attn_bwd(s["q_aug"], s["k_aug"], s["v_own"], do, s["lse"], delta, d)
        df = df_key.reshape(heads, rows) + df_query.reshape(rows, heads, HEAD_DIM)[:, :, 0].T
        dz, grads["b_forget"][l] = _logf_bwd(df, s["z"], s["bias"])
        dzt = jnp.pad(dz.T.astype(BF16), ((0, 0), (0, FORGET_PAD - heads)))
        dproj = jnp.concatenate([dq, dk, dv, du, dgp, dga, dzt], axis=1)
        dwc = _matmul(s["h1"].T, dproj, "nn", F32, "grad_w_in")
        grads["w_in"][l] = jnp.concatenate([dwc[:, 3 * d:4 * d], dwc[:, :3 * d], dwc[:, 6 * d:6 * d + heads],
                                            dwc[:, 4 * d:6 * d]], axis=1)
        dh1 = _matmul(dproj, w_cat[l], "nt", F32, "proj_dx")
        dh, grads["norm_mix_pre"][l] = _norm_bwd(s["h_in"], norm_mix_pre[l], dh1, dh_mid, F32)

    grad_x = dh[PAD_ROWS + META_TOKENS:][None]
    dmeta = dh[PAD_ROWS:PAD_ROWS + META_TOKENS]

    late_recv = dict(zip(last_made, _exchange([grad_slots(n, 0, 1) for n in last_made], False, "scatter_last")))
    early_recv = _exchange_wait(early, dh, False, "scatter_early_wait")
    early_recv = [_fill_own(r, lax.dynamic_index_in_dim(s_, me, 0, keepdims=False), me)
                  for r, s_ in zip(early_recv, early_slots)]
    recv = [jnp.concatenate([late_recv[n], r], axis=1) if n in last_made else r for n, r in zip(BIG, early_recv)]
    big_out = {}
    for n, r in zip(BIG, recv):
        outs = _adamw(r, _rows2d(big[n]), _rows2d(big_m[n]), _rows2d(big_v[n]), "adamw_" + n)
        big_out[n] = [o_.reshape(big[n].shape) for o_ in outs]

    def table(parts, forget):
        t = jnp.concatenate([_rows2d(p) for p in parts] + [jnp.pad(forget, ((0, 0), (0, d - heads)))], axis=0)
        return jnp.pad(t, ((0, -t.shape[0] % 8), (0, 0)))

    g_table = table([jnp.stack(grads[n]) for n in SMALL], jnp.stack(grads["b_forget"]))
    rep_rows = g_table.shape[0]
    got = _exchange([jnp.concatenate([g_table, dmeta], axis=0)], True, "gather_small_grads")[0]
    outs = _adamw(got[:, :rep_rows], table([small[n] for n in SMALL], b_forget),
                  table([small_m[n] for n in SMALL], m_b_forget), table([small_v[n] for n in SMALL], v_b_forget),
                  "adamw_small")
    dcols = d // N_DEV
    meta_slots = lax.dynamic_slice_in_dim(got[:, rep_rows:], me * dcols, dcols, axis=2)
    meta_out = _adamw(meta_slots, meta_tokens, m_meta_tokens, v_meta_tokens, "adamw_meta")

    def ordered(k):
        t = outs[k]
        so = {n: t[a * depth:(a + 1) * depth] for a, n in enumerate(SMALL)}
        forget = t[len(SMALL) * depth:(len(SMALL) + 1) * depth, :heads]
        return (meta_out[k], so["norm_mix_pre"], so["norm_mix_post"], so["norm_ffn_pre"], so["norm_ffn_post"],
                big_out["w_in"][k], forget, big_out["w_pool"][k], so["pool_scale"], big_out["w_out"][k],
                big_out["w_gate"][k], big_out["w_up"][k], big_out["w_down"][k])

    return (loss, grad_x) + ordered(0) + ordered(1) + ordered(2) + ordered(3)
```
